```python
import math
import jax, jax.numpy as jnp
from jax import lax
import numpy as np

D_MODEL = 1024
BATCH = 16
SEQ = 256
DEPTH = 2
DEC_BATCH = 8
DEC_SEQ = 2048
PAST_LEN = 256

GRID_W = 64
Q_BLOCK = 128
ROPE_BASE = 10000.0
EPS = 1e-6
N_EVEN = (DEPTH + 1) // 2
N_ODD = DEPTH // 2

CONV_CH = D_MODEL // 2
CONV_WIDTH = 31
CONV_PAD = CONV_WIDTH // 2
MLA_HEADS = 8
MLA_NOPE = 64
MLA_ROPE = 32
MLA_QK = MLA_NOPE + MLA_ROPE
MLA_V = 64
MLA_Q_RANK = 384
MLA_KV_RANK = 256
EVEN_IN = 2 * CONV_CH + MLA_Q_RANK + MLA_KV_RANK + MLA_ROPE
EVEN_OUT = CONV_CH + MLA_HEADS * MLA_V
DIFF_HEADS = 8
DIFF_DK = 64
DIFF_DV = 2 * DIFF_DK
DIFF_OUT = DIFF_HEADS * DIFF_DV
N_GROUPS = 4
EXPERTS_PER_GROUP = 8
N_EXPERTS = N_GROUPS * EXPERTS_PER_GROUP
TOP_K_IN_GROUP = 2
EXPERT_FF = 256

kernel_name = "hybrid_diffusion_conv_mla_diffattn_hmoe_step"


def rms_norm(x, g):
    xf = x.astype(jnp.float32)
    y = xf * lax.rsqrt(jnp.mean(xf * xf, axis=-1, keepdims=True) + EPS)
    return (y * g.astype(jnp.float32)).astype(x.dtype)


def layer_norm(x, g, b):
    xf = x.astype(jnp.float32)
    mu = jnp.mean(xf, axis=-1, keepdims=True)
    var = jnp.mean(jnp.square(xf - mu), axis=-1, keepdims=True)
    y = (xf - mu) * lax.rsqrt(var + EPS)
    return (y * g.astype(jnp.float32) + b.astype(jnp.float32)).astype(x.dtype)


def axial_rope_tables(n_tok, rot_dim):
    rows = n_tok // GRID_W
    row_ids = jnp.repeat(jnp.arange(rows, dtype=jnp.float32), GRID_W)
    col_ids = jnp.tile(jnp.arange(GRID_W, dtype=jnp.float32), rows)
    quarter = rot_dim // 4
    freqs = jnp.power(ROPE_BASE, -jnp.arange(quarter, dtype=jnp.float32) / quarter)
    ang = jnp.concatenate([row_ids[:, None] * freqs, col_ids[:, None] * freqs], axis=-1)
    return jnp.cos(ang), jnp.sin(ang)


def apply_rope(x, cos, sin):
    half = x.shape[-1] // 2
    shp = (cos.shape[0],) + (1,) * (x.ndim - 3) + (half,)
    c = cos.reshape(shp).astype(x.dtype)
    s = sin.reshape(shp).astype(x.dtype)
    x1, x2 = x[..., :half], x[..., half:]
    return jnp.concatenate([x1 * c - x2 * s, x1 * s + x2 * c], axis=-1)


def attend(q, k, v, scale):
    b, s = q.shape[0], q.shape[1]
    nb = s // Q_BLOCK
    qb = q.reshape((b, nb, Q_BLOCK) + q.shape[2:]).swapaxes(0, 1)

    def block(qi):
        sc = jnp.einsum('bqhmd,bkhmd->bhmqk', qi, k).astype(jnp.float32) * scale
        p = jax.nn.softmax(sc, axis=-1).astype(v.dtype)
        return jnp.einsum('bhmqk,bkhd->bqhmd', p, v)

    o = lax.map(block, qb)
    return o.swapaxes(0, 1).reshape((b, s) + o.shape[3:])


def conformer_conv(glu_in, w_dw, b_dw, ln_g, ln_b):
    a, gte = jnp.split(glu_in, 2, axis=-1)
    u = a * jax.nn.sigmoid(gte)
    y = lax.conv_general_dilated(u, w_dw[:, None, :].astype(u.dtype), window_strides=(1,),
                                 padding=[(CONV_PAD, CONV_PAD)],
                                 dimension_numbers=('NWC', 'WIO', 'NWC'),
                                 feature_group_count=CONV_CH) + b_dw
    return jax.nn.silu(layer_norm(y, ln_g, ln_b))


def mla_keys(c_kv, k_rope, w_kvb, kn_g):
    b, t = c_kv.shape[0], c_kv.shape[1]
    kv = (c_kv @ w_kvb).reshape(b, t, MLA_HEADS, MLA_NOPE + MLA_V)
    k_nope, v = kv[..., :MLA_NOPE], kv[..., MLA_NOPE:]
    k_r = jnp.broadcast_to(k_rope[:, :, None, :], (b, t, MLA_HEADS, MLA_ROPE))
    k = rms_norm(jnp.concatenate([k_nope, k_r], axis=-1), kn_g)
    return k, v


def even_mixer(h, P, e, ctx):
    b, s, _ = h.shape
    proj = h @ P['ev_w_in'][e]
    o1 = 2 * CONV_CH
    o2 = o1 + MLA_Q_RANK
    o3 = o2 + MLA_KV_RANK
    glu_in, q_a, kv_a, k_rope = proj[..., :o1], proj[..., o1:o2], proj[..., o2:o3], proj[..., o3:]
    conv_out = conformer_conv(glu_in, P['ev_conv_w'][e], P['ev_conv_b'][e],
                              P['ev_conv_ln_g'][e], P['ev_conv_ln_b'][e])
    q = (rms_norm(q_a, P['ev_q_norm_g'][e]) @ P['ev_w_qb'][e]).reshape(b, s, MLA_HEADS, MLA_QK)
    q = rms_norm(q, P['ev_qn_g'][e])
    c_kv = rms_norm(kv_a, P['ev_kv_norm_g'][e])
    k, v = mla_keys(c_kv, k_rope, P['ev_w_kvb'][e], P['ev_kn_g'][e])
    if ctx is not None:
        cos, sin = axial_rope_tables(s, MLA_ROPE)
        q = jnp.concatenate([q[..., :MLA_NOPE], apply_rope(q[..., MLA_NOPE:], cos, sin)], axis=-1)
        k = jnp.concatenate([k[..., :MLA_NOPE], apply_rope(k[..., MLA_NOPE:], cos, sin)], axis=-1)
        k_c, v_c = mla_keys(ctx[0], ctx[1], P['ev_w_kvb'][e], P['ev_kn_g'][e])
        k = jnp.concatenate([k, k_c], axis=1)
        v = jnp.concatenate([v, v_c], axis=1)
    o = attend(q[:, :, :, None, :], k[:, :, :, None, :], v, MLA_QK ** -0.5)[:, :, :, 0, :]
    out = jnp.concatenate([conv_out, o.reshape(b, s, MLA_HEADS * MLA_V)], axis=-1) @ P['ev_w_o'][e]
    return out, (c_kv, k_rope)


def diff_mixer(h, P, o_idx, lambda_init, ctx):
    b, s, _ = h.shape
    qkv = h @ P['od_w_qkv'][o_idx]
    q, k, v = jnp.split(qkv, 3, axis=-1)
    q = rms_norm(q.reshape(b, s, DIFF_HEADS, 2, DIFF_DK), P['od_qn_g'][o_idx])
    k = rms_norm(k.reshape(b, s, DIFF_HEADS, 2, DIFF_DK), P['od_kn_g'][o_idx])
    v = v.reshape(b, s, DIFF_HEADS, DIFF_DV)
    cache = (k, v)
    if ctx is not None:
        cos, sin = axial_rope_tables(s, DIFF_DK)
        q = apply_rope(q, cos, sin)
        k = jnp.concatenate([apply_rope(k, cos, sin), ctx[0]], axis=1)
        v = jnp.concatenate([v, ctx[1]], axis=1)
    o = attend(q, k, v, DIFF_DK ** -0.5)
    f32 = jnp.float32
    lam = (jnp.exp(jnp.sum(P['od_lambda_q1'][o_idx].astype(f32) * P['od_lambda_k1'][o_idx].astype(f32)))
           - jnp.exp(jnp.sum(P['od_lambda_q2'][o_idx].astype(f32) * P['od_lambda_k2'][o_idx].astype(f32)))
           + lambda_init)
    o = o[..., 0, :] - lam.astype(o.dtype) * o[..., 1, :]
    o = rms_norm(o, P['od_subln_g'][o_idx]) * (1.0 - lambda_init)
    return o.reshape(b, s, DIFF_OUT) @ P['od_w_o'][o_idx], cache


def hier_moe(h, P, l):
    b, s, d = h.shape
    t = h.reshape(-1, d)
    f32 = jnp.float32
    grp_prob = jax.nn.softmax((t @ P['moe_wg'][l]).astype(f32) + P['moe_bg'][l].astype(f32), axis=-1)
    grp_gate, grp_idx = lax.top_k(grp_prob, 1)
    exp_logits = ((t @ P['moe_we'][l]).astype(f32) + P['moe_be'][l].astype(f32)).reshape(-1, N_GROUPS, EXPERTS_PER_GROUP)
    sel = jnp.take_along_axis(exp_logits, grp_idx[:, :, None], axis=1)[:, 0, :]
    top_p, top_i = lax.top_k(jax.nn.softmax(sel, axis=-1), TOP_K_IN_GROUP)
    w = grp_gate * top_p / jnp.sum(top_p, axis=-1, keepdims=True)
    expert_ids = grp_idx * EXPERTS_PER_GROUP + top_i
    combine = jnp.sum(jax.nn.one_hot(expert_ids, N_EXPERTS, dtype=f32) * w[..., None], axis=1)

    def body(acc, xs):
        wg_e, wu_e, wd_e, cw = xs
        y = (jax.nn.silu(t @ wg_e) * (t @ wu_e)) @ wd_e
        return acc + cw[:, None].astype(t.dtype) * y, None

    acc, _ = lax.scan(body, jnp.zeros_like(t),
                      (P['moe_w_gate'][l], P['moe_w_up'][l], P['moe_w_down'][l], combine.T))
    return acc.reshape(b, s, d)


def trunk(x, cond, caches, P):
    mla_states, diff_states = [], []
    for l in range(DEPTH):
        mod = (jax.nn.silu(cond) @ P['ada_w'][l] + P['ada_b'][l])[:, None, :]
        sh1, sc1, g1, sh2, sc2, g2 = jnp.split(mod, 6, axis=-1)
        h = rms_norm(x, P['norm1_g'][l]) * (1 + sc1) + sh1
        if l % 2 == 0:
            e = l // 2
            ctx = None if caches is None else (caches[0][:, e], caches[1][:, e])
            out, st = even_mixer(h, P, e, ctx)
            mla_states.append(st)
        else:
            o_idx = l // 2
            ctx = None if caches is None else (caches[2][:, o_idx], caches[3][:, o_idx])
            lambda_init = 0.8 - 0.6 * math.exp(-0.3 * l)
            out, st = diff_mixer(h, P, o_idx, lambda_init, ctx)
            diff_states.append(st)
        x = x + g1 * out
        h = rms_norm(x, P['norm2_g'][l]) * (1 + sc2) + sh2
        x = x + g2 * hier_moe(h, P, l)
    return x, mla_states, diff_states


def setup_inputs(seed: int = 0) -> dict:
    key = jax.random.key(seed)
    ks = iter(jax.random.split(key, 64))
    f32 = jnp.float32

    def nrm(shape, scale=1.0):
        return jax.random.normal(next(ks), shape, f32) * scale

    def gain(shape):
        return 1.0 + nrm(shape, 0.02)

    D = D_MODEL
    return {
        'x_prompt': nrm((BATCH, SEQ, D)),
        'x_sample': nrm((DEC_BATCH, DEC_SEQ, D)),
        'cache_mla_ckv': nrm((DEC_BATCH, N_EVEN, PAST_LEN, MLA_KV_RANK)),
        'cache_mla_krope': nrm((DEC_BATCH, N_EVEN, PAST_LEN, MLA_ROPE)),
        'cache_diff_k': nrm((DEC_BATCH, N_ODD, PAST_LEN, DIFF_HEADS, 2, DIFF_DK)),
        'cache_diff_v': nrm((DEC_BATCH, N_ODD, PAST_LEN, DIFF_HEADS, DIFF_DV)),
        'c': nrm((DEC_BATCH, D)),
        'c_ctx': nrm((D,)),
        'ada_w': nrm((DEPTH, D, 6 * D), 0.5 * D ** -0.5),
        'ada_b': nrm((DEPTH, 6 * D), 0.02),
        'norm1_g': gain((DEPTH, D)),
        'norm2_g': gain((DEPTH, D)),
        'ev_w_in': nrm((N_EVEN, D, EVEN_IN), D ** -0.5),
        'ev_conv_w': nrm((N_EVEN, CONV_WIDTH, CONV_CH), CONV_WIDTH ** -0.5),
        'ev_conv_b': nrm((N_EVEN, CONV_CH), 0.02),
        'ev_conv_ln_g': gain((N_EVEN, CONV_CH)),
        'ev_conv_ln_b': nrm((N_EVEN, CONV_CH), 0.02),
        'ev_q_norm_g': gain((N_EVEN, MLA_Q_RANK)),
        'ev_w_qb': nrm((N_EVEN, MLA_Q_RANK, MLA_HEADS * MLA_QK), MLA_Q_RANK ** -0.5),
        'ev_kv_norm_g': gain((N_EVEN, MLA_KV_RANK)),
        'ev_w_kvb': nrm((N_EVEN, MLA_KV_RANK, MLA_HEADS * (MLA_NOPE + MLA_V)), MLA_KV_RANK ** -0.5),
        'ev_qn_g': gain((N_EVEN, MLA_QK)),
        'ev_kn_g': gain((N_EVEN, MLA_QK)),
        'ev_w_o': nrm((N_EVEN, EVEN_OUT, D), EVEN_OUT ** -0.5),
        'od_w_qkv': nrm((N_ODD, D, 3 * DIFF_OUT), D ** -0.5),
        'od_qn_g': gain((N_ODD, DIFF_DK)),
        'od_kn_g': gain((N_ODD, DIFF_DK)),
        'od_lambda_q1': nrm((N_ODD, DIFF_DK), 0.1),
        'od_lambda_k1': nrm((N_ODD, DIFF_DK), 0.1),
        'od_lambda_q2': nrm((N_ODD, DIFF_DK), 0.1),
        'od_lambda_k2': nrm((N_ODD, DIFF_DK), 0.1),
        'od_subln_g': gain((N_ODD, DIFF_DV)),
        'od_w_o': nrm((N_ODD, DIFF_OUT, D), DIFF_OUT ** -0.5),
        'moe_wg': nrm((DEPTH, D, N_GROUPS), D ** -0.5),
        'moe_bg': nrm((DEPTH, N_GROUPS), 0.01),
        'moe_we': nrm((DEPTH, D, N_EXPERTS), D ** -0.5),
        'moe_be': nrm((DEPTH, N_EXPERTS), 0.01),
        'moe_w_gate': nrm((DEPTH, N_EXPERTS, D, EXPERT_FF), D ** -0.5),
        'moe_w_up': nrm((DEPTH, N_EXPERTS, D, EXPERT_FF), D ** -0.5),
        'moe_w_down': nrm((DEPTH, N_EXPERTS, EXPERT_FF, D), EXPERT_FF ** -0.5),
    }


def reference(x_prompt, x_sample, cache_mla_ckv, cache_mla_krope, cache_diff_k, cache_diff_v, c, c_ctx,
              ada_w, ada_b, norm1_g, norm2_g,
              ev_w_in, ev_conv_w, ev_conv_b, ev_conv_ln_g, ev_conv_ln_b, ev_q_norm_g, ev_w_qb,
              ev_kv_norm_g, ev_w_kvb, ev_qn_g, ev_kn_g, ev_w_o,
              od_w_qkv, od_qn_g, od_kn_g, od_lambda_q1, od_lambda_k1, od_lambda_q2, od_lambda_k2,
              od_subln_g, od_w_o,
              moe_wg, moe_bg, moe_we, moe_be, moe_w_gate, moe_w_up, moe_w_down):
    P = dict(ada_w=ada_w, ada_b=ada_b, norm1_g=norm1_g, norm2_g=norm2_g,
             ev_w_in=ev_w_in, ev_conv_w=ev_conv_w, ev_conv_b=ev_conv_b, ev_conv_ln_g=ev_conv_ln_g,
             ev_conv_ln_b=ev_conv_ln_b, ev_q_norm_g=ev_q_norm_g, ev_w_qb=ev_w_qb,
             ev_kv_norm_g=ev_kv_norm_g, ev_w_kvb=ev_w_kvb, ev_qn_g=ev_qn_g, ev_kn_g=ev_kn_g, ev_w_o=ev_w_o,
             od_w_qkv=od_w_qkv, od_qn_g=od_qn_g, od_kn_g=od_kn_g, od_lambda_q1=od_lambda_q1,
             od_lambda_k1=od_lambda_k1, od_lambda_q2=od_lambda_q2, od_lambda_k2=od_lambda_k2,
             od_subln_g=od_subln_g, od_w_o=od_w_o,
             moe_wg=moe_wg, moe_bg=moe_bg, moe_we=moe_we, moe_be=moe_be,
             moe_w_gate=moe_w_gate, moe_w_up=moe_w_up, moe_w_down=moe_w_down)
    y_prompt, mla_states, diff_states = trunk(x_prompt, c_ctx[None, :], None, P)
    y_sample, _, _ = trunk(x_sample, c, (cache_mla_ckv, cache_mla_krope, cache_diff_k, cache_diff_v), P)
    new_mla_ckv = jnp.stack([st[0] for st in mla_states], axis=1)
    new_mla_krope = jnp.stack([st[1] for st in mla_states], axis=1)
    new_diff_k = jnp.stack([st[0] for st in diff_states], axis=1)
    new_diff_v = jnp.stack([st[1] for st in diff_states], axis=1)
    return (y_prompt, y_sample, new_mla_ckv, new_mla_krope, new_diff_k, new_diff_v)
```

```python
import functools
import math

import jax
import jax.numpy as jnp
from jax import lax
from jax.experimental import pallas as pl
from jax.experimental.pallas import tpu as pltpu

F32 = jnp.float32
BF16 = jnp.bfloat16

D_MODEL = 1024
BATCH = 16
SEQ = 256
DEC_BATCH = 8
DEC_SEQ = 2048
PAST_LEN = 256
GRID_W = 64
ROPE_BASE = 10000.0
EPS = 1e-6

CONV_CH = 512
CONV_WIDTH = 31
CONV_PAD = 15
MLA_HEADS = 8
MLA_NOPE = 64
MLA_ROPE = 32
MLA_QK = 96
MLA_V = 64
MLA_Q_RANK = 384
MLA_KV_RANK = 256
DIFF_HEADS = 8
DIFF_DK = 64
DIFF_DV = 128
N_GROUPS = 4
EXPERTS_PER_GROUP = 8
N_EXPERTS = 32
EXPERT_FF = 256

LANES = 128
N_CTX = BATCH * SEQ
N_LAT = DEC_BATCH * DEC_SEQ
N_TOK = N_CTX + N_LAT
N_CACHE = DEC_BATCH * PAST_LEN
MOD_ROWS = 16
VMEM_LIMIT = 48 * 1024 * 1024

TM = 512
NB_CTX = N_CTX // TM
NB_TOK = N_TOK // TM
BLK_PER_LAT = DEC_SEQ // TM
TQ = 256
CONV_T = 256
CONV_HALO = 16
CONV_RB = 32
TM_MOE = 1024


def _params(*sem):
    return pltpu.CompilerParams(dimension_semantics=sem, vmem_limit_bytes=VMEM_LIMIT)


def _sigmoid(x):
    return 1.0 / (1.0 + jnp.exp(-x))


def _rms(x, g):
    return x * lax.rsqrt(jnp.mean(x * x, axis=-1, keepdims=True) + EPS) * g


def _dot(a, b):
    return jnp.dot(a, b, preferred_element_type=F32)


def _dot_nt(a, b):
    return lax.dot_general(a, b, (((1,), (1,)), ((), ())), preferred_element_type=F32)


def _seq_of_block(i):
    return jnp.where(i < NB_CTX, 0, 1 + (i - NB_CTX) // BLK_PER_LAT)


def _mod_spec(layer, slot):
    return pl.BlockSpec((1, 1, D_MODEL), lambda i: ((layer * MOD_ROWS + _seq_of_block(i)) * 6 + slot, 0, 0))


def _row_spec(width, tm=TM):
    return pl.BlockSpec((tm, width), lambda i: (i, 0))


def _full_spec(shape):
    nd = len(shape)
    return pl.BlockSpec(shape, lambda *_: (0,) * nd)


def _mod_kernel(c_ref, w_ref, b_ref, o_ref):
    c = c_ref[...]
    s = (c * _sigmoid(c)).astype(BF16)
    o_ref[0] = _dot(s, w_ref[0].astype(BF16)) + b_ref[0]


def _mod_call(cond, ada_w, ada_b):
    depth = ada_w.shape[0]
    tn = 1024
    return pl.pallas_call(
        _mod_kernel,
        grid=(depth, 6 * D_MODEL // tn),
        in_specs=[
            pl.BlockSpec((MOD_ROWS, D_MODEL), lambda l, j: (0, 0)),
            pl.BlockSpec((1, D_MODEL, tn), lambda l, j: (l, 0, j)),
            pl.BlockSpec((1, 1, tn), lambda l, j: (l, 0, j)),
        ],
        out_specs=pl.BlockSpec((1, MOD_ROWS, tn), lambda l, j: (l, 0, j)),
        out_shape=jax.ShapeDtypeStruct((depth, MOD_ROWS, 6 * D_MODEL), F32),
        compiler_params=_params("parallel", "parallel"),
        name="adaln_mod",
    )(cond, ada_w, ada_b.reshape(depth, 1, 6 * D_MODEL))


EV_W = 1792


def _inproj0_kernel(x_ref, g_ref, sh_ref, sc_ref, w_ref, qg_ref, kvg_ref, u_ref, qn_ref, ckv_ref, kr_ref):
    h = _rms(x_ref[...], g_ref[...]) * (1.0 + sc_ref[0]) + sh_ref[0]
    p = _dot(h.astype(BF16), w_ref[...])
    u_ref[...] = p[:, :CONV_CH] * _sigmoid(p[:, CONV_CH:2 * CONV_CH])
    o1 = 2 * CONV_CH
    o2 = o1 + MLA_Q_RANK
    o3 = o2 + MLA_KV_RANK
    qn_ref[...] = _rms(p[:, o1:o2], qg_ref[...]).astype(BF16)
    ckv_ref[...] = _rms(p[:, o2:o3], kvg_ref[...])
    kr_ref[...] = p[:, o3:o3 + LANES]


def _inproj0_call(x, mod, g1, w_in, qg, kvg):
    return pl.pallas_call(
        _inproj0_kernel,
        grid=(NB_TOK,),
        in_specs=[
            _row_spec(D_MODEL), _full_spec((1, D_MODEL)), _mod_spec(0, 0), _mod_spec(0, 1),
            _full_spec((D_MODEL, EV_W)), _full_spec((1, MLA_Q_RANK)), _full_spec((1, MLA_KV_RANK)),
        ],
        out_specs=[_row_spec(CONV_CH), _row_spec(MLA_Q_RANK), _row_spec(MLA_KV_RANK), _row_spec(LANES)],
        out_shape=[
            jax.ShapeDtypeStruct((N_TOK, CONV_CH), F32),
            jax.ShapeDtypeStruct((N_TOK, MLA_Q_RANK), BF16),
            jax.ShapeDtypeStruct((N_TOK, MLA_KV_RANK), F32),
            jax.ShapeDtypeStruct((N_TOK, LANES), F32),
        ],
        compiler_params=_params("parallel"),
        name="inproj0",
    )(x, g1, mod, mod, w_in, qg, kvg)


CONV_CHUNKS_PER_LAT = DEC_SEQ // CONV_T
CONV_NB_CTX = N_CTX // CONV_T
CONV_NB = N_TOK // CONV_T


def _conv_kernel(prev_ref, cur_ref, nxt_ref, w_ref, b_ref, lg_ref, lb_ref, o_ref, buf):
    i = pl.program_id(0)
    j = (i - CONV_NB_CTX) % CONV_CHUNKS_PER_LAT
    first = jnp.logical_or(i < CONV_NB_CTX, j == 0)
    last = jnp.logical_or(i < CONV_NB_CTX, j == CONV_CHUNKS_PER_LAT - 1)
    buf[0:CONV_HALO, :] = jnp.where(first, 0.0, prev_ref[...])
    buf[CONV_HALO:CONV_HALO + CONV_T, :] = cur_ref[...]
    buf[CONV_HALO + CONV_T:, :] = jnp.where(last, 0.0, nxt_ref[...])
    for rb in range(CONV_T // CONV_RB):
        base = rb * CONV_RB + CONV_HALO - CONV_PAD
        acc = jnp.zeros((CONV_RB, CONV_CH), F32)
        for k in range(CONV_WIDTH):
            acc = acc + w_ref[k:k + 1, :] * buf[base + k:base + k + CONV_RB, :]
        y = acc + b_ref[...]
        mu = jnp.mean(y, axis=-1, keepdims=True)
        yc = y - mu
        var = jnp.mean(yc * yc, axis=-1, keepdims=True)
        z = yc * lax.rsqrt(var + EPS) * lg_ref[...] + lb_ref[...]
        o_ref[rb * CONV_RB:(rb + 1) * CONV_RB, :] = (z * _sigmoid(z)).astype(BF16)


def _conv_call(u, w, b, lg, lb):
    hb = CONV_T // CONV_HALO
    n_halo_blocks = N_TOK // CONV_HALO
    return pl.pallas_call(
        _conv_kernel,
        grid=(CONV_NB,),
        in_specs=[
            pl.BlockSpec((CONV_HALO, CONV_CH), lambda i: (jnp.maximum(i * hb - 1, 0), 0)),
            pl.BlockSpec((CONV_T, CONV_CH), lambda i: (i, 0)),
            pl.BlockSpec((CONV_HALO, CONV_CH), lambda i: (jnp.minimum((i + 1) * hb, n_halo_blocks - 1), 0)),
            _full_spec((CONV_WIDTH, CONV_CH)), _full_spec((1, CONV_CH)),
            _full_spec((1, CONV_CH)), _full_spec((1, CONV_CH)),
        ],
        out_specs=pl.BlockSpec((CONV_T, CONV_CH), lambda i: (i, 0)),
        out_shape=jax.ShapeDtypeStruct((N_TOK, CONV_CH), BF16),
        scratch_shapes=[pltpu.VMEM((CONV_T + 2 * CONV_HALO, CONV_CH), F32)],
        compiler_params=_params("parallel"),
        name="conformer_conv",
    )(u, u, u, w, b, lg, lb)


def _rope_tile(x, c_ref, s1_ref, s2_ref, half):
    up = pltpu.roll(x, LANES - half, 1)
    dn = pltpu.roll(x, half, 1)
    return x * c_ref[...] + up * s1_ref[...] + dn * s2_ref[...]


def _rope_tables(rot_dim, lane_starts, n_tok):
    rows = n_tok // GRID_W
    row_ids = jnp.repeat(jnp.arange(rows, dtype=F32), GRID_W)
    col_ids = jnp.tile(jnp.arange(GRID_W, dtype=F32), rows)
    quarter = rot_dim // 4
    half = rot_dim // 2
    freqs = jnp.power(ROPE_BASE, -jnp.arange(quarter, dtype=F32) / quarter)
    ang = jnp.concatenate([row_ids[:, None] * freqs, col_ids[:, None] * freqs], axis=-1)
    cos, sin = jnp.cos(ang), jnp.sin(ang)
    c = jnp.ones((n_tok, LANES), F32)
    s1 = jnp.zeros((n_tok, LANES), F32)
    s2 = jnp.zeros((n_tok, LANES), F32)
    for st in lane_starts:
        c = c.at[:, st:st + half].set(cos).at[:, st + half:st + rot_dim].set(cos)
        s1 = s1.at[:, st:st + half].set(-sin)
        s2 = s2.at[:, st + half:st + rot_dim].set(sin)
    ident = (jnp.ones((TM, LANES), F32), jnp.zeros((TM, LANES), F32), jnp.zeros((TM, LANES), F32))
    return tuple(jnp.concatenate([t, e], axis=0) for t, e in zip((c, s1, s2), ident))


def _rope_block(i):
    return jnp.where(i < NB_CTX, BLK_PER_LAT, (i - NB_CTX) % BLK_PER_LAT)


def _mla_prep_kernel(qn_ref, ckv_ref, kr_ref, wq_ref, wk_ref, wv_ref, qg_ref, kg_ref, c_ref, s1_ref, s2_ref,
                     q_ref, k_ref, v_ref):
    qf = _dot(qn_ref[...], wq_ref[...])
    ckv = ckv_ref[...].astype(BF16)
    kf = _dot(ckv, wk_ref[...])
    v_ref[...] = _dot(ckv, wv_ref[...]).astype(BF16)
    kr = kr_ref[...]
    half = MLA_ROPE // 2
    for h in range(MLA_HEADS):
        sl = slice(h * LANES, (h + 1) * LANES)
        qh = qf[:, sl]
        qh = qh * lax.rsqrt(jnp.sum(qh * qh, axis=-1, keepdims=True) * (1.0 / MLA_QK) + EPS) * qg_ref[...]
        q_ref[:, sl] = _rope_tile(qh, c_ref, s1_ref, s2_ref, half).astype(BF16)
        kh = kf[:, sl] + kr
        kh = kh * lax.rsqrt(jnp.sum(kh * kh, axis=-1, keepdims=True) * (1.0 / MLA_QK) + EPS) * kg_ref[...]
        k_ref[:, sl] = _rope_tile(kh, c_ref, s1_ref, s2_ref, half).astype(BF16)


def _mla_prep_call(qn, ckv, kr, wq, wk, wv, qg, kg, tables, rope_block_fn):
    n = qn.shape[0]
    hw = MLA_HEADS * LANES
    tspec = pl.BlockSpec((TM, LANES), lambda i: (rope_block_fn(i), 0))
    return pl.pallas_call(
        _mla_prep_kernel,
        grid=(n // TM,),
        in_specs=[
            _row_spec(MLA_Q_RANK), _row_spec(MLA_KV_RANK), _row_spec(LANES),
            _full_spec((MLA_Q_RANK, hw)), _full_spec((MLA_KV_RANK, hw)), _full_spec((MLA_KV_RANK, MLA_HEADS * MLA_V)),
            _full_spec((1, LANES)), _full_spec((1, LANES)), tspec, tspec, tspec,
        ],
        out_specs=[_row_spec(hw), _row_spec(hw), _row_spec(MLA_HEADS * MLA_V)],
        out_shape=[
            jax.ShapeDtypeStruct((n, hw), BF16),
            jax.ShapeDtypeStruct((n, hw), BF16),
            jax.ShapeDtypeStruct((n, MLA_HEADS * MLA_V), BF16),
        ],
        compiler_params=_params("parallel"),
        name="mla_prep",
    )(qn, ckv, kr, wq, wk, wv, qg, kg, *tables)


def _mla_attn_kernel(has_cache, *refs):
    if has_cache:
        q_ref, k_ref, v_ref, kc_ref, vc_ref, o_ref = refs
    else:
        q_ref, k_ref, v_ref, o_ref = refs
    outs = []
    for hh in range(2):
        sl = slice(hh * LANES, (hh + 1) * LANES)
        q = q_ref[:, sl]
        s1 = _dot_nt(q, k_ref[:, sl])
        m = jnp.max(s1, axis=-1, keepdims=True)
        if has_cache:
            s2 = _dot_nt(q, kc_ref[:, sl])
            m = jnp.maximum(m, jnp.max(s2, axis=-1, keepdims=True))
        p1 = jnp.exp(s1 - m)
        l = jnp.sum(p1, axis=-1, keepdims=True)
        o = _dot(p1.astype(BF16), v_ref[...])
        if has_cache:
            p2 = jnp.exp(s2 - m)
            l = l + jnp.sum(p2, axis=-1, keepdims=True)
            o = o + _dot(p2.astype(BF16), vc_ref[...])
        outs.append(o / l)
    lane = lax.broadcasted_iota(jnp.int32, outs[0].shape, 1)
    o_ref[...] = jnp.where(lane < MLA_V, outs[0], outs[1]).astype(BF16)


def _mla_attn_ctx_call(q, k, v):
    return pl.pallas_call(
        functools.partial(_mla_attn_kernel, False),
        grid=(BATCH, MLA_HEADS // 2),
        in_specs=[
            pl.BlockSpec((SEQ, 2 * LANES), lambda b, hp: (b, hp)),
            pl.BlockSpec((SEQ, 2 * LANES), lambda b, hp: (b, hp)),
            pl.BlockSpec((SEQ, LANES), lambda b, hp: (b, hp)),
        ],
        out_specs=pl.BlockSpec((SEQ, LANES), lambda b, hp: (b, hp)),
        out_shape=jax.ShapeDtypeStruct((N_CTX, MLA_HEADS * MLA_V), BF16),
        compiler_params=_params("parallel", "parallel"),
        name="mla_attn_ctx",
    )(q, k, v)


def _mla_attn_lat_call(q, k, v, kc, vc):
    nq = DEC_SEQ // TQ
    q0 = N_CTX // TQ
    k0 = N_CTX // DEC_SEQ
    return pl.pallas_call(
        functools.partial(_mla_attn_kernel, True),
        grid=(DEC_BATCH, MLA_HEADS // 2, nq),
        in_specs=[
            pl.BlockSpec((TQ, 2 * LANES), lambda b, hp, qi: (q0 + b * nq + qi, hp)),
            pl.BlockSpec((DEC_SEQ, 2 * LANES), lambda b, hp, qi: (k0 + b, hp)),
            pl.BlockSpec((DEC_SEQ, LANES), lambda b, hp, qi: (k0 + b, hp)),
            pl.BlockSpec((PAST_LEN, 2 * LANES), lambda b, hp, qi: (b, hp)),
            pl.BlockSpec((PAST_LEN, LANES), lambda b, hp, qi: (b, hp)),
        ],
        out_specs=pl.BlockSpec((TQ, LANES), lambda b, hp, qi: (b * nq + qi, hp)),
        out_shape=jax.ShapeDtypeStruct((N_LAT, MLA_HEADS * MLA_V), BF16),
        compiler_params=_params("parallel", "parallel", "parallel"),
        name="mla_attn_lat",
    )(q, k, v, kc, vc)


def _route(logits):
    lane = lax.broadcasted_iota(jnp.int32, logits.shape, 1)
    lane_f = lane.astype(F32)
    big = float(LANES)
    neg = -jnp.inf
    gmask = jnp.logical_and(lane >= N_EXPERTS, lane < N_EXPERTS + N_GROUPS)
    gl = jnp.where(gmask, logits, neg)
    ge = jnp.exp(gl - jnp.max(gl, axis=-1, keepdims=True))
    gp = ge / jnp.sum(ge, axis=-1, keepdims=True)
    gate = jnp.max(gp, axis=-1, keepdims=True)
    gidx = jnp.min(jnp.where(jnp.logical_and(gmask, gp == gate), lane_f, big), axis=-1, keepdims=True) - N_EXPERTS
    lo = gidx * EXPERTS_PER_GROUP
    emask = jnp.logical_and(lane_f >= lo, lane_f < lo + EXPERTS_PER_GROUP)
    el = jnp.where(emask, logits, neg)
    ee = jnp.exp(el - jnp.max(el, axis=-1, keepdims=True))
    ep = jnp.where(emask, ee / jnp.sum(ee, axis=-1, keepdims=True), -1.0)
    p1 = jnp.max(ep, axis=-1, keepdims=True)
    i1 = jnp.min(jnp.where(ep == p1, lane_f, big), axis=-1, keepdims=True)
    ep2 = jnp.where(lane_f == i1, -1.0, ep)
    p2 = jnp.max(ep2, axis=-1, keepdims=True)
    i2 = jnp.min(jnp.where(ep2 == p2, lane_f, big), axis=-1, keepdims=True)
    den = p1 + p2
    return jnp.where(lane_f == i1, gate * p1 / den, 0.0) + jnp.where(lane_f == i2, gate * p2 / den, 0.0)


def _outproj_kernel(n_in, *refs):
    a_refs = refs[:n_in]
    w_refs = refs[n_in:2 * n_in]
    x_ref, g1_ref, n2_ref, sh2_ref, sc2_ref, rh_ref, rl_ref, rb_ref, xo_ref, h_ref, cw_ref = refs[2 * n_in:]
    out = _dot(a_refs[0][...], w_refs[0][...])
    for a_ref, w_ref in zip(a_refs[1:], w_refs[1:]):
        out = out + _dot(a_ref[...], w_ref[...])
    x = x_ref[...] + g1_ref[0] * out
    xo_ref[...] = x
    h = _rms(x, n2_ref[...]) * (1.0 + sc2_ref[0]) + sh2_ref[0]
    h_hi = h.astype(BF16)
    h_ref[...] = h_hi
    h_lo = (h - h_hi.astype(F32)).astype(BF16)
    logits = _dot(h_hi, rh_ref[...]) + (_dot(h_hi, rl_ref[...]) + _dot(h_lo, rh_ref[...])) + rb_ref[...]
    cw_ref[...] = _route(logits)


def _outproj_call(layer, acts, ws, x, mod, n2, r_hi, r_lo, r_b):
    n_in = len(acts)
    return pl.pallas_call(
        functools.partial(_outproj_kernel, n_in),
        grid=(NB_TOK,),
        in_specs=(
            [_row_spec(a.shape[1]) for a in acts] + [_full_spec(w.shape) for w in ws]
            + [_row_spec(D_MODEL), _mod_spec(layer, 2), _full_spec((1, D_MODEL)), _mod_spec(layer, 3),
               _mod_spec(layer, 4), _full_spec((D_MODEL, LANES)), _full_spec((D_MODEL, LANES)), _full_spec((1, LANES))]
        ),
        out_specs=[_row_spec(D_MODEL), _row_spec(D_MODEL), _row_spec(LANES)],
        out_shape=[
            jax.ShapeDtypeStruct((N_TOK, D_MODEL), F32),
            jax.ShapeDtypeStruct((N_TOK, D_MODEL), BF16),
            jax.ShapeDtypeStruct((N_TOK, LANES), F32),
        ],
        compiler_params=_params("parallel"),
        name=f"outproj{layer}",
    )(*acts, *ws, x, mod, n2, mod, mod, r_hi, r_lo, r_b)


def _moe_kernel(h_ref, cw_ref, wg_ref, wu_ref, wd_ref, x_ref, g2_ref, o_ref, acc_ref):
    e = pl.program_id(1)

    @pl.when(e == 0)
    def _():
        acc_ref[...] = jnp.zeros_like(acc_ref)

    h = h_ref[...]
    a = _dot(h, wg_ref[0].astype(BF16))
    b = _dot(h, wu_ref[0].astype(BF16))
    cw = cw_ref[...]
    lane = lax.broadcasted_iota(jnp.int32, cw.shape, 1)
    col = jnp.sum(jnp.where(lane == e, cw, 0.0), axis=-1, keepdims=True)
    act = (a * _sigmoid(a) * b * col).astype(BF16)
    acc_ref[...] += _dot(act, wd_ref[0].astype(BF16))

    @pl.when(e == N_EXPERTS - 1)
    def _():
        o_ref[...] = x_ref[...] + g2_ref[0] * acc_ref[...]


def _moe_call(layer, h, cw, w_gate, w_up, w_down, x, mod):
    blk_per_lat = DEC_SEQ // TM_MOE
    nb_ctx = N_CTX // TM_MOE

    def g2_map(i, e):
        seq = jnp.where(i < nb_ctx, 0, 1 + (i - nb_ctx) // blk_per_lat)
        return ((layer * MOD_ROWS + seq) * 6 + 5, 0, 0)

    return pl.pallas_call(
        _moe_kernel,
        grid=(N_TOK // TM_MOE, N_EXPERTS),
        in_specs=[
            pl.BlockSpec((TM_MOE, D_MODEL), lambda i, e: (i, 0)),
            pl.BlockSpec((TM_MOE, LANES), lambda i, e: (i, 0)),
            pl.BlockSpec((1, D_MODEL, EXPERT_FF), lambda i, e: (e, 0, 0)),
            pl.BlockSpec((1, D_MODEL, EXPERT_FF), lambda i, e: (e, 0, 0)),
            pl.BlockSpec((1, EXPERT_FF, D_MODEL), lambda i, e: (e, 0, 0)),
            pl.BlockSpec((TM_MOE, D_MODEL), lambda i, e: (i, 0)),
            pl.BlockSpec((1, 1, D_MODEL), g2_map),
        ],
        out_specs=pl.BlockSpec((TM_MOE, D_MODEL), lambda i, e: (i, 0)),
        out_shape=jax.ShapeDtypeStruct((N_TOK, D_MODEL), F32),
        scratch_shapes=[pltpu.VMEM((TM_MOE, D_MODEL), F32)],
        compiler_params=_params("parallel", "arbitrary"),
        name=f"moe{layer}",
    )(h, cw, w_gate, w_up, w_down, x, mod)


def _group_rms(x, g_ref, n):
    lane = lax.broadcasted_iota(jnp.int32, x.shape, 1)
    lo = lane < n
    xx = x * x
    ss_lo = jnp.sum(jnp.where(lo, xx, 0.0), axis=-1, keepdims=True)
    ss_hi = jnp.sum(jnp.where(lo, 0.0, xx), axis=-1, keepdims=True)
    ss = jnp.where(lo, ss_lo, ss_hi)
    return x * lax.rsqrt(ss * (1.0 / n) + EPS) * g_ref[...]


def _inproj1_kernel(x_ref, g_ref, sh_ref, sc_ref, w_ref, qg_ref, kg_ref, c_ref, s1_ref, s2_ref,
                    q_ref, k_ref, v_ref, kn_ref, vf_ref):
    i = pl.program_id(0)
    h = _rms(x_ref[...], g_ref[...]) * (1.0 + sc_ref[0]) + sh_ref[0]
    p = _dot(h.astype(BF16), w_ref[...])
    hw = DIFF_HEADS * DIFF_DV
    v = p[:, 2 * hw:]
    v_ref[...] = v.astype(BF16)
    half = DIFF_DK // 2
    for t in range(DIFF_HEADS):
        sl = slice(t * LANES, (t + 1) * LANES)
        qt = _group_rms(p[:, sl], qg_ref, DIFF_DK)
        q_ref[:, sl] = _rope_tile(qt, c_ref, s1_ref, s2_ref, half).astype(BF16)
        kt = _group_rms(p[:, hw + t * LANES:hw + (t + 1) * LANES], kg_ref, DIFF_DK)
        k_ref[:, sl] = _rope_tile(kt, c_ref, s1_ref, s2_ref, half).astype(BF16)

        @pl.when(i < NB_CTX)
        def _():
            kn_ref[:, sl] = kt

    @pl.when(i < NB_CTX)
    def _():
        vf_ref[...] = v


def _inproj1_call(x, mod, g1, w_qkv, qg, kg, tables):
    hw = DIFF_HEADS * DIFF_DV
    tspec = pl.BlockSpec((TM, LANES), lambda i: (_rope_block(i), 0))
    ctx_spec = pl.BlockSpec((TM, hw), lambda i: (jnp.minimum(i, NB_CTX - 1), 0))
    return pl.pallas_call(
        _inproj1_kernel,
        grid=(NB_TOK,),
        in_specs=[
            _row_spec(D_MODEL), _full_spec((1, D_MODEL)), _mod_spec(1, 0), _mod_spec(1, 1),
            _full_spec((D_MODEL, 3 * hw)), _full_spec((1, LANES)), _full_spec((1, LANES)), tspec, tspec, tspec,
        ],
        out_specs=[_row_spec(hw), _row_spec(hw), _row_spec(hw), ctx_spec, ctx_spec],
        out_shape=[
            jax.ShapeDtypeStruct((N_TOK, hw), BF16),
            jax.ShapeDtypeStruct((N_TOK, hw), BF16),
            jax.ShapeDtypeStruct((N_TOK, hw), BF16),
            jax.ShapeDtypeStruct((N_CTX, hw), F32),
            jax.ShapeDtypeStruct((N_CTX, hw), F32),
        ],
        compiler_params=_params("arbitrary"),
        name="inproj1",
    )(x, g1, mod, mod, w_qkv, qg, kg, *tables)


def _softmax_parts(s_list):
    m = s_list[0].max(axis=-1, keepdims=True)
    for s in s_list[1:]:
        m = jnp.maximum(m, s.max(axis=-1, keepdims=True))
    ps = [jnp.exp(s - m) for s in s_list]
    l = ps[0].sum(axis=-1, keepdims=True)
    for p in ps[1:]:
        l = l + p.sum(axis=-1, keepdims=True)
    return ps, l


def _diff_attn_kernel(has_cache, lambda_init, *refs):
    if has_cache:
        q_ref, k_ref, v_ref, kc_ref, vc_ref, lq1, lk1, lq2, lk2, sg_ref, o_ref = refs
    else:
        q_ref, k_ref, v_ref, lq1, lk1, lq2, lk2, sg_ref, o_ref = refs
    lam = (jnp.exp(jnp.sum(lq1[...] * lk1[...], axis=-1, keepdims=True))
           - jnp.exp(jnp.sum(lq2[...] * lk2[...], axis=-1, keepdims=True)) + lambda_init)
    q = q_ref[...]
    lane = lax.broadcasted_iota(jnp.int32, q.shape, 1)
    zero = jnp.zeros_like(q)
    q0 = jnp.where(lane < DIFF_DK, q, zero)
    q1 = jnp.where(lane < DIFF_DK, zero, q)
    krefs = [k_ref, kc_ref] if has_cache else [k_ref]
    vrefs = [v_ref, vc_ref] if has_cache else [v_ref]
    p0, l0 = _softmax_parts([_dot_nt(q0, kr[...].astype(BF16)) for kr in krefs])
    p1, l1 = _softmax_parts([_dot_nt(q1, kr[...].astype(BF16)) for kr in krefs])
    r0 = 1.0 / l0
    r1 = lam / l1
    o = None
    for a, b, vr in zip(p0, p1, vrefs):
        t = _dot((a * r0 - b * r1).astype(BF16), vr[...].astype(BF16))
        o = t if o is None else o + t
    o = o * lax.rsqrt(jnp.mean(o * o, axis=-1, keepdims=True) + EPS) * sg_ref[...] * (1.0 - lambda_init)
    o_ref[...] = o.astype(BF16)


def _diff_attn_ctx_call(lambda_init, q, k, v, lams, sg):
    lspec = _full_spec((1, DIFF_DK))
    return pl.pallas_call(
        functools.partial(_diff_attn_kernel, False, lambda_init),
        grid=(BATCH, DIFF_HEADS),
        in_specs=[pl.BlockSpec((SEQ, LANES), lambda b, h: (b, h))] * 3 + [lspec] * 4 + [_full_spec((1, DIFF_DV))],
        out_specs=pl.BlockSpec((SEQ, LANES), lambda b, h: (b, h)),
        out_shape=jax.ShapeDtypeStruct((N_CTX, DIFF_HEADS * DIFF_DV), BF16),
        compiler_params=_params("parallel", "parallel"),
        name="diff_attn_ctx",
    )(q, k, v, *lams, sg)


def _diff_attn_lat_call(lambda_init, q, k, v, kc, vc, lams, sg):
    nq = DEC_SEQ // TQ
    q0 = N_CTX // TQ
    k0 = N_CTX // DEC_SEQ
    lspec = _full_spec((1, DIFF_DK))
    kv_spec = pl.BlockSpec((DEC_SEQ, LANES), lambda b, h, qi: (k0 + b, h))
    c_spec = pl.BlockSpec((PAST_LEN, LANES), lambda b, h, qi: (b, h))
    return pl.pallas_call(
        functools.partial(_diff_attn_kernel, True, lambda_init),
        grid=(DEC_BATCH, DIFF_HEADS, nq),
        in_specs=[pl.BlockSpec((TQ, LANES), lambda b, h, qi: (q0 + b * nq + qi, h)), kv_spec, kv_spec, c_spec, c_spec]
        + [lspec] * 4 + [_full_spec((1, DIFF_DV))],
        out_specs=pl.BlockSpec((TQ, LANES), lambda b, h, qi: (b * nq + qi, h)),
        out_shape=jax.ShapeDtypeStruct((N_LAT, DIFF_HEADS * DIFF_DV), BF16),
        compiler_params=_params("parallel", "parallel", "parallel"),
        name="diff_attn_lat",
    )(q, k, v, kc, vc, *lams, sg)


def _router_weights(we, wg, be, bg):
    w = jnp.zeros((D_MODEL, LANES), F32).at[:, :N_EXPERTS].set(we).at[:, N_EXPERTS:N_EXPERTS + N_GROUPS].set(wg)
    b = jnp.zeros((1, LANES), F32).at[0, :N_EXPERTS].set(be).at[0, N_EXPERTS:N_EXPERTS + N_GROUPS].set(bg)
    hi = w.astype(BF16)
    lo = (w - hi.astype(F32)).astype(BF16)
    return hi, lo, b


def kernel(x_prompt, x_sample, cache_mla_ckv, cache_mla_krope, cache_diff_k, cache_diff_v, c, c_ctx, ada_w, ada_b, norm1_g, norm2_g, ev_w_in, ev_conv_w, ev_conv_b, ev_conv_ln_g, ev_conv_ln_b, ev_q_norm_g, ev_w_qb, ev_kv_norm_g, ev_w_kvb, ev_qn_g, ev_kn_g, ev_w_o, od_w_qkv, od_qn_g, od_kn_g, od_lambda_q1, od_lambda_k1, od_lambda_q2, od_lambda_k2, od_subln_g, od_w_o, moe_wg, moe_bg, moe_we, moe_be, moe_w_gate, moe_w_up, moe_w_down):
    x = jnp.concatenate([x_prompt.reshape(N_CTX, D_MODEL), x_sample.reshape(N_LAT, D_MODEL)], axis=0)
    cond = jnp.concatenate([c_ctx[None, :], c, jnp.zeros((MOD_ROWS - 1 - DEC_BATCH, D_MODEL), F32)], axis=0)
    mod = _mod_call(cond, ada_w, ada_b).reshape(-1, 1, D_MODEL)

    o3 = 2 * CONV_CH + MLA_Q_RANK + MLA_KV_RANK
    w_in = jnp.zeros((D_MODEL, EV_W), F32).at[:, :o3].set(ev_w_in[0][:, :o3])
    w_in = w_in.at[:, o3 + MLA_NOPE:o3 + MLA_QK].set(ev_w_in[0][:, o3:]).astype(BF16)
    u, qn, ckv, kr = _inproj0_call(x, mod, norm1_g[0][None], w_in, ev_q_norm_g[0][None], ev_kv_norm_g[0][None])
    conv = _conv_call(u, ev_conv_w[0], ev_conv_b[0][None], ev_conv_ln_g[0][None], ev_conv_ln_b[0][None])

    pad_qk = ((0, 0), (0, 0), (0, LANES - MLA_QK))
    wq = jnp.pad(ev_w_qb[0].reshape(MLA_Q_RANK, MLA_HEADS, MLA_QK), pad_qk).reshape(MLA_Q_RANK, -1).astype(BF16)
    wkv = ev_w_kvb[0].reshape(MLA_KV_RANK, MLA_HEADS, MLA_NOPE + MLA_V)
    wk = jnp.pad(wkv[..., :MLA_NOPE], ((0, 0), (0, 0), (0, LANES - MLA_NOPE))).reshape(MLA_KV_RANK, -1).astype(BF16)
    wv = wkv[..., MLA_NOPE:].reshape(MLA_KV_RANK, -1).astype(BF16)
    qg = jnp.pad(ev_qn_g[0] * (MLA_QK ** -0.5), (0, LANES - MLA_QK))[None]
    kg = jnp.pad(ev_kn_g[0], (0, LANES - MLA_QK))[None]
    tables = _rope_tables(MLA_ROPE, (MLA_NOPE,), DEC_SEQ)
    q0, k0, v0 = _mla_prep_call(qn, ckv, kr, wq, wk, wv, qg, kg, tables, _rope_block)
    kr_cache = jnp.pad(cache_mla_krope[:, 0].reshape(N_CACHE, MLA_ROPE), ((0, 0), (MLA_NOPE, LANES - MLA_QK)))
    _, kc0, vc0 = _mla_prep_call(jnp.zeros((N_CACHE, MLA_Q_RANK), BF16), cache_mla_ckv[:, 0].reshape(N_CACHE, MLA_KV_RANK),
                                 kr_cache, wq, wk, wv, qg, kg, tables, lambda i: BLK_PER_LAT)
    attn = jnp.concatenate([_mla_attn_ctx_call(q0, k0, v0), _mla_attn_lat_call(q0, k0, v0, kc0, vc0)], axis=0)

    w_o = ev_w_o[0].astype(BF16)
    r_hi, r_lo, r_b = _router_weights(moe_we[0], moe_wg[0], moe_be[0], moe_bg[0])
    x, h2, cw = _outproj_call(0, [conv, attn], [w_o[:CONV_CH], w_o[CONV_CH:]], x, mod, norm2_g[0][None], r_hi, r_lo, r_b)
    x = _moe_call(0, h2, cw, moe_w_gate[0], moe_w_up[0], moe_w_down[0], x, mod)

    lambda_init = 0.8 - 0.6 * math.exp(-0.3 * 1)
    qg1 = jnp.tile(od_qn_g[0] * (DIFF_DK ** -0.5), 2)[None]
    kg1 = jnp.tile(od_kn_g[0], 2)[None]
    tables1 = _rope_tables(DIFF_DK, (0, DIFF_DK), DEC_SEQ)
    q1, k1, v1, kn1, vf1 = _inproj1_call(x, mod, norm1_g[1][None], od_w_qkv[0].astype(BF16), qg1, kg1, tables1)
    lams = [od_lambda_q1[0][None], od_lambda_k1[0][None], od_lambda_q2[0][None], od_lambda_k2[0][None]]
    sg = od_subln_g[0][None]
    hw = DIFF_HEADS * DIFF_DV
    kc1 = cache_diff_k[:, 0].reshape(N_CACHE, hw)
    vc1 = cache_diff_v[:, 0].reshape(N_CACHE, hw)
    attn1 = jnp.concatenate([_diff_attn_ctx_call(lambda_init, q1, k1, v1, lams, sg),
                             _diff_attn_lat_call(lambda_init, q1, k1, v1, kc1, vc1, lams, sg)], axis=0)
    r_hi, r_lo, r_b = _router_weights(moe_we[1], moe_wg[1], moe_be[1], moe_bg[1])
    x, h2, cw = _outproj_call(1, [attn1], [od_w_o[0].astype(BF16)], x, mod, norm2_g[1][None], r_hi, r_lo, r_b)
    x = _moe_call(1, h2, cw, moe_w_gate[1], moe_w_up[1], moe_w_down[1], x, mod)

    y_prompt = x[:N_CTX].reshape(BATCH, SEQ, D_MODEL)
    y_sample = x[N_CTX:].reshape(DEC_BATCH, DEC_SEQ, D_MODEL)
    new_mla_ckv = ckv[:N_CTX].reshape(BATCH, 1, SEQ, MLA_KV_RANK)
    new_mla_krope = kr[:N_CTX, MLA_NOPE:MLA_QK].reshape(BATCH, 1, SEQ, MLA_ROPE)
    new_diff_k = kn1.reshape(BATCH, 1, SEQ, DIFF_HEADS, 2, DIFF_DK)
    new_diff_v = vf1.reshape(BATCH, 1, SEQ, DIFF_HEADS, DIFF_DV)
    return (y_prompt, y_sample, new_mla_ckv, new_mla_krope, new_diff_k, new_diff_v)
```

```python
import functools
import math

import jax
import jax.numpy as jnp
from jax import lax
from jax.experimental import pallas as pl
from jax.experimental.pallas import tpu as pltpu

F32 = jnp.float32
BF16 = jnp.bfloat16

D_MODEL = 1024
BATCH = 16
SEQ = 256
DEC_BATCH = 8
DEC_SEQ = 2048
PAST_LEN = 256
GRID_W = 64
ROPE_BASE = 10000.0
EPS = 1e-6

CONV_CH = 512
CONV_WIDTH = 31
CONV_PAD = 15
MLA_HEADS = 8
MLA_NOPE = 64
MLA_ROPE = 32
MLA_QK = 96
MLA_V = 64
MLA_Q_RANK = 384
MLA_KV_RANK = 256
DIFF_HEADS = 8
DIFF_DK = 64
DIFF_DV = 128
N_GROUPS = 4
EXPERTS_PER_GROUP = 8
N_EXPERTS = 32
EXPERT_FF = 256

LANES = 128
N_CTX = BATCH * SEQ
N_LAT = DEC_BATCH * DEC_SEQ
N_TOK = N_CTX + N_LAT
N_CACHE = DEC_BATCH * PAST_LEN
MOD_ROWS = 16
VMEM_LIMIT = 48 * 1024 * 1024

TM = 512
NB_CTX = N_CTX // TM
NB_TOK = N_TOK // TM
BLK_PER_LAT = DEC_SEQ // TM
TQ = 256
CONV_T = 256
CONV_HALO = 16
CONV_RB = 32


def _params(*sem):
    return pltpu.CompilerParams(dimension_semantics=sem, vmem_limit_bytes=VMEM_LIMIT)


def _sigmoid(x):
    return 1.0 / (1.0 + jnp.exp(-x))


def _rms(x, g):
    return x * lax.rsqrt(jnp.mean(x * x, axis=-1, keepdims=True) + EPS) * g


def _dot(a, b):
    return jnp.dot(a, b, preferred_element_type=F32)


def _dot_nt(a, b):
    return lax.dot_general(a, b, (((1,), (1,)), ((), ())), preferred_element_type=F32)


def _seq_of_block(i):
    return jnp.where(i < NB_CTX, 0, 1 + (i - NB_CTX) // BLK_PER_LAT)


def _mod_spec(layer, slot):
    return pl.BlockSpec((1, 1, D_MODEL), lambda i, *_: ((layer * MOD_ROWS + _seq_of_block(i)) * 6 + slot, 0, 0))


def _row_spec(width, tm=TM):
    return pl.BlockSpec((tm, width), lambda i, *_: (i, 0))


def _full_spec(shape):
    nd = len(shape)
    return pl.BlockSpec(shape, lambda *_: (0,) * nd)


def _mod_kernel(c_ref, w_ref, b_ref, o_ref):
    c = c_ref[...]
    s = (c * _sigmoid(c)).astype(BF16)
    o_ref[0] = _dot(s, w_ref[0].astype(BF16)) + b_ref[0]


def _mod_call(cond, ada_w, ada_b):
    depth = ada_w.shape[0]
    tn = 1024
    return pl.pallas_call(
        _mod_kernel,
        grid=(depth, 6 * D_MODEL // tn),
        in_specs=[
            pl.BlockSpec((MOD_ROWS, D_MODEL), lambda l, j: (0, 0)),
            pl.BlockSpec((1, D_MODEL, tn), lambda l, j: (l, 0, j)),
            pl.BlockSpec((1, 1, tn), lambda l, j: (l, 0, j)),
        ],
        out_specs=pl.BlockSpec((1, MOD_ROWS, tn), lambda l, j: (l, 0, j)),
        out_shape=jax.ShapeDtypeStruct((depth, MOD_ROWS, 6 * D_MODEL), F32),
        compiler_params=_params("parallel", "parallel"),
        name="adaln_mod",
    )(cond, ada_w, ada_b.reshape(depth, 1, 6 * D_MODEL))


EV_W = 1792


def _inproj0_kernel(x_ref, g_ref, sh_ref, sc_ref, w_ref, qg_ref, kvg_ref, u_ref, qn_ref, ckv_ref, kr_ref):
    h = _rms(x_ref[...], g_ref[...]) * (1.0 + sc_ref[0]) + sh_ref[0]
    p = _dot(h.astype(BF16), w_ref[...])
    u_ref[...] = p[:, :CONV_CH] * _sigmoid(p[:, CONV_CH:2 * CONV_CH])
    o1 = 2 * CONV_CH
    o2 = o1 + MLA_Q_RANK
    o3 = o2 + MLA_KV_RANK
    qn_ref[...] = _rms(p[:, o1:o2], qg_ref[...]).astype(BF16)
    ckv_ref[...] = _rms(p[:, o2:o3], kvg_ref[...])
    kr_ref[...] = p[:, o3:o3 + LANES]


def _inproj0_call(x, mod, g1, w_in, qg, kvg):
    return pl.pallas_call(
        _inproj0_kernel,
        grid=(NB_TOK,),
        in_specs=[
            _row_spec(D_MODEL), _full_spec((1, D_MODEL)), _mod_spec(0, 0), _mod_spec(0, 1),
            _full_spec((D_MODEL, EV_W)), _full_spec((1, MLA_Q_RANK)), _full_spec((1, MLA_KV_RANK)),
        ],
        out_specs=[_row_spec(CONV_CH), _row_spec(MLA_Q_RANK), _row_spec(MLA_KV_RANK), _row_spec(LANES)],
        out_shape=[
            jax.ShapeDtypeStruct((N_TOK, CONV_CH), F32),
            jax.ShapeDtypeStruct((N_TOK, MLA_Q_RANK), BF16),
            jax.ShapeDtypeStruct((N_TOK, MLA_KV_RANK), F32),
            jax.ShapeDtypeStruct((N_TOK, LANES), F32),
        ],
        compiler_params=_params("parallel"),
        name="inproj0",
    )(x, g1, mod, mod, w_in, qg, kvg)


CONV_CHUNKS_PER_LAT = DEC_SEQ // CONV_T
CONV_NB_CTX = N_CTX // CONV_T
CONV_NB = N_TOK // CONV_T


def _conv_kernel(prev_ref, cur_ref, nxt_ref, w_ref, b_ref, lg_ref, lb_ref, o_ref, buf):
    i = pl.program_id(0)
    j = (i - CONV_NB_CTX) % CONV_CHUNKS_PER_LAT
    first = jnp.logical_or(i < CONV_NB_CTX, j == 0)
    last = jnp.logical_or(i < CONV_NB_CTX, j == CONV_CHUNKS_PER_LAT - 1)
    buf[0:CONV_HALO, :] = jnp.where(first, 0.0, prev_ref[...])
    buf[CONV_HALO:CONV_HALO + CONV_T, :] = cur_ref[...]
    buf[CONV_HALO + CONV_T:, :] = jnp.where(last, 0.0, nxt_ref[...])
    for rb in range(CONV_T // CONV_RB):
        base = rb * CONV_RB + CONV_HALO - CONV_PAD
        acc = jnp.zeros((CONV_RB, CONV_CH), F32)
        for k in range(CONV_WIDTH):
            acc = acc + w_ref[k:k + 1, :] * buf[base + k:base + k + CONV_RB, :]
        y = acc + b_ref[...]
        mu = jnp.mean(y, axis=-1, keepdims=True)
        yc = y - mu
        var = jnp.mean(yc * yc, axis=-1, keepdims=True)
        z = yc * lax.rsqrt(var + EPS) * lg_ref[...] + lb_ref[...]
        o_ref[rb * CONV_RB:(rb + 1) * CONV_RB, :] = (z * _sigmoid(z)).astype(BF16)


def _conv_call(u, w, b, lg, lb):
    hb = CONV_T // CONV_HALO
    n_halo_blocks = N_TOK // CONV_HALO
    return pl.pallas_call(
        _conv_kernel,
        grid=(CONV_NB,),
        in_specs=[
            pl.BlockSpec((CONV_HALO, CONV_CH), lambda i: (jnp.maximum(i * hb - 1, 0), 0)),
            pl.BlockSpec((CONV_T, CONV_CH), lambda i: (i, 0)),
            pl.BlockSpec((CONV_HALO, CONV_CH), lambda i: (jnp.minimum((i + 1) * hb, n_halo_blocks - 1), 0)),
            _full_spec((CONV_WIDTH, CONV_CH)), _full_spec((1, CONV_CH)),
            _full_spec((1, CONV_CH)), _full_spec((1, CONV_CH)),
        ],
        out_specs=pl.BlockSpec((CONV_T, CONV_CH), lambda i: (i, 0)),
        out_shape=jax.ShapeDtypeStruct((N_TOK, CONV_CH), BF16),
        scratch_shapes=[pltpu.VMEM((CONV_T + 2 * CONV_HALO, CONV_CH), F32)],
        compiler_params=_params("parallel"),
        name="conformer_conv",
    )(u, u, u, w, b, lg, lb)


def _rope_tile(x, c_ref, s1_ref, s2_ref, half):
    up = pltpu.roll(x, LANES - half, 1)
    dn = pltpu.roll(x, half, 1)
    return x * c_ref[...] + up * s1_ref[...] + dn * s2_ref[...]


def _rope_tables(rot_dim, lane_starts, n_tok):
    rows = n_tok // GRID_W
    row_ids = jnp.repeat(jnp.arange(rows, dtype=F32), GRID_W)
    col_ids = jnp.tile(jnp.arange(GRID_W, dtype=F32), rows)
    quarter = rot_dim // 4
    half = rot_dim // 2
    freqs = jnp.power(ROPE_BASE, -jnp.arange(quarter, dtype=F32) / quarter)
    ang = jnp.concatenate([row_ids[:, None] * freqs, col_ids[:, None] * freqs], axis=-1)
    cos, sin = jnp.cos(ang), jnp.sin(ang)
    c = jnp.ones((n_tok, LANES), F32)
    s1 = jnp.zeros((n_tok, LANES), F32)
    s2 = jnp.zeros((n_tok, LANES), F32)
    for st in lane_starts:
        c = c.at[:, st:st + half].set(cos).at[:, st + half:st + rot_dim].set(cos)
        s1 = s1.at[:, st:st + half].set(-sin)
        s2 = s2.at[:, st + half:st + rot_dim].set(sin)
    ident = (jnp.ones((TM, LANES), F32), jnp.zeros((TM, LANES), F32), jnp.zeros((TM, LANES), F32))
    return tuple(jnp.concatenate([t, e], axis=0) for t, e in zip((c, s1, s2), ident))


def _rope_block(i):
    return jnp.where(i < NB_CTX, BLK_PER_LAT, (i - NB_CTX) % BLK_PER_LAT)


def _mla_prep_kernel(qn_ref, ckv_ref, kr_ref, wq_ref, wk_ref, wv_ref, qg_ref, kg_ref, c_ref, s1_ref, s2_ref,
                     q_ref, k_ref, v_ref):
    qf = _dot(qn_ref[...], wq_ref[...])
    ckv = ckv_ref[...].astype(BF16)
    kf = _dot(ckv, wk_ref[...])
    v_ref[...] = _dot(ckv, wv_ref[...]).astype(BF16)
    kr = kr_ref[...]
    half = MLA_ROPE // 2
    for h in range(MLA_HEADS):
        sl = slice(h * LANES, (h + 1) * LANES)
        qh = qf[:, sl]
        qh = qh * lax.rsqrt(jnp.sum(qh * qh, axis=-1, keepdims=True) * (1.0 / MLA_QK) + EPS) * qg_ref[...]
        q_ref[:, sl] = _rope_tile(qh, c_ref, s1_ref, s2_ref, half).astype(BF16)
        kh = kf[:, sl] + kr
        kh = kh * lax.rsqrt(jnp.sum(kh * kh, axis=-1, keepdims=True) * (1.0 / MLA_QK) + EPS) * kg_ref[...]
        k_ref[:, sl] = _rope_tile(kh, c_ref, s1_ref, s2_ref, half).astype(BF16)


def _mla_prep_call(qn, ckv, kr, wq, wk, wv, qg, kg, tables, rope_block_fn):
    n = qn.shape[0]
    hw = MLA_HEADS * LANES
    tspec = pl.BlockSpec((TM, LANES), lambda i: (rope_block_fn(i), 0))
    return pl.pallas_call(
        _mla_prep_kernel,
        grid=(n // TM,),
        in_specs=[
            _row_spec(MLA_Q_RANK), _row_spec(MLA_KV_RANK), _row_spec(LANES),
            _full_spec((MLA_Q_RANK, hw)), _full_spec((MLA_KV_RANK, hw)), _full_spec((MLA_KV_RANK, MLA_HEADS * MLA_V)),
            _full_spec((1, LANES)), _full_spec((1, LANES)), tspec, tspec, tspec,
        ],
        out_specs=[_row_spec(hw), _row_spec(hw), _row_spec(MLA_HEADS * MLA_V)],
        out_shape=[
            jax.ShapeDtypeStruct((n, hw), BF16),
            jax.ShapeDtypeStruct((n, hw), BF16),
            jax.ShapeDtypeStruct((n, MLA_HEADS * MLA_V), BF16),
        ],
        compiler_params=_params("parallel"),
        name="mla_prep",
    )(qn, ckv, kr, wq, wk, wv, qg, kg, *tables)


def _mla_attn_kernel(has_cache, *refs):
    if has_cache:
        q_ref, k_ref, v_ref, kc_ref, vc_ref, o_ref = refs
    else:
        q_ref, k_ref, v_ref, o_ref = refs
    outs = []
    for hh in range(2):
        sl = slice(hh * LANES, (hh + 1) * LANES)
        q = q_ref[:, sl]
        s1 = _dot_nt(q, k_ref[:, sl])
        m = jnp.max(s1, axis=-1, keepdims=True)
        if has_cache:
            s2 = _dot_nt(q, kc_ref[:, sl])
            m = jnp.maximum(m, jnp.max(s2, axis=-1, keepdims=True))
        p1 = jnp.exp(s1 - m)
        l = jnp.sum(p1, axis=-1, keepdims=True)
        o = _dot(p1.astype(BF16), v_ref[...])
        if has_cache:
            p2 = jnp.exp(s2 - m)
            l = l + jnp.sum(p2, axis=-1, keepdims=True)
            o = o + _dot(p2.astype(BF16), vc_ref[...])
        outs.append(o / l)
    lane = lax.broadcasted_iota(jnp.int32, outs[0].shape, 1)
    o_ref[...] = jnp.where(lane < MLA_V, outs[0], outs[1]).astype(BF16)


def _mla_attn_ctx_call(q, k, v):
    return pl.pallas_call(
        functools.partial(_mla_attn_kernel, False),
        grid=(BATCH, MLA_HEADS // 2),
        in_specs=[
            pl.BlockSpec((SEQ, 2 * LANES), lambda b, hp: (b, hp)),
            pl.BlockSpec((SEQ, 2 * LANES), lambda b, hp: (b, hp)),
            pl.BlockSpec((SEQ, LANES), lambda b, hp: (b, hp)),
        ],
        out_specs=pl.BlockSpec((SEQ, LANES), lambda b, hp: (b, hp)),
        out_shape=jax.ShapeDtypeStruct((N_CTX, MLA_HEADS * MLA_V), BF16),
        compiler_params=_params("parallel", "parallel"),
        name="mla_attn_ctx",
    )(q, k, v)


def _mla_attn_lat_call(q, k, v, kc, vc):
    nq = DEC_SEQ // TQ
    q0 = N_CTX // TQ
    k0 = N_CTX // DEC_SEQ
    return pl.pallas_call(
        functools.partial(_mla_attn_kernel, True),
        grid=(DEC_BATCH, MLA_HEADS // 2, nq),
        in_specs=[
            pl.BlockSpec((TQ, 2 * LANES), lambda b, hp, qi: (q0 + b * nq + qi, hp)),
            pl.BlockSpec((DEC_SEQ, 2 * LANES), lambda b, hp, qi: (k0 + b, hp)),
            pl.BlockSpec((DEC_SEQ, LANES), lambda b, hp, qi: (k0 + b, hp)),
            pl.BlockSpec((PAST_LEN, 2 * LANES), lambda b, hp, qi: (b, hp)),
            pl.BlockSpec((PAST_LEN, LANES), lambda b, hp, qi: (b, hp)),
        ],
        out_specs=pl.BlockSpec((TQ, LANES), lambda b, hp, qi: (b * nq + qi, hp)),
        out_shape=jax.ShapeDtypeStruct((N_LAT, MLA_HEADS * MLA_V), BF16),
        compiler_params=_params("parallel", "parallel", "parallel"),
        name="mla_attn_lat",
    )(q, k, v, kc, vc)


def _route(logits):
    lane = lax.broadcasted_iota(jnp.int32, logits.shape, 1)
    lane_f = lane.astype(F32)
    big = float(LANES)
    neg = -jnp.inf
    gmask = jnp.logical_and(lane >= N_EXPERTS, lane < N_EXPERTS + N_GROUPS)
    gl = jnp.where(gmask, logits, neg)
    ge = jnp.exp(gl - jnp.max(gl, axis=-1, keepdims=True))
    gp = ge / jnp.sum(ge, axis=-1, keepdims=True)
    gate = jnp.max(gp, axis=-1, keepdims=True)
    gidx = jnp.min(jnp.where(jnp.logical_and(gmask, gp == gate), lane_f, big), axis=-1, keepdims=True) - N_EXPERTS
    lo = gidx * EXPERTS_PER_GROUP
    emask = jnp.logical_and(lane_f >= lo, lane_f < lo + EXPERTS_PER_GROUP)
    el = jnp.where(emask, logits, neg)
    ee = jnp.exp(el - jnp.max(el, axis=-1, keepdims=True))
    ep = jnp.where(emask, ee / jnp.sum(ee, axis=-1, keepdims=True), -1.0)
    p1 = jnp.max(ep, axis=-1, keepdims=True)
    i1 = jnp.min(jnp.where(ep == p1, lane_f, big), axis=-1, keepdims=True)
    ep2 = jnp.where(lane_f == i1, -1.0, ep)
    p2 = jnp.max(ep2, axis=-1, keepdims=True)
    i2 = jnp.min(jnp.where(ep2 == p2, lane_f, big), axis=-1, keepdims=True)
    den = p1 + p2
    return jnp.where(lane_f == i1, gate * p1 / den, 0.0) + jnp.where(lane_f == i2, gate * p2 / den, 0.0)


def _outproj_kernel(n_in, *refs):
    a_refs = refs[:n_in]
    w_refs = refs[n_in:2 * n_in]
    x_ref, g1_ref, n2_ref, sh2_ref, sc2_ref, rh_ref, rl_ref, rb_ref, xo_ref, h_ref, cw_ref, cnt_ref = refs[2 * n_in:]
    out = _dot(a_refs[0][...], w_refs[0][...])
    for a_ref, w_ref in zip(a_refs[1:], w_refs[1:]):
        out = out + _dot(a_ref[...], w_ref[...])
    x = x_ref[...] + g1_ref[0] * out
    xo_ref[...] = x
    h = _rms(x, n2_ref[...]) * (1.0 + sc2_ref[0]) + sh2_ref[0]
    h_hi = h.astype(BF16)
    h_ref[...] = h_hi
    h_lo = (h - h_hi.astype(F32)).astype(BF16)
    logits = _dot(h_hi, rh_ref[...]) + (_dot(h_hi, rl_ref[...]) + _dot(h_lo, rh_ref[...])) + rb_ref[...]
    cw = _route(logits)
    cw_ref[...] = cw
    cnt_ref[0] = jnp.sum(jnp.where(cw > 0.0, 1.0, 0.0), axis=0, keepdims=True)


def _outproj_call(layer, acts, ws, x, mod, n2, r_hi, r_lo, r_b):
    n_in = len(acts)
    return pl.pallas_call(
        functools.partial(_outproj_kernel, n_in),
        grid=(NB_TOK,),
        in_specs=(
            [_row_spec(a.shape[1]) for a in acts] + [_full_spec(w.shape) for w in ws]
            + [_row_spec(D_MODEL), _mod_spec(layer, 2), _full_spec((1, D_MODEL)), _mod_spec(layer, 3),
               _mod_spec(layer, 4), _full_spec((D_MODEL, LANES)), _full_spec((D_MODEL, LANES)), _full_spec((1, LANES))]
        ),
        out_specs=[_row_spec(D_MODEL), _row_spec(D_MODEL), _row_spec(LANES),
                   pl.BlockSpec((1, 1, LANES), lambda i: (i, 0, 0))],
        out_shape=[
            jax.ShapeDtypeStruct((N_TOK, D_MODEL), F32),
            jax.ShapeDtypeStruct((N_TOK, D_MODEL), BF16),
            jax.ShapeDtypeStruct((N_TOK, LANES), F32),
            jax.ShapeDtypeStruct((NB_TOK, 1, LANES), F32),
        ],
        compiler_params=_params("parallel"),
        name=f"outproj{layer}",
    )(*acts, *ws, x, mod, n2, mod, mod, r_hi, r_lo, r_b)


MOE_CHUNK = 16
MOE_TILE = 256
NW = N_TOK // TM
R_LOC = 2 * TM + N_EXPERTS * MOE_CHUNK
R_SORTED_MAX = 2 * N_TOK + NW * N_EXPERTS * (MOE_CHUNK - 1) + N_EXPERTS * (MOE_TILE - 1)
NT_MAX = -(-R_SORTED_MAX // MOE_TILE)


def _moe_tables(cnt):
    cnt = cnt[:, 0, :N_EXPERTS].astype(jnp.int32)
    padc = (cnt + (MOE_CHUNK - 1)) // MOE_CHUNK * MOE_CHUNK
    loff = jnp.cumsum(padc, axis=1) - padc
    tot = jnp.sum(padc, axis=0)
    seg = (tot + (MOE_TILE - 1)) // MOE_TILE * MOE_TILE
    seg_end = jnp.cumsum(seg)
    base = seg_end - seg
    goff = base[None, :] + jnp.cumsum(padc, axis=0) - padc
    tile_start = jnp.arange(NT_MAX, dtype=jnp.int32) * MOE_TILE
    texp = jnp.minimum(jnp.searchsorted(seg_end, tile_start, side="right"), N_EXPERTS - 1).astype(jnp.int32)
    nvalid = (seg_end[-1:] // MOE_TILE).astype(jnp.int32)
    i32 = lambda a: a.reshape(-1).astype(jnp.int32)
    return dict(nchunk=i32(padc // MOE_CHUNK), loff=i32(loff), goff=i32(goff), tail_start=i32(base + tot),
                tail_chunks=i32((seg - tot) // MOE_CHUNK), texp=texp, nvalid=nvalid)


def _local_positions(cw):
    n = cw.shape[0]
    a = cw > 0.0
    a_f = jnp.where(a, 1.0, 0.0)
    below = jnp.where(lax.broadcasted_iota(jnp.int32, (n, n), 1) < lax.broadcasted_iota(jnp.int32, (n, n), 0), 1.0, 0.0)
    rank = _dot(below.astype(BF16), a_f.astype(BF16))
    cnt = jnp.sum(a_f, axis=0, keepdims=True)
    padc = jnp.floor((cnt + (MOE_CHUNK - 1)) * (1.0 / MOE_CHUNK)) * MOE_CHUNK
    before = jnp.where(lax.broadcasted_iota(jnp.int32, (LANES, LANES), 0) < lax.broadcasted_iota(jnp.int32, (LANES, LANES), 1),
                       1.0, 0.0)
    loff = _dot(jnp.broadcast_to(padc, (8, LANES)).astype(BF16), before.astype(BF16))[0:1]
    lpos = rank + loff
    pa = jnp.min(jnp.where(a, lpos, float(R_LOC)), axis=-1, keepdims=True)
    pb = jnp.max(jnp.where(a, lpos, -1.0), axis=-1, keepdims=True)
    wa = jnp.sum(jnp.where(jnp.logical_and(a, lpos == pa), cw, 0.0), axis=-1, keepdims=True)
    wb = jnp.sum(jnp.where(jnp.logical_and(a, lpos == pb), cw, 0.0), axis=-1, keepdims=True)
    return pa, pb, wa, wb


def _for_each_piece(w, nchunk_ref, loff_ref, goff_ref, fn):
    def per_expert(e, total):
        n = nchunk_ref[w * N_EXPERTS + e]
        lo = loff_ref[w * N_EXPERTS + e]
        go = goff_ref[w * N_EXPERTS + e]

        def piece(c, carry):
            fn(pl.multiple_of(lo + c * MOE_CHUNK, MOE_CHUNK), pl.multiple_of(go + c * MOE_CHUNK, MOE_CHUNK))
            return carry

        lax.fori_loop(0, n, piece, 0)
        return total + n

    return lax.fori_loop(0, N_EXPERTS, per_expert, 0)


def _dispatch_kernel(nchunk_ref, loff_ref, goff_ref, tstart_ref, tchunk_ref, h_ref, cw_ref, hs_ref, buf, zbuf, sem):
    w = pl.program_id(0)
    pa, pb, _, _ = _local_positions(cw_ref[...])
    lane = lax.broadcasted_iota(jnp.int32, (TM, LANES), 1)
    pos_t = jnp.where(lane == 0, pa, jnp.where(lane == 1, pb, 0.0)).T
    r = lax.broadcasted_iota(jnp.int32, (R_LOC, TM), 0).astype(F32)
    sel = jnp.logical_or(r == pos_t[0:1, :], r == pos_t[1:2, :])
    buf[...] = _dot(jnp.where(sel, 1.0, 0.0).astype(BF16), h_ref[...]).astype(BF16)

    def piece_copy(local_row, global_row):
        return pltpu.make_async_copy(buf.at[pl.ds(local_row, MOE_CHUNK)], hs_ref.at[pl.ds(global_row, MOE_CHUNK)], sem)

    total = _for_each_piece(w, nchunk_ref, loff_ref, goff_ref, lambda lo, go: piece_copy(lo, go).start())

    def wait_piece(c, carry):
        piece_copy(0, 0).wait()
        return carry

    lax.fori_loop(0, total, wait_piece, 0)

    @pl.when(w == NW - 1)
    def _():
        zbuf[...] = jnp.zeros_like(zbuf)

        def zero_copy(global_row):
            return pltpu.make_async_copy(zbuf, hs_ref.at[pl.ds(global_row, MOE_CHUNK)], sem)

        def per_expert(e, total):
            n = tchunk_ref[e]
            st = tstart_ref[e]

            def piece(c, carry):
                zero_copy(pl.multiple_of(st + c * MOE_CHUNK, MOE_CHUNK)).start()
                return carry

            lax.fori_loop(0, n, piece, 0)
            return total + n

        total_z = lax.fori_loop(0, N_EXPERTS, per_expert, 0)

        def wait_zero(c, carry):
            zero_copy(0).wait()
            return carry

        lax.fori_loop(0, total_z, wait_zero, 0)


def _dispatch_call(layer, h, cw, tb):
    return pl.pallas_call(
        _dispatch_kernel,
        grid_spec=pltpu.PrefetchScalarGridSpec(
            num_scalar_prefetch=5,
            grid=(NW,),
            in_specs=[_row_spec(D_MODEL), _row_spec(LANES)],
            out_specs=pl.BlockSpec(memory_space=pl.ANY),
            scratch_shapes=[pltpu.VMEM((R_LOC, D_MODEL), BF16), pltpu.VMEM((MOE_CHUNK, D_MODEL), BF16),
                            pltpu.SemaphoreType.DMA],
        ),
        out_shape=jax.ShapeDtypeStruct((NT_MAX * MOE_TILE, D_MODEL), BF16),
        compiler_params=_params("arbitrary"),
        name=f"moe_dispatch{layer}",
    )(tb["nchunk"], tb["loff"], tb["goff"], tb["tail_start"], tb["tail_chunks"], h, cw)


def _expert_kernel(texp_ref, nvalid_ref, hs_ref, wg_ref, wu_ref, wd_ref, ys_ref, wg_b, wu_b, wd_b):
    i = pl.program_id(0)
    fresh = jnp.logical_or(i == 0, texp_ref[i] != texp_ref[jnp.maximum(i - 1, 0)])

    @pl.when(fresh)
    def _():
        wg_b[...] = wg_ref[0].astype(BF16)
        wu_b[...] = wu_ref[0].astype(BF16)
        wd_b[...] = wd_ref[0].astype(BF16)

    @pl.when(i < nvalid_ref[0])
    def _():
        h = hs_ref[...]
        a = _dot(h, wg_b[...])
        b = _dot(h, wu_b[...])
        ys_ref[...] = _dot((a * _sigmoid(a) * b).astype(BF16), wd_b[...]).astype(BF16)


def _expert_call(layer, hs, tb, w_gate, w_up, w_down):
    tile_map = lambda i, texp, nv: (jnp.minimum(i, nv[0] - 1), 0)
    w_map = lambda i, texp, nv: (texp[i], 0, 0)
    return pl.pallas_call(
        _expert_kernel,
        grid_spec=pltpu.PrefetchScalarGridSpec(
            num_scalar_prefetch=2,
            grid=(NT_MAX,),
            in_specs=[
                pl.BlockSpec((MOE_TILE, D_MODEL), tile_map),
                pl.BlockSpec((1, D_MODEL, EXPERT_FF), w_map),
                pl.BlockSpec((1, D_MODEL, EXPERT_FF), w_map),
                pl.BlockSpec((1, EXPERT_FF, D_MODEL), w_map),
            ],
            out_specs=pl.BlockSpec((MOE_TILE, D_MODEL), tile_map),
            scratch_shapes=[pltpu.VMEM((D_MODEL, EXPERT_FF), BF16), pltpu.VMEM((D_MODEL, EXPERT_FF), BF16),
                            pltpu.VMEM((EXPERT_FF, D_MODEL), BF16)],
        ),
        out_shape=jax.ShapeDtypeStruct((NT_MAX * MOE_TILE, D_MODEL), BF16),
        compiler_params=_params("arbitrary"),
        name=f"moe_experts{layer}",
    )(tb["texp"], tb["nvalid"], hs, w_gate, w_up, w_down)


def _combine_kernel(nchunk_ref, loff_ref, goff_ref, cw_ref, x_ref, g2_ref, ys_ref, o_ref, buf, sem):
    w = pl.program_id(0)

    @pl.when(w == 0)
    def _():
        buf[...] = jnp.zeros_like(buf)

    def piece_copy(local_row, global_row):
        return pltpu.make_async_copy(ys_ref.at[pl.ds(global_row, MOE_CHUNK)], buf.at[pl.ds(local_row, MOE_CHUNK)], sem)

    total = _for_each_piece(w, nchunk_ref, loff_ref, goff_ref, lambda lo, go: piece_copy(lo, go).start())
    pa, pb, wa, wb = _local_positions(cw_ref[...])
    r = lax.broadcasted_iota(jnp.int32, (TM, R_LOC), 1).astype(F32)
    wt = jnp.where(r == pa, wa, jnp.where(r == pb, wb, 0.0)).astype(BF16)

    def wait_piece(c, carry):
        piece_copy(0, 0).wait()
        return carry

    lax.fori_loop(0, total, wait_piece, 0)
    o_ref[...] = x_ref[...] + g2_ref[0] * _dot(wt, buf[...])


def _combine_call(layer, cw, x, mod, ys, tb):
    return pl.pallas_call(
        _combine_kernel,
        grid_spec=pltpu.PrefetchScalarGridSpec(
            num_scalar_prefetch=3,
            grid=(NW,),
            in_specs=[_row_spec(LANES), _row_spec(D_MODEL), _mod_spec(layer, 5), pl.BlockSpec(memory_space=pl.ANY)],
            out_specs=_row_spec(D_MODEL),
            scratch_shapes=[pltpu.VMEM((R_LOC, D_MODEL), BF16), pltpu.SemaphoreType.DMA],
        ),
        out_shape=jax.ShapeDtypeStruct((N_TOK, D_MODEL), F32),
        compiler_params=_params("arbitrary"),
        name=f"moe_combine{layer}",
    )(tb["nchunk"], tb["loff"], tb["goff"], cw, x, mod, ys)


def _moe_call(layer, h, cw, cnt, w_gate, w_up, w_down, x, mod):
    tb = _moe_tables(cnt)
    hs = _dispatch_call(layer, h, cw, tb)
    ys = _expert_call(layer, hs, tb, w_gate, w_up, w_down)
    return _combine_call(layer, cw, x, mod, ys, tb)


def _group_rms(x, g_ref, n):
    lane = lax.broadcasted_iota(jnp.int32, x.shape, 1)
    lo = lane < n
    xx = x * x
    ss_lo = jnp.sum(jnp.where(lo, xx, 0.0), axis=-1, keepdims=True)
    ss_hi = jnp.sum(jnp.where(lo, 0.0, xx), axis=-1, keepdims=True)
    ss = jnp.where(lo, ss_lo, ss_hi)
    return x * lax.rsqrt(ss * (1.0 / n) + EPS) * g_ref[...]


def _inproj1_kernel(x_ref, g_ref, sh_ref, sc_ref, w_ref, qg_ref, kg_ref, c_ref, s1_ref, s2_ref,
                    q_ref, k_ref, v_ref, kn_ref, vf_ref):
    i = pl.program_id(0)
    h = _rms(x_ref[...], g_ref[...]) * (1.0 + sc_ref[0]) + sh_ref[0]
    p = _dot(h.astype(BF16), w_ref[...])
    hw = DIFF_HEADS * DIFF_DV
    v = p[:, 2 * hw:]
    v_ref[...] = v.astype(BF16)
    half = DIFF_DK // 2
    for t in range(DIFF_HEADS):
        sl = slice(t * LANES, (t + 1) * LANES)
        qt = _group_rms(p[:, sl], qg_ref, DIFF_DK)
        q_ref[:, sl] = _rope_tile(qt, c_ref, s1_ref, s2_ref, half).astype(BF16)
        kt = _group_rms(p[:, hw + t * LANES:hw + (t + 1) * LANES], kg_ref, DIFF_DK)
        k_ref[:, sl] = _rope_tile(kt, c_ref, s1_ref, s2_ref, half).astype(BF16)

        @pl.when(i < NB_CTX)
        def _():
            kn_ref[:, sl] = kt

    @pl.when(i < NB_CTX)
    def _():
        vf_ref[...] = v


def _inproj1_call(x, mod, g1, w_qkv, qg, kg, tables):
    hw = DIFF_HEADS * DIFF_DV
    tspec = pl.BlockSpec((TM, LANES), lambda i: (_rope_block(i), 0))
    ctx_spec = pl.BlockSpec((TM, hw), lambda i: (jnp.minimum(i, NB_CTX - 1), 0))
    return pl.pallas_call(
        _inproj1_kernel,
        grid=(NB_TOK,),
        in_specs=[
            _row_spec(D_MODEL), _full_spec((1, D_MODEL)), _mod_spec(1, 0), _mod_spec(1, 1),
            _full_spec((D_MODEL, 3 * hw)), _full_spec((1, LANES)), _full_spec((1, LANES)), tspec, tspec, tspec,
        ],
        out_specs=[_row_spec(hw), _row_spec(hw), _row_spec(hw), ctx_spec, ctx_spec],
        out_shape=[
            jax.ShapeDtypeStruct((N_TOK, hw), BF16),
            jax.ShapeDtypeStruct((N_TOK, hw), BF16),
            jax.ShapeDtypeStruct((N_TOK, hw), BF16),
            jax.ShapeDtypeStruct((N_CTX, hw), F32),
            jax.ShapeDtypeStruct((N_CTX, hw), F32),
        ],
        compiler_params=_params("arbitrary"),
        name="inproj1",
    )(x, g1, mod, mod, w_qkv, qg, kg, *tables)


def _softmax_parts(s_list):
    m = s_list[0].max(axis=-1, keepdims=True)
    for s in s_list[1:]:
        m = jnp.maximum(m, s.max(axis=-1, keepdims=True))
    ps = [jnp.exp(s - m) for s in s_list]
    l = ps[0].sum(axis=-1, keepdims=True)
    for p in ps[1:]:
        l = l + p.sum(axis=-1, keepdims=True)
    return ps, l


def _diff_attn_kernel(has_cache, lambda_init, *refs):
    if has_cache:
        q_ref, k_ref, v_ref, kc_ref, vc_ref, lq1, lk1, lq2, lk2, sg_ref, o_ref = refs
    else:
        q_ref, k_ref, v_ref, lq1, lk1, lq2, lk2, sg_ref, o_ref = refs
    lam = (jnp.exp(jnp.sum(lq1[...] * lk1[...], axis=-1, keepdims=True))
           - jnp.exp(jnp.sum(lq2[...] * lk2[...], axis=-1, keepdims=True)) + lambda_init)
    q = q_ref[...]
    lane = lax.broadcasted_iota(jnp.int32, q.shape, 1)
    zero = jnp.zeros_like(q)
    q0 = jnp.where(lane < DIFF_DK, q, zero)
    q1 = jnp.where(lane < DIFF_DK, zero, q)
    krefs = [k_ref, kc_ref] if has_cache else [k_ref]
    vrefs = [v_ref, vc_ref] if has_cache else [v_ref]
    p0, l0 = _softmax_parts([_dot_nt(q0, kr[...].astype(BF16)) for kr in krefs])
    p1, l1 = _softmax_parts([_dot_nt(q1, kr[...].astype(BF16)) for kr in krefs])
    r0 = 1.0 / l0
    r1 = lam / l1
    o = None
    for a, b, vr in zip(p0, p1, vrefs):
        t = _dot((a * r0 - b * r1).astype(BF16), vr[...].astype(BF16))
        o = t if o is None else o + t
    o = o * lax.rsqrt(jnp.mean(o * o, axis=-1, keepdims=True) + EPS) * sg_ref[...] * (1.0 - lambda_init)
    o_ref[...] = o.astype(BF16)


def _diff_attn_ctx_call(lambda_init, q, k, v, lams, sg):
    lspec = _full_spec((1, DIFF_DK))
    return pl.pallas_call(
        functools.partial(_diff_attn_kernel, False, lambda_init),
        grid=(BATCH, DIFF_HEADS),
        in_specs=[pl.BlockSpec((SEQ, LANES), lambda b, h: (b, h))] * 3 + [lspec] * 4 + [_full_spec((1, DIFF_DV))],
        out_specs=pl.BlockSpec((SEQ, LANES), lambda b, h: (b, h)),
        out_shape=jax.ShapeDtypeStruct((N_CTX, DIFF_HEADS * DIFF_DV), BF16),
        compiler_params=_params("parallel", "parallel"),
        name="diff_attn_ctx",
    )(q, k, v, *lams, sg)


def _diff_attn_lat_call(lambda_init, q, k, v, kc, vc, lams, sg):
    nq = DEC_SEQ // TQ
    q0 = N_CTX // TQ
    k0 = N_CTX // DEC_SEQ
    lspec = _full_spec((1, DIFF_DK))
    kv_spec = pl.BlockSpec((DEC_SEQ, LANES), lambda b, h, qi: (k0 + b, h))
    c_spec = pl.BlockSpec((PAST_LEN, LANES), lambda b, h, qi: (b, h))
    return pl.pallas_call(
        functools.partial(_diff_attn_kernel, True, lambda_init),
        grid=(DEC_BATCH, DIFF_HEADS, nq),
        in_specs=[pl.BlockSpec((TQ, LANES), lambda b, h, qi: (q0 + b * nq + qi, h)), kv_spec, kv_spec, c_spec, c_spec]
        + [lspec] * 4 + [_full_spec((1, DIFF_DV))],
        out_specs=pl.BlockSpec((TQ, LANES), lambda b, h, qi: (b * nq + qi, h)),
        out_shape=jax.ShapeDtypeStruct((N_LAT, DIFF_HEADS * DIFF_DV), BF16),
        compiler_params=_params("parallel", "parallel", "parallel"),
        name="diff_attn_lat",
    )(q, k, v, kc, vc, *lams, sg)


def _router_weights(we, wg, be, bg):
    w = jnp.zeros((D_MODEL, LANES), F32).at[:, :N_EXPERTS].set(we).at[:, N_EXPERTS:N_EXPERTS + N_GROUPS].set(wg)
    b = jnp.zeros((1, LANES), F32).at[0, :N_EXPERTS].set(be).at[0, N_EXPERTS:N_EXPERTS + N_GROUPS].set(bg)
    hi = w.astype(BF16)
    lo = (w - hi.astype(F32)).astype(BF16)
    return hi, lo, b


def kernel(x_prompt, x_sample, cache_mla_ckv, cache_mla_krope, cache_diff_k, cache_diff_v, c, c_ctx, ada_w, ada_b, norm1_g, norm2_g, ev_w_in, ev_conv_w, ev_conv_b, ev_conv_ln_g, ev_conv_ln_b, ev_q_norm_g, ev_w_qb, ev_kv_norm_g, ev_w_kvb, ev_qn_g, ev_kn_g, ev_w_o, od_w_qkv, od_qn_g, od_kn_g, od_lambda_q1, od_lambda_k1, od_lambda_q2, od_lambda_k2, od_subln_g, od_w_o, moe_wg, moe_bg, moe_we, moe_be, moe_w_gate, moe_w_up, moe_w_down):
    x = jnp.concatenate([x_prompt.reshape(N_CTX, D_MODEL), x_sample.reshape(N_LAT, D_MODEL)], axis=0)
    cond = jnp.concatenate([c_ctx[None, :], c, jnp.zeros((MOD_ROWS - 1 - DEC_BATCH, D_MODEL), F32)], axis=0)
    mod = _mod_call(cond, ada_w, ada_b).reshape(-1, 1, D_MODEL)

    o3 = 2 * CONV_CH + MLA_Q_RANK + MLA_KV_RANK
    w_in = jnp.zeros((D_MODEL, EV_W), F32).at[:, :o3].set(ev_w_in[0][:, :o3])
    w_in = w_in.at[:, o3 + MLA_NOPE:o3 + MLA_QK].set(ev_w_in[0][:, o3:]).astype(BF16)
    u, qn, ckv, kr = _inproj0_call(x, mod, norm1_g[0][None], w_in, ev_q_norm_g[0][None], ev_kv_norm_g[0][None])
    conv = _conv_call(u, ev_conv_w[0], ev_conv_b[0][None], ev_conv_ln_g[0][None], ev_conv_ln_b[0][None])

    pad_qk = ((0, 0), (0, 0), (0, LANES - MLA_QK))
    wq = jnp.pad(ev_w_qb[0].reshape(MLA_Q_RANK, MLA_HEADS, MLA_QK), pad_qk).reshape(MLA_Q_RANK, -1).astype(BF16)
    wkv = ev_w_kvb[0].reshape(MLA_KV_RANK, MLA_HEADS, MLA_NOPE + MLA_V)
    wk = jnp.pad(wkv[..., :MLA_NOPE], ((0, 0), (0, 0), (0, LANES - MLA_NOPE))).reshape(MLA_KV_RANK, -1).astype(BF16)
    wv = wkv[..., MLA_NOPE:].reshape(MLA_KV_RANK, -1).astype(BF16)
    qg = jnp.pad(ev_qn_g[0] * (MLA_QK ** -0.5), (0, LANES - MLA_QK))[None]
    kg = jnp.pad(ev_kn_g[0], (0, LANES - MLA_QK))[None]
    tables = _rope_tables(MLA_ROPE, (MLA_NOPE,), DEC_SEQ)
    q0, k0, v0 = _mla_prep_call(qn, ckv, kr, wq, wk, wv, qg, kg, tables, _rope_block)
    kr_cache = jnp.pad(cache_mla_krope[:, 0].reshape(N_CACHE, MLA_ROPE), ((0, 0), (MLA_NOPE, LANES - MLA_QK)))
    _, kc0, vc0 = _mla_prep_call(jnp.zeros((N_CACHE, MLA_Q_RANK), BF16), cache_mla_ckv[:, 0].reshape(N_CACHE, MLA_KV_RANK),
                                 kr_cache, wq, wk, wv, qg, kg, tables, lambda i: BLK_PER_LAT)
    attn = jnp.concatenate([_mla_attn_ctx_call(q0, k0, v0), _mla_attn_lat_call(q0, k0, v0, kc0, vc0)], axis=0)

    w_o = ev_w_o[0].astype(BF16)
    r_hi, r_lo, r_b = _router_weights(moe_we[0], moe_wg[0], moe_be[0], moe_bg[0])
    x, h2, cw, cnt = _outproj_call(0, [conv, attn], [w_o[:CONV_CH], w_o[CONV_CH:]], x, mod, norm2_g[0][None], r_hi, r_lo, r_b)
    x = _moe_call(0, h2, cw, cnt, moe_w_gate[0], moe_w_up[0], moe_w_down[0], x, mod)

    lambda_init = 0.8 - 0.6 * math.exp(-0.3 * 1)
    qg1 = jnp.tile(od_qn_g[0] * (DIFF_DK ** -0.5), 2)[None]
    kg1 = jnp.tile(od_kn_g[0], 2)[None]
    tables1 = _rope_tables(DIFF_DK, (0, DIFF_DK), DEC_SEQ)
    q1, k1, v1, kn1, vf1 = _inproj1_call(x, mod, norm1_g[1][None], od_w_qkv[0].astype(BF16), qg1, kg1, tables1)
    lams = [od_lambda_q1[0][None], od_lambda_k1[0][None], od_lambda_q2[0][None], od_lambda_k2[0][None]]
    sg = od_subln_g[0][None]
    hw = DIFF_HEADS * DIFF_DV
    kc1 = cache_diff_k[:, 0].reshape(N_CACHE, hw)
    vc1 = cache_diff_v[:, 0].reshape(N_CACHE, hw)
    attn1 = jnp.concatenate([_diff_attn_ctx_call(lambda_init, q1, k1, v1, lams, sg),
                             _diff_attn_lat_call(lambda_init, q1, k1, v1, kc1, vc1, lams, sg)], axis=0)
    r_hi, r_lo, r_b = _router_weights(moe_we[1], moe_wg[1], moe_be[1], moe_bg[1])
    x, h2, cw, cnt = _outproj_call(1, [attn1], [od_w_o[0].astype(BF16)], x, mod, norm2_g[1][None], r_hi, r_lo, r_b)
    x = _moe_call(1, h2, cw, cnt, moe_w_gate[1], moe_w_up[1], moe_w_down[1], x, mod)

    y_prompt = x[:N_CTX].reshape(BATCH, SEQ, D_MODEL)
    y_sample = x[N_CTX:].reshape(DEC_BATCH, DEC_SEQ, D_MODEL)
    new_mla_ckv = ckv[:N_CTX].reshape(BATCH, 1, SEQ, MLA_KV_RANK)
    new_mla_krope = kr[:N_CTX, MLA_NOPE:MLA_QK].reshape(BATCH, 1, SEQ, MLA_ROPE)
    new_diff_k = kn1.reshape(BATCH, 1, SEQ, DIFF_HEADS, 2, DIFF_DK)
    new_diff_v = vf1.reshape(BATCH, 1, SEQ, DIFF_HEADS, DIFF_DV)
    return (y_prompt, y_sample, new_mla_ckv, new_mla_krope, new_diff_k, new_diff_v)
```

```python
import functools
import math

import jax
import jax.numpy as jnp
from jax import lax
from jax.experimental import pallas as pl
from jax.experimental.pallas import tpu as pltpu

F32 = jnp.float32
BF16 = jnp.bfloat16

D_MODEL = 1024
BATCH = 16
SEQ = 256
DEC_BATCH = 8
DEC_SEQ = 2048
PAST_LEN = 256
GRID_W = 64
ROPE_BASE = 10000.0
EPS = 1e-6

CONV_CH = 512
CONV_WIDTH = 31
CONV_PAD = 15
MLA_HEADS = 8
MLA_NOPE = 64
MLA_ROPE = 32
MLA_QK = 96
MLA_V = 64
MLA_Q_RANK = 384
MLA_KV_RANK = 256
DIFF_HEADS = 8
DIFF_DK = 64
DIFF_DV = 128
N_GROUPS = 4
EXPERTS_PER_GROUP = 8
N_EXPERTS = 32
EXPERT_FF = 256

LANES = 128
N_CTX = BATCH * SEQ
N_LAT = DEC_BATCH * DEC_SEQ
N_TOK = N_CTX + N_LAT
N_CACHE = DEC_BATCH * PAST_LEN
MOD_ROWS = 16
VMEM_LIMIT = 48 * 1024 * 1024

TM = 512
NB_CTX = N_CTX // TM
NB_TOK = N_TOK // TM
BLK_PER_LAT = DEC_SEQ // TM
TQ = 256
CONV_T = 256
CONV_HALO = 16
CONV_RB = 32


def _params(*sem):
    return pltpu.CompilerParams(dimension_semantics=sem, vmem_limit_bytes=VMEM_LIMIT)


def _sigmoid(x):
    return 1.0 / (1.0 + jnp.exp(-x))


def _rms(x, g):
    return x * lax.rsqrt(jnp.mean(x * x, axis=-1, keepdims=True) + EPS) * g


def _dot(a, b):
    return jnp.dot(a, b, preferred_element_type=F32)


def _dot_nt(a, b):
    return lax.dot_general(a, b, (((1,), (1,)), ((), ())), preferred_element_type=F32)


def _seq_of_block(i):
    return jnp.where(i < NB_CTX, 0, 1 + (i - NB_CTX) // BLK_PER_LAT)


def _mod_spec(layer, slot):
    return pl.BlockSpec((1, 1, D_MODEL), lambda i, *_: ((layer * MOD_ROWS + _seq_of_block(i)) * 6 + slot, 0, 0))


def _row_spec(width, tm=TM):
    return pl.BlockSpec((tm, width), lambda i, *_: (i, 0))


def _ctx_spec(width):
    return pl.BlockSpec((TM, width), lambda i, *_: (jnp.minimum(i, NB_CTX - 1), 0))


def _lat_spec(width):
    return pl.BlockSpec((TM, width), lambda i, *_: (jnp.maximum(i - NB_CTX, 0), 0))


def _split_specs(width):
    return [_ctx_spec(width), _lat_spec(width)]


def _pick(ctx_ref, lat_ref):
    return jnp.where(pl.program_id(0) < NB_CTX, ctx_ref[...], lat_ref[...])


def _full_spec(shape):
    nd = len(shape)
    return pl.BlockSpec(shape, lambda *_: (0,) * nd)


def _mod_kernel(c_ref, w_ref, b_ref, o_ref):
    c = c_ref[...]
    s = (c * _sigmoid(c)).astype(BF16)
    o_ref[0] = _dot(s, w_ref[0].astype(BF16)) + b_ref[0]


def _mod_call(cond, ada_w, ada_b):
    depth = ada_w.shape[0]
    tn = 1024
    return pl.pallas_call(
        _mod_kernel,
        grid=(depth, 6 * D_MODEL // tn),
        in_specs=[
            pl.BlockSpec((MOD_ROWS, D_MODEL), lambda l, j: (0, 0)),
            pl.BlockSpec((1, D_MODEL, tn), lambda l, j: (l, 0, j)),
            pl.BlockSpec((1, 1, tn), lambda l, j: (l, 0, j)),
        ],
        out_specs=pl.BlockSpec((1, MOD_ROWS, tn), lambda l, j: (l, 0, j)),
        out_shape=jax.ShapeDtypeStruct((depth, MOD_ROWS, 6 * D_MODEL), F32),
        compiler_params=_params("parallel", "parallel"),
        name="adaln_mod",
    )(cond, ada_w, ada_b.reshape(depth, 1, 6 * D_MODEL))


EV_W = 1792


def _inproj0_kernel(xc_ref, xl_ref, g_ref, sh_ref, sc_ref, w_ref, qg_ref, kvg_ref, u_ref, qn_ref, ckv_ref, kr_ref):
    h = _rms(_pick(xc_ref, xl_ref), g_ref[...]) * (1.0 + sc_ref[0]) + sh_ref[0]
    p = _dot(h.astype(BF16), w_ref[...])
    u_ref[...] = p[:, :CONV_CH] * _sigmoid(p[:, CONV_CH:2 * CONV_CH])
    o1 = 2 * CONV_CH
    o2 = o1 + MLA_Q_RANK
    o3 = o2 + MLA_KV_RANK
    qn_ref[...] = _rms(p[:, o1:o2], qg_ref[...]).astype(BF16)
    ckv_ref[...] = _rms(p[:, o2:o3], kvg_ref[...])
    kr_ref[...] = p[:, o3:o3 + LANES]


def _inproj0_call(x_ctx, x_lat, mod, g1, w_in, qg, kvg):
    return pl.pallas_call(
        _inproj0_kernel,
        grid=(NB_TOK,),
        in_specs=_split_specs(D_MODEL) + [
            _full_spec((1, D_MODEL)), _mod_spec(0, 0), _mod_spec(0, 1),
            _full_spec((D_MODEL, EV_W)), _full_spec((1, MLA_Q_RANK)), _full_spec((1, MLA_KV_RANK)),
        ],
        out_specs=[_row_spec(CONV_CH), _row_spec(MLA_Q_RANK), _row_spec(MLA_KV_RANK), _row_spec(LANES)],
        out_shape=[
            jax.ShapeDtypeStruct((N_TOK, CONV_CH), F32),
            jax.ShapeDtypeStruct((N_TOK, MLA_Q_RANK), BF16),
            jax.ShapeDtypeStruct((N_TOK, MLA_KV_RANK), F32),
            jax.ShapeDtypeStruct((N_TOK, LANES), F32),
        ],
        compiler_params=_params("parallel"),
        name="inproj0",
    )(x_ctx, x_lat, g1, mod, mod, w_in, qg, kvg)


CONV_CHUNKS_PER_LAT = DEC_SEQ // CONV_T
CONV_NB_CTX = N_CTX // CONV_T
CONV_NB = N_TOK // CONV_T


def _conv_kernel(prev_ref, cur_ref, nxt_ref, w_ref, b_ref, lg_ref, lb_ref, o_ref, buf):
    i = pl.program_id(0)
    j = (i - CONV_NB_CTX) % CONV_CHUNKS_PER_LAT
    first = jnp.logical_or(i < CONV_NB_CTX, j == 0)
    last = jnp.logical_or(i < CONV_NB_CTX, j == CONV_CHUNKS_PER_LAT - 1)
    buf[0:CONV_HALO, :] = jnp.where(first, 0.0, prev_ref[...])
    buf[CONV_HALO:CONV_HALO + CONV_T, :] = cur_ref[...]
    buf[CONV_HALO + CONV_T:, :] = jnp.where(last, 0.0, nxt_ref[...])
    for rb in range(CONV_T // CONV_RB):
        base = rb * CONV_RB + CONV_HALO - CONV_PAD
        acc = jnp.zeros((CONV_RB, CONV_CH), F32)
        for k in range(CONV_WIDTH):
            acc = acc + w_ref[k:k + 1, :] * buf[base + k:base + k + CONV_RB, :]
        y = acc + b_ref[...]
        mu = jnp.mean(y, axis=-1, keepdims=True)
        yc = y - mu
        var = jnp.mean(yc * yc, axis=-1, keepdims=True)
        z = yc * lax.rsqrt(var + EPS) * lg_ref[...] + lb_ref[...]
        o_ref[rb * CONV_RB:(rb + 1) * CONV_RB, :] = (z * _sigmoid(z)).astype(BF16)


def _conv_call(u, w, b, lg, lb):
    hb = CONV_T // CONV_HALO
    n_halo_blocks = N_TOK // CONV_HALO
    return pl.pallas_call(
        _conv_kernel,
        grid=(CONV_NB,),
        in_specs=[
            pl.BlockSpec((CONV_HALO, CONV_CH), lambda i: (jnp.maximum(i * hb - 1, 0), 0)),
            pl.BlockSpec((CONV_T, CONV_CH), lambda i: (i, 0)),
            pl.BlockSpec((CONV_HALO, CONV_CH), lambda i: (jnp.minimum((i + 1) * hb, n_halo_blocks - 1), 0)),
            _full_spec((CONV_WIDTH, CONV_CH)), _full_spec((1, CONV_CH)),
            _full_spec((1, CONV_CH)), _full_spec((1, CONV_CH)),
        ],
        out_specs=pl.BlockSpec((CONV_T, CONV_CH), lambda i: (i, 0)),
        out_shape=jax.ShapeDtypeStruct((N_TOK, CONV_CH), BF16),
        scratch_shapes=[pltpu.VMEM((CONV_T + 2 * CONV_HALO, CONV_CH), F32)],
        compiler_params=_params("parallel"),
        name="conformer_conv",
    )(u, u, u, w, b, lg, lb)


def _rope_tile(x, c_ref, s1_ref, s2_ref, half):
    up = pltpu.roll(x, LANES - half, 1)
    dn = pltpu.roll(x, half, 1)
    return x * c_ref[...] + up * s1_ref[...] + dn * s2_ref[...]


def _rope_tables(rot_dim, lane_starts, n_tok):
    rows = n_tok // GRID_W
    row_ids = jnp.repeat(jnp.arange(rows, dtype=F32), GRID_W)
    col_ids = jnp.tile(jnp.arange(GRID_W, dtype=F32), rows)
    quarter = rot_dim // 4
    half = rot_dim // 2
    freqs = jnp.power(ROPE_BASE, -jnp.arange(quarter, dtype=F32) / quarter)
    ang = jnp.concatenate([row_ids[:, None] * freqs, col_ids[:, None] * freqs], axis=-1)
    cos, sin = jnp.cos(ang), jnp.sin(ang)
    one = lambda n: jnp.ones((n_tok, n), F32)
    zero = lambda n: jnp.zeros((n_tok, n), F32)
    c, s1, s2, at = [], [], [], 0
    for st in lane_starts:
        c += [one(st - at), cos, cos]
        s1 += [zero(st - at), -sin, zero(half)]
        s2 += [zero(st - at), zero(half), sin]
        at = st + rot_dim
    c, s1, s2 = (jnp.concatenate(t + [fill(LANES - at)], axis=1) for t, fill in ((c, one), (s1, zero), (s2, zero)))
    ident = (jnp.ones((TM, LANES), F32), jnp.zeros((TM, LANES), F32), jnp.zeros((TM, LANES), F32))
    return tuple(jnp.concatenate([t, e], axis=0) for t, e in zip((c, s1, s2), ident))


def _rope_block(i):
    return jnp.where(i < NB_CTX, BLK_PER_LAT, (i - NB_CTX) % BLK_PER_LAT)


def _mla_prep_kernel(qn_ref, ckv_ref, kr_ref, wq_ref, wk_ref, wv_ref, qg_ref, kg_ref, c_ref, s1_ref, s2_ref,
                     q_ref, k_ref, v_ref):
    qf = _dot(qn_ref[...], wq_ref[...])
    ckv = ckv_ref[...].astype(BF16)
    kf = _dot(ckv, wk_ref[...])
    v_ref[...] = _dot(ckv, wv_ref[...]).astype(BF16)
    kr = kr_ref[...]
    half = MLA_ROPE // 2
    for h in range(MLA_HEADS):
        sl = slice(h * LANES, (h + 1) * LANES)
        qh = qf[:, sl]
        qh = qh * lax.rsqrt(jnp.sum(qh * qh, axis=-1, keepdims=True) * (1.0 / MLA_QK) + EPS) * qg_ref[...]
        q_ref[:, sl] = _rope_tile(qh, c_ref, s1_ref, s2_ref, half).astype(BF16)
        kh = kf[:, sl] + kr
        kh = kh * lax.rsqrt(jnp.sum(kh * kh, axis=-1, keepdims=True) * (1.0 / MLA_QK) + EPS) * kg_ref[...]
        k_ref[:, sl] = _rope_tile(kh, c_ref, s1_ref, s2_ref, half).astype(BF16)


def _mla_prep_call(qn, ckv, kr, wq, wk, wv, qg, kg, tables, rope_block_fn):
    n = qn.shape[0]
    hw = MLA_HEADS * LANES
    tspec = pl.BlockSpec((TM, LANES), lambda i: (rope_block_fn(i), 0))
    return pl.pallas_call(
        _mla_prep_kernel,
        grid=(n // TM,),
        in_specs=[
            _row_spec(MLA_Q_RANK), _row_spec(MLA_KV_RANK), _row_spec(LANES),
            _full_spec((MLA_Q_RANK, hw)), _full_spec((MLA_KV_RANK, hw)), _full_spec((MLA_KV_RANK, MLA_HEADS * MLA_V)),
            _full_spec((1, LANES)), _full_spec((1, LANES)), tspec, tspec, tspec,
        ],
        out_specs=[_row_spec(hw), _row_spec(hw), _row_spec(MLA_HEADS * MLA_V)],
        out_shape=[
            jax.ShapeDtypeStruct((n, hw), BF16),
            jax.ShapeDtypeStruct((n, hw), BF16),
            jax.ShapeDtypeStruct((n, MLA_HEADS * MLA_V), BF16),
        ],
        compiler_params=_params("parallel"),
        name="mla_prep",
    )(qn, ckv, kr, wq, wk, wv, qg, kg, *tables)


def _mla_attn_kernel(has_cache, *refs):
    if has_cache:
        q_ref, k_ref, v_ref, kc_ref, vc_ref, o_ref = refs
    else:
        q_ref, k_ref, v_ref, o_ref = refs
    outs = []
    for hh in range(2):
        sl = slice(hh * LANES, (hh + 1) * LANES)
        q = q_ref[:, sl]
        s1 = _dot_nt(q, k_ref[:, sl])
        m = jnp.max(s1, axis=-1, keepdims=True)
        if has_cache:
            s2 = _dot_nt(q, kc_ref[:, sl])
            m = jnp.maximum(m, jnp.max(s2, axis=-1, keepdims=True))
        p1 = jnp.exp(s1 - m)
        l = jnp.sum(p1, axis=-1, keepdims=True)
        o = _dot(p1.astype(BF16), v_ref[...])
        if has_cache:
            p2 = jnp.exp(s2 - m)
            l = l + jnp.sum(p2, axis=-1, keepdims=True)
            o = o + _dot(p2.astype(BF16), vc_ref[...])
        outs.append(o / l)
    lane = lax.broadcasted_iota(jnp.int32, outs[0].shape, 1)
    o_ref[...] = jnp.where(lane < MLA_V, outs[0], outs[1]).astype(BF16)


def _mla_attn_ctx_call(q, k, v):
    return pl.pallas_call(
        functools.partial(_mla_attn_kernel, False),
        grid=(BATCH, MLA_HEADS // 2),
        in_specs=[
            pl.BlockSpec((SEQ, 2 * LANES), lambda b, hp: (b, hp)),
            pl.BlockSpec((SEQ, 2 * LANES), lambda b, hp: (b, hp)),
            pl.BlockSpec((SEQ, LANES), lambda b, hp: (b, hp)),
        ],
        out_specs=pl.BlockSpec((SEQ, LANES), lambda b, hp: (b, hp)),
        out_shape=jax.ShapeDtypeStruct((N_CTX, MLA_HEADS * MLA_V), BF16),
        compiler_params=_params("parallel", "parallel"),
        name="mla_attn_ctx",
    )(q, k, v)


def _mla_attn_lat_call(q, k, v, kc, vc):
    nq = DEC_SEQ // TQ
    q0 = N_CTX // TQ
    k0 = N_CTX // DEC_SEQ
    return pl.pallas_call(
        functools.partial(_mla_attn_kernel, True),
        grid=(DEC_BATCH, MLA_HEADS // 2, nq),
        in_specs=[
            pl.BlockSpec((TQ, 2 * LANES), lambda b, hp, qi: (q0 + b * nq + qi, hp)),
            pl.BlockSpec((DEC_SEQ, 2 * LANES), lambda b, hp, qi: (k0 + b, hp)),
            pl.BlockSpec((DEC_SEQ, LANES), lambda b, hp, qi: (k0 + b, hp)),
            pl.BlockSpec((PAST_LEN, 2 * LANES), lambda b, hp, qi: (b, hp)),
            pl.BlockSpec((PAST_LEN, LANES), lambda b, hp, qi: (b, hp)),
        ],
        out_specs=pl.BlockSpec((TQ, LANES), lambda b, hp, qi: (b * nq + qi, hp)),
        out_shape=jax.ShapeDtypeStruct((N_LAT, MLA_HEADS * MLA_V), BF16),
        compiler_params=_params("parallel", "parallel", "parallel"),
        name="mla_attn_lat",
    )(q, k, v, kc, vc)


def _route(logits):
    lane = lax.broadcasted_iota(jnp.int32, logits.shape, 1)
    lane_f = lane.astype(F32)
    big = float(LANES)
    neg = -jnp.inf
    gmask = jnp.logical_and(lane >= N_EXPERTS, lane < N_EXPERTS + N_GROUPS)
    gl = jnp.where(gmask, logits, neg)
    ge = jnp.exp(gl - jnp.max(gl, axis=-1, keepdims=True))
    gp = ge / jnp.sum(ge, axis=-1, keepdims=True)
    gate = jnp.max(gp, axis=-1, keepdims=True)
    gidx = jnp.min(jnp.where(jnp.logical_and(gmask, gp == gate), lane_f, big), axis=-1, keepdims=True) - N_EXPERTS
    lo = gidx * EXPERTS_PER_GROUP
    emask = jnp.logical_and(lane_f >= lo, lane_f < lo + EXPERTS_PER_GROUP)
    el = jnp.where(emask, logits, neg)
    ee = jnp.exp(el - jnp.max(el, axis=-1, keepdims=True))
    ep = jnp.where(emask, ee / jnp.sum(ee, axis=-1, keepdims=True), -1.0)
    p1 = jnp.max(ep, axis=-1, keepdims=True)
    i1 = jnp.min(jnp.where(ep == p1, lane_f, big), axis=-1, keepdims=True)
    ep2 = jnp.where(lane_f == i1, -1.0, ep)
    p2 = jnp.max(ep2, axis=-1, keepdims=True)
    i2 = jnp.min(jnp.where(ep2 == p2, lane_f, big), axis=-1, keepdims=True)
    den = p1 + p2
    return jnp.where(lane_f == i1, gate * p1 / den, 0.0) + jnp.where(lane_f == i2, gate * p2 / den, 0.0)


def _outproj_kernel(arity, *refs):
    refs = list(refs)
    rows = []
    for n in arity:
        rows.append(refs[0][...] if n == 1 else _pick(refs[0], refs[1]))
        refs = refs[n:]
    n_act = len(arity) - 1
    w_refs, refs = refs[:n_act], refs[n_act:]
    g1_ref, n2_ref, sh2_ref, sc2_ref, rh_ref, rl_ref, rb_ref, xo_ref, h_ref, cw_ref, cnt_ref = refs
    out = _dot(rows[0], w_refs[0][...])
    for a, w_ref in zip(rows[1:n_act], w_refs[1:]):
        out = out + _dot(a, w_ref[...])
    x = rows[n_act] + g1_ref[0] * out
    xo_ref[...] = x
    h = _rms(x, n2_ref[...]) * (1.0 + sc2_ref[0]) + sh2_ref[0]
    h_hi = h.astype(BF16)
    h_ref[...] = h_hi
    h_lo = (h - h_hi.astype(F32)).astype(BF16)
    logits = _dot(h_hi, rh_ref[...]) + (_dot(h_hi, rl_ref[...]) + _dot(h_lo, rh_ref[...])) + rb_ref[...]
    cw = _route(logits)
    cw_ref[...] = cw
    cnt_ref[0] = jnp.sum(jnp.where(cw > 0.0, 1.0, 0.0), axis=0, keepdims=True)


def _outproj_call(layer, acts, ws, x, mod, n2, r_hi, r_lo, r_b):
    operands = list(acts) + [x]
    row_specs = []
    for op in operands:
        row_specs += [_row_spec(op[0].shape[1])] if len(op) == 1 else _split_specs(op[0].shape[1])
    flat = [a for op in operands for a in op]
    return pl.pallas_call(
        functools.partial(_outproj_kernel, tuple(len(op) for op in operands)),
        grid=(NB_TOK,),
        in_specs=(
            row_specs + [_full_spec(w.shape) for w in ws]
            + [_mod_spec(layer, 2), _full_spec((1, D_MODEL)), _mod_spec(layer, 3),
               _mod_spec(layer, 4), _full_spec((D_MODEL, LANES)), _full_spec((D_MODEL, LANES)), _full_spec((1, LANES))]
        ),
        out_specs=[_row_spec(D_MODEL), _row_spec(D_MODEL), _row_spec(LANES),
                   pl.BlockSpec((1, 1, LANES), lambda i: (i, 0, 0))],
        out_shape=[
            jax.ShapeDtypeStruct((N_TOK, D_MODEL), F32),
            jax.ShapeDtypeStruct((N_TOK, D_MODEL), BF16),
            jax.ShapeDtypeStruct((N_TOK, LANES), F32),
            jax.ShapeDtypeStruct((NB_TOK, 1, LANES), F32),
        ],
        compiler_params=_params("parallel"),
        name=f"outproj{layer}",
    )(*flat, *ws, mod, n2, mod, mod, r_hi, r_lo, r_b)


MOE_CHUNK = 16
MOE_TILE = 512
NW = N_TOK // TM
R_LOC = 2 * TM + N_EXPERTS * MOE_CHUNK
R_SORTED_MAX = 2 * N_TOK + NW * N_EXPERTS * (MOE_CHUNK - 1) + N_EXPERTS * (MOE_TILE - 1)
NT_MAX = -(-R_SORTED_MAX // MOE_TILE)


def _moe_tables(cnt):
    cnt = cnt[:, 0, :N_EXPERTS].astype(jnp.int32)
    padc = (cnt + (MOE_CHUNK - 1)) // MOE_CHUNK * MOE_CHUNK
    loff = jnp.cumsum(padc, axis=1) - padc
    tot = jnp.sum(padc, axis=0)
    seg = (tot + (MOE_TILE - 1)) // MOE_TILE * MOE_TILE
    seg_end = jnp.cumsum(seg)
    base = seg_end - seg
    goff = base[None, :] + jnp.cumsum(padc, axis=0) - padc
    tile_start = jnp.arange(NT_MAX, dtype=jnp.int32) * MOE_TILE
    texp = jnp.minimum(jnp.sum((tile_start[:, None] >= seg_end[None, :]).astype(jnp.int32), axis=1), N_EXPERTS - 1)
    nvalid = (seg_end[-1:] // MOE_TILE).astype(jnp.int32)
    i32 = lambda a: a.reshape(-1).astype(jnp.int32)
    return dict(nchunk=i32(padc // MOE_CHUNK), loff=i32(loff), goff=i32(goff), tail_start=i32(base + tot),
                tail_chunks=i32((seg - tot) // MOE_CHUNK), texp=texp, nvalid=nvalid)


def _local_positions(cw):
    n = cw.shape[0]
    a = cw > 0.0
    a_f = jnp.where(a, 1.0, 0.0)
    below = jnp.where(lax.broadcasted_iota(jnp.int32, (n, n), 1) < lax.broadcasted_iota(jnp.int32, (n, n), 0), 1.0, 0.0)
    rank = _dot(below.astype(BF16), a_f.astype(BF16))
    cnt = jnp.sum(a_f, axis=0, keepdims=True)
    padc = jnp.floor((cnt + (MOE_CHUNK - 1)) * (1.0 / MOE_CHUNK)) * MOE_CHUNK
    before = jnp.where(lax.broadcasted_iota(jnp.int32, (LANES, LANES), 0) < lax.broadcasted_iota(jnp.int32, (LANES, LANES), 1),
                       1.0, 0.0)
    loff = _dot(jnp.broadcast_to(padc, (8, LANES)).astype(BF16), before.astype(BF16))[0:1]
    lpos = rank + loff
    pa = jnp.min(jnp.where(a, lpos, float(R_LOC)), axis=-1, keepdims=True)
    pb = jnp.max(jnp.where(a, lpos, -1.0), axis=-1, keepdims=True)
    wa = jnp.sum(jnp.where(jnp.logical_and(a, lpos == pa), cw, 0.0), axis=-1, keepdims=True)
    wb = jnp.sum(jnp.where(jnp.logical_and(a, lpos == pb), cw, 0.0), axis=-1, keepdims=True)
    return pa, pb, wa, wb


def _for_each_piece(w, nchunk_ref, loff_ref, goff_ref, fn):
    def per_expert(e, total):
        n = nchunk_ref[w * N_EXPERTS + e]
        lo = loff_ref[w * N_EXPERTS + e]
        go = goff_ref[w * N_EXPERTS + e]

        def piece(c, carry):
            fn(pl.multiple_of(lo + c * MOE_CHUNK, MOE_CHUNK), pl.multiple_of(go + c * MOE_CHUNK, MOE_CHUNK))
            return carry

        lax.fori_loop(0, n, piece, 0)
        return total + n

    return lax.fori_loop(0, N_EXPERTS, per_expert, 0)


def _dispatch_kernel(nchunk_ref, loff_ref, goff_ref, tstart_ref, tchunk_ref, h_ref, cw_ref, hs_ref, buf, zbuf, sem):
    w = pl.program_id(0)
    pa, pb, _, _ = _local_positions(cw_ref[...])
    lane = lax.broadcasted_iota(jnp.int32, (TM, LANES), 1)
    pos_t = jnp.where(lane == 0, pa, jnp.where(lane == 1, pb, 0.0)).T
    r = lax.broadcasted_iota(jnp.int32, (R_LOC, TM), 0).astype(F32)
    sel = jnp.logical_or(r == pos_t[0:1, :], r == pos_t[1:2, :])
    buf[...] = _dot(jnp.where(sel, 1.0, 0.0).astype(BF16), h_ref[...]).astype(BF16)

    def piece_copy(local_row, global_row):
        return pltpu.make_async_copy(buf.at[pl.ds(local_row, MOE_CHUNK)], hs_ref.at[pl.ds(global_row, MOE_CHUNK)], sem)

    total = _for_each_piece(w, nchunk_ref, loff_ref, goff_ref, lambda lo, go: piece_copy(lo, go).start())

    def wait_piece(c, carry):
        piece_copy(0, 0).wait()
        return carry

    lax.fori_loop(0, total, wait_piece, 0)

    @pl.when(w == NW - 1)
    def _():
        zbuf[...] = jnp.zeros_like(zbuf)

        def zero_copy(global_row):
            return pltpu.make_async_copy(zbuf, hs_ref.at[pl.ds(global_row, MOE_CHUNK)], sem)

        def per_expert(e, total):
            n = tchunk_ref[e]
            st = tstart_ref[e]

            def piece(c, carry):
                zero_copy(pl.multiple_of(st + c * MOE_CHUNK, MOE_CHUNK)).start()
                return carry

            lax.fori_loop(0, n, piece, 0)
            return total + n

        total_z = lax.fori_loop(0, N_EXPERTS, per_expert, 0)

        def wait_zero(c, carry):
            zero_copy(0).wait()
            return carry

        lax.fori_loop(0, total_z, wait_zero, 0)


def _dispatch_call(layer, h, cw, tb):
    return pl.pallas_call(
        _dispatch_kernel,
        grid_spec=pltpu.PrefetchScalarGridSpec(
            num_scalar_prefetch=5,
            grid=(NW,),
            in_specs=[_row_spec(D_MODEL), _row_spec(LANES)],
            out_specs=pl.BlockSpec(memory_space=pl.ANY),
            scratch_shapes=[pltpu.VMEM((R_LOC, D_MODEL), BF16), pltpu.VMEM((MOE_CHUNK, D_MODEL), BF16),
                            pltpu.SemaphoreType.DMA],
        ),
        out_shape=jax.ShapeDtypeStruct((NT_MAX * MOE_TILE, D_MODEL), BF16),
        compiler_params=_params("arbitrary"),
        name=f"moe_dispatch{layer}",
    )(tb["nchunk"], tb["loff"], tb["goff"], tb["tail_start"], tb["tail_chunks"], h, cw)


def _expert_kernel(texp_ref, nvalid_ref, hs_ref, wg_ref, wu_ref, wd_ref, ys_ref, wg_b, wu_b, wd_b):
    i = pl.program_id(0)
    fresh = jnp.logical_or(i == 0, texp_ref[i] != texp_ref[jnp.maximum(i - 1, 0)])

    @pl.when(fresh)
    def _():
        wg_b[...] = wg_ref[0].astype(BF16)
        wu_b[...] = wu_ref[0].astype(BF16)
        wd_b[...] = wd_ref[0].astype(BF16)

    @pl.when(i < nvalid_ref[0])
    def _():
        h = hs_ref[...]
        a = _dot(h, wg_b[...])
        b = _dot(h, wu_b[...])
        ys_ref[...] = _dot((a * _sigmoid(a) * b).astype(BF16), wd_b[...]).astype(BF16)


def _expert_call(layer, hs, tb, w_gate, w_up, w_down):
    tile_map = lambda i, texp, nv: (jnp.minimum(i, nv[0] - 1), 0)
    w_map = lambda i, texp, nv: (texp[i], 0, 0)
    return pl.pallas_call(
        _expert_kernel,
        grid_spec=pltpu.PrefetchScalarGridSpec(
            num_scalar_prefetch=2,
            grid=(NT_MAX,),
            in_specs=[
                pl.BlockSpec((MOE_TILE, D_MODEL), tile_map),
                pl.BlockSpec((1, D_MODEL, EXPERT_FF), w_map),
                pl.BlockSpec((1, D_MODEL, EXPERT_FF), w_map),
                pl.BlockSpec((1, EXPERT_FF, D_MODEL), w_map),
            ],
            out_specs=pl.BlockSpec((MOE_TILE, D_MODEL), tile_map),
            scratch_shapes=[pltpu.VMEM((D_MODEL, EXPERT_FF), BF16), pltpu.VMEM((D_MODEL, EXPERT_FF), BF16),
                            pltpu.VMEM((EXPERT_FF, D_MODEL), BF16)],
        ),
        out_shape=jax.ShapeDtypeStruct((NT_MAX * MOE_TILE, D_MODEL), BF16),
        compiler_params=_params("arbitrary"),
        name=f"moe_experts{layer}",
    )(tb["texp"], tb["nvalid"], hs, w_gate, w_up, w_down)


def _combine_kernel(split_out, nchunk_ref, loff_ref, goff_ref, cw_ref, x_ref, g2_ref, ys_ref, *refs):
    o_refs, (buf, sem) = refs[:-2], refs[-2:]
    w = pl.program_id(0)

    @pl.when(w == 0)
    def _():
        buf[...] = jnp.zeros_like(buf)

    def piece_copy(local_row, global_row):
        return pltpu.make_async_copy(ys_ref.at[pl.ds(global_row, MOE_CHUNK)], buf.at[pl.ds(local_row, MOE_CHUNK)], sem)

    total = _for_each_piece(w, nchunk_ref, loff_ref, goff_ref, lambda lo, go: piece_copy(lo, go).start())
    pa, pb, wa, wb = _local_positions(cw_ref[...])
    r = lax.broadcasted_iota(jnp.int32, (TM, R_LOC), 1).astype(F32)
    wt = jnp.where(r == pa, wa, jnp.where(r == pb, wb, 0.0)).astype(BF16)

    def wait_piece(c, carry):
        piece_copy(0, 0).wait()
        return carry

    lax.fori_loop(0, total, wait_piece, 0)
    y = x_ref[...] + g2_ref[0] * _dot(wt, buf[...])
    if split_out:
        @pl.when(w < NB_CTX)
        def _():
            o_refs[0][...] = y

        @pl.when(w >= NB_CTX)
        def _():
            o_refs[1][...] = y
    else:
        o_refs[0][...] = y


def _combine_call(layer, cw, x, mod, ys, tb, split_out):
    if split_out:
        out_specs = _split_specs(D_MODEL)
        out_shape = [jax.ShapeDtypeStruct((N_CTX, D_MODEL), F32), jax.ShapeDtypeStruct((N_LAT, D_MODEL), F32)]
    else:
        out_specs = [_row_spec(D_MODEL)]
        out_shape = [jax.ShapeDtypeStruct((N_TOK, D_MODEL), F32)]
    return pl.pallas_call(
        functools.partial(_combine_kernel, split_out),
        grid_spec=pltpu.PrefetchScalarGridSpec(
            num_scalar_prefetch=3,
            grid=(NW,),
            in_specs=[_row_spec(LANES), _row_spec(D_MODEL), _mod_spec(layer, 5), pl.BlockSpec(memory_space=pl.ANY)],
            out_specs=out_specs,
            scratch_shapes=[pltpu.VMEM((R_LOC, D_MODEL), BF16), pltpu.SemaphoreType.DMA],
        ),
        out_shape=out_shape,
        compiler_params=_params("arbitrary"),
        name=f"moe_combine{layer}",
    )(tb["nchunk"], tb["loff"], tb["goff"], cw, x, mod, ys)


def _moe_call(layer, h, cw, cnt, w_gate, w_up, w_down, x, mod, split_out=False):
    tb = _moe_tables(cnt)
    hs = _dispatch_call(layer, h, cw, tb)
    ys = _expert_call(layer, hs, tb, w_gate, w_up, w_down)
    return _combine_call(layer, cw, x, mod, ys, tb, split_out)


def _group_rms(x, g_ref, n):
    lane = lax.broadcasted_iota(jnp.int32, x.shape, 1)
    lo = lane < n
    xx = x * x
    ss_lo = jnp.sum(jnp.where(lo, xx, 0.0), axis=-1, keepdims=True)
    ss_hi = jnp.sum(jnp.where(lo, 0.0, xx), axis=-1, keepdims=True)
    ss = jnp.where(lo, ss_lo, ss_hi)
    return x * lax.rsqrt(ss * (1.0 / n) + EPS) * g_ref[...]


def _inproj1_kernel(x_ref, g_ref, sh_ref, sc_ref, w_ref, qg_ref, kg_ref, c_ref, s1_ref, s2_ref,
                    q_ref, k_ref, v_ref, kn_ref, vf_ref):
    i = pl.program_id(0)
    h = _rms(x_ref[...], g_ref[...]) * (1.0 + sc_ref[0]) + sh_ref[0]
    p = _dot(h.astype(BF16), w_ref[...])
    hw = DIFF_HEADS * DIFF_DV
    v = p[:, 2 * hw:]
    v_ref[...] = v.astype(BF16)
    half = DIFF_DK // 2
    for t in range(DIFF_HEADS):
        sl = slice(t * LANES, (t + 1) * LANES)
        qt = _group_rms(p[:, sl], qg_ref, DIFF_DK)
        q_ref[:, sl] = _rope_tile(qt, c_ref, s1_ref, s2_ref, half).astype(BF16)
        kt = _group_rms(p[:, hw + t * LANES:hw + (t + 1) * LANES], kg_ref, DIFF_DK)
        k_ref[:, sl] = _rope_tile(kt, c_ref, s1_ref, s2_ref, half).astype(BF16)

        @pl.when(i < NB_CTX)
        def _():
            kn_ref[:, sl] = kt

    @pl.when(i < NB_CTX)
    def _():
        vf_ref[...] = v


def _inproj1_call(x, mod, g1, w_qkv, qg, kg, tables):
    hw = DIFF_HEADS * DIFF_DV
    tspec = pl.BlockSpec((TM, LANES), lambda i: (_rope_block(i), 0))
    ctx_spec = pl.BlockSpec((TM, hw), lambda i: (jnp.minimum(i, NB_CTX - 1), 0))
    return pl.pallas_call(
        _inproj1_kernel,
        grid=(NB_TOK,),
        in_specs=[
            _row_spec(D_MODEL), _full_spec((1, D_MODEL)), _mod_spec(1, 0), _mod_spec(1, 1),
            _full_spec((D_MODEL, 3 * hw)), _full_spec((1, LANES)), _full_spec((1, LANES)), tspec, tspec, tspec,
        ],
        out_specs=[_row_spec(hw), _row_spec(hw), _row_spec(hw), ctx_spec, ctx_spec],
        out_shape=[
            jax.ShapeDtypeStruct((N_TOK, hw), BF16),
            jax.ShapeDtypeStruct((N_TOK, hw), BF16),
            jax.ShapeDtypeStruct((N_TOK, hw), BF16),
            jax.ShapeDtypeStruct((N_CTX, hw), F32),
            jax.ShapeDtypeStruct((N_CTX, hw), F32),
        ],
        compiler_params=_params("arbitrary"),
        name="inproj1",
    )(x, g1, mod, mod, w_qkv, qg, kg, *tables)


def _softmax_parts(s_list):
    m = s_list[0].max(axis=-1, keepdims=True)
    for s in s_list[1:]:
        m = jnp.maximum(m, s.max(axis=-1, keepdims=True))
    ps = [jnp.exp(s - m) for s in s_list]
    l = ps[0].sum(axis=-1, keepdims=True)
    for p in ps[1:]:
        l = l + p.sum(axis=-1, keepdims=True)
    return ps, l


def _diff_attn_kernel(has_cache, lambda_init, *refs):
    if has_cache:
        q_ref, k_ref, v_ref, kc_ref, vc_ref, lq1, lk1, lq2, lk2, sg_ref, o_ref = refs
    else:
        q_ref, k_ref, v_ref, lq1, lk1, lq2, lk2, sg_ref, o_ref = refs
    lam = (jnp.exp(jnp.sum(lq1[...] * lk1[...], axis=-1, keepdims=True))
           - jnp.exp(jnp.sum(lq2[...] * lk2[...], axis=-1, keepdims=True)) + lambda_init)
    q = q_ref[...]
    lane = lax.broadcasted_iota(jnp.int32, q.shape, 1)
    zero = jnp.zeros_like(q)
    q0 = jnp.where(lane < DIFF_DK, q, zero)
    q1 = jnp.where(lane < DIFF_DK, zero, q)
    krefs = [k_ref, kc_ref] if has_cache else [k_ref]
    vrefs = [v_ref, vc_ref] if has_cache else [v_ref]
    p0, l0 = _softmax_parts([_dot_nt(q0, kr[...].astype(BF16)) for kr in krefs])
    p1, l1 = _softmax_parts([_dot_nt(q1, kr[...].astype(BF16)) for kr in krefs])
    r0 = 1.0 / l0
    r1 = lam / l1
    o = None
    for a, b, vr in zip(p0, p1, vrefs):
        t = _dot((a * r0 - b * r1).astype(BF16), vr[...].astype(BF16))
        o = t if o is None else o + t
    o = o * lax.rsqrt(jnp.mean(o * o, axis=-1, keepdims=True) + EPS) * sg_ref[...] * (1.0 - lambda_init)
    o_ref[...] = o.astype(BF16)


def _diff_attn_ctx_call(lambda_init, q, k, v, lams, sg):
    lspec = _full_spec((1, DIFF_DK))
    return pl.pallas_call(
        functools.partial(_diff_attn_kernel, False, lambda_init),
        grid=(BATCH, DIFF_HEADS),
        in_specs=[pl.BlockSpec((SEQ, LANES), lambda b, h: (b, h))] * 3 + [lspec] * 4 + [_full_spec((1, DIFF_DV))],
        out_specs=pl.BlockSpec((SEQ, LANES), lambda b, h: (b, h)),
        out_shape=jax.ShapeDtypeStruct((N_CTX, DIFF_HEADS * DIFF_DV), BF16),
        compiler_params=_params("parallel", "parallel"),
        name="diff_attn_ctx",
    )(q, k, v, *lams, sg)


def _diff_attn_lat_call(lambda_init, q, k, v, kc, vc, lams, sg):
    nq = DEC_SEQ // TQ
    q0 = N_CTX // TQ
    k0 = N_CTX // DEC_SEQ
    lspec = _full_spec((1, DIFF_DK))
    kv_spec = pl.BlockSpec((DEC_SEQ, LANES), lambda b, h, qi: (k0 + b, h))
    c_spec = pl.BlockSpec((PAST_LEN, LANES), lambda b, h, qi: (b, h))
    return pl.pallas_call(
        functools.partial(_diff_attn_kernel, True, lambda_init),
        grid=(DEC_BATCH, DIFF_HEADS, nq),
        in_specs=[pl.BlockSpec((TQ, LANES), lambda b, h, qi: (q0 + b * nq + qi, h)), kv_spec, kv_spec, c_spec, c_spec]
        + [lspec] * 4 + [_full_spec((1, DIFF_DV))],
        out_specs=pl.BlockSpec((TQ, LANES), lambda b, h, qi: (b * nq + qi, h)),
        out_shape=jax.ShapeDtypeStruct((N_LAT, DIFF_HEADS * DIFF_DV), BF16),
        compiler_params=_params("parallel", "parallel", "parallel"),
        name="diff_attn_lat",
    )(q, k, v, kc, vc, *lams, sg)


def _router_weights(we, wg, be, bg):
    pad = LANES - N_EXPERTS - N_GROUPS
    w = jnp.concatenate([we, wg, jnp.zeros((D_MODEL, pad), F32)], axis=1)
    b = jnp.concatenate([be, bg, jnp.zeros((pad,), F32)])[None]
    hi = w.astype(BF16)
    lo = (w - hi.astype(F32)).astype(BF16)
    return hi, lo, b


def kernel(x_prompt, x_sample, cache_mla_ckv, cache_mla_krope, cache_diff_k, cache_diff_v, c, c_ctx, ada_w, ada_b, norm1_g, norm2_g, ev_w_in, ev_conv_w, ev_conv_b, ev_conv_ln_g, ev_conv_ln_b, ev_q_norm_g, ev_w_qb, ev_kv_norm_g, ev_w_kvb, ev_qn_g, ev_kn_g, ev_w_o, od_w_qkv, od_qn_g, od_kn_g, od_lambda_q1, od_lambda_k1, od_lambda_q2, od_lambda_k2, od_subln_g, od_w_o, moe_wg, moe_bg, moe_we, moe_be, moe_w_gate, moe_w_up, moe_w_down):
    x_in = (x_prompt.reshape(N_CTX, D_MODEL), x_sample.reshape(N_LAT, D_MODEL))
    cond = jnp.concatenate([c_ctx[None, :], c, jnp.zeros((MOD_ROWS - 1 - DEC_BATCH, D_MODEL), F32)], axis=0)
    mod = _mod_call(cond, ada_w, ada_b).reshape(-1, 1, D_MODEL)

    o3 = 2 * CONV_CH + MLA_Q_RANK + MLA_KV_RANK
    w_in = jnp.concatenate([ev_w_in[0][:, :o3], jnp.zeros((D_MODEL, MLA_NOPE), F32), ev_w_in[0][:, o3:],
                            jnp.zeros((D_MODEL, LANES - MLA_QK), F32)], axis=1).astype(BF16)
    u, qn, ckv, kr = _inproj0_call(*x_in, mod, norm1_g[0][None], w_in, ev_q_norm_g[0][None], ev_kv_norm_g[0][None])
    conv = _conv_call(u, ev_conv_w[0], ev_conv_b[0][None], ev_conv_ln_g[0][None], ev_conv_ln_b[0][None])

    pad_qk = ((0, 0), (0, 0), (0, LANES - MLA_QK))
    wq = jnp.pad(ev_w_qb[0].reshape(MLA_Q_RANK, MLA_HEADS, MLA_QK), pad_qk).reshape(MLA_Q_RANK, -1).astype(BF16)
    wkv = ev_w_kvb[0].reshape(MLA_KV_RANK, MLA_HEADS, MLA_NOPE + MLA_V)
    wk = jnp.pad(wkv[..., :MLA_NOPE], ((0, 0), (0, 0), (0, LANES - MLA_NOPE))).reshape(MLA_KV_RANK, -1).astype(BF16)
    wv = wkv[..., MLA_NOPE:].reshape(MLA_KV_RANK, -1).astype(BF16)
    qg = jnp.pad(ev_qn_g[0] * (MLA_QK ** -0.5), (0, LANES - MLA_QK))[None]
    kg = jnp.pad(ev_kn_g[0], (0, LANES - MLA_QK))[None]
    tables = _rope_tables(MLA_ROPE, (MLA_NOPE,), DEC_SEQ)
    q0, k0, v0 = _mla_prep_call(qn, ckv, kr, wq, wk, wv, qg, kg, tables, _rope_block)
    kr_cache = jnp.pad(cache_mla_krope[:, 0].reshape(N_CACHE, MLA_ROPE), ((0, 0), (MLA_NOPE, LANES - MLA_QK)))
    _, kc0, vc0 = _mla_prep_call(jnp.zeros((N_CACHE, MLA_Q_RANK), BF16), cache_mla_ckv[:, 0].reshape(N_CACHE, MLA_KV_RANK),
                                 kr_cache, wq, wk, wv, qg, kg, tables, lambda i: BLK_PER_LAT)
    attn = (_mla_attn_ctx_call(q0, k0, v0), _mla_attn_lat_call(q0, k0, v0, kc0, vc0))

    w_o = ev_w_o[0].astype(BF16)
    r_hi, r_lo, r_b = _router_weights(moe_we[0], moe_wg[0], moe_be[0], moe_bg[0])
    x, h2, cw, cnt = _outproj_call(0, [(conv,), attn], [w_o[:CONV_CH], w_o[CONV_CH:]], x_in, mod, norm2_g[0][None], r_hi, r_lo, r_b)
    (x,) = _moe_call(0, h2, cw, cnt, moe_w_gate[0], moe_w_up[0], moe_w_down[0], x, mod)

    lambda_init = 0.8 - 0.6 * math.exp(-0.3 * 1)
    qg1 = jnp.tile(od_qn_g[0] * (DIFF_DK ** -0.5), 2)[None]
    kg1 = jnp.tile(od_kn_g[0], 2)[None]
    tables1 = _rope_tables(DIFF_DK, (0, DIFF_DK), DEC_SEQ)
    q1, k1, v1, kn1, vf1 = _inproj1_call(x, mod, norm1_g[1][None], od_w_qkv[0].astype(BF16), qg1, kg1, tables1)
    lams = [od_lambda_q1[0][None], od_lambda_k1[0][None], od_lambda_q2[0][None], od_lambda_k2[0][None]]
    sg = od_subln_g[0][None]
    hw = DIFF_HEADS * DIFF_DV
    kc1 = cache_diff_k[:, 0].reshape(N_CACHE, hw)
    vc1 = cache_diff_v[:, 0].reshape(N_CACHE, hw)
    attn1 = (_diff_attn_ctx_call(lambda_init, q1, k1, v1, lams, sg),
             _diff_attn_lat_call(lambda_init, q1, k1, v1, kc1, vc1, lams, sg))
    r_hi, r_lo, r_b = _router_weights(moe_we[1], moe_wg[1], moe_be[1], moe_bg[1])
    x, h2, cw, cnt = _outproj_call(1, [attn1], [od_w_o[0].astype(BF16)], (x,), mod, norm2_g[1][None], r_hi, r_lo, r_b)
    y_ctx, y_lat = _moe_call(1, h2, cw, cnt, moe_w_gate[1], moe_w_up[1], moe_w_down[1], x, mod, split_out=True)

    y_prompt = y_ctx.reshape(BATCH, SEQ, D_MODEL)
    y_sample = y_lat.reshape(DEC_BATCH, DEC_SEQ, D_MODEL)
    new_mla_ckv = ckv[:N_CTX].reshape(BATCH, 1, SEQ, MLA_KV_RANK)
    new_mla_krope = kr[:N_CTX, MLA_NOPE:MLA_QK].reshape(BATCH, 1, SEQ, MLA_ROPE)
    new_diff_k = kn1.reshape(BATCH, 1, SEQ, DIFF_HEADS, 2, DIFF_DK)
    new_diff_v = vf1.reshape(BATCH, 1, SEQ, DIFF_HEADS, DIFF_DV)
    return (y_prompt, y_sample, new_mla_ckv, new_mla_krope, new_diff_k, new_diff_v)
```

```python
import functools
import math

import jax
import jax.numpy as jnp
from jax import lax
from jax.experimental import pallas as pl
from jax.experimental.pallas import tpu as pltpu

F32 = jnp.float32
BF16 = jnp.bfloat16

D_MODEL = 1024
BATCH = 16
SEQ = 256
DEC_BATCH = 8
DEC_SEQ = 2048
PAST_LEN = 256
GRID_W = 64
ROPE_BASE = 10000.0
EPS = 1e-6

CONV_CH = 512
CONV_WIDTH = 31
CONV_PAD = 15
MLA_HEADS = 8
MLA_NOPE = 64
MLA_ROPE = 32
MLA_QK = 96
MLA_V = 64
MLA_Q_RANK = 384
MLA_KV_RANK = 256
DIFF_HEADS = 8
DIFF_DK = 64
DIFF_DV = 128
N_GROUPS = 4
EXPERTS_PER_GROUP = 8
N_EXPERTS = 32
EXPERT_FF = 256

LANES = 128
SUBLANES = 8
N_CTX = BATCH * SEQ
N_LAT = DEC_BATCH * DEC_SEQ
N_TOK = N_CTX + N_LAT
N_CACHE = DEC_BATCH * PAST_LEN
MOD_ROWS = 16
VMEM_LIMIT = 48 * 1024 * 1024

TM = 512
NB_CTX = N_CTX // TM
NB_TOK = N_TOK // TM
BLK_PER_LAT = DEC_SEQ // TM
TQ = 256
CONV_T = 256
CONV_HALO = 16
CONV_RB = 32
CONV_SHIFT_ROWS = CONV_T + 2 * CONV_HALO - SUBLANES


def _params(*sem):
    return pltpu.CompilerParams(dimension_semantics=sem, vmem_limit_bytes=VMEM_LIMIT)


def _sigmoid(x):
    return 1.0 / (1.0 + jnp.exp(-x))


def _rms(x, g):
    return x * lax.rsqrt(jnp.mean(x * x, axis=-1, keepdims=True) + EPS) * g


def _dot(a, b):
    return jnp.dot(a, b, preferred_element_type=F32)


def _dot_nt(a, b):
    return lax.dot_general(a, b, (((1,), (1,)), ((), ())), preferred_element_type=F32)


def _seq_of_block(i):
    return jnp.where(i < NB_CTX, 0, 1 + (i - NB_CTX) // BLK_PER_LAT)


def _mod_spec(layer, slot):
    return pl.BlockSpec((1, 1, D_MODEL), lambda i, *_: ((layer * MOD_ROWS + _seq_of_block(i)) * 6 + slot, 0, 0))


def _row_spec(width, tm=TM):
    return pl.BlockSpec((tm, width), lambda i, *_: (i, 0))


def _ctx_spec(width):
    return pl.BlockSpec((TM, width), lambda i, *_: (jnp.minimum(i, NB_CTX - 1), 0))


def _lat_spec(width):
    return pl.BlockSpec((TM, width), lambda i, *_: (jnp.maximum(i - NB_CTX, 0), 0))


def _split_specs(width):
    return [_ctx_spec(width), _lat_spec(width)]


def _pick(ctx_ref, lat_ref):
    return jnp.where(pl.program_id(0) < NB_CTX, ctx_ref[...], lat_ref[...])


def _ones_blocks(group_log2):
    r = lax.broadcasted_iota(jnp.int32, (LANES, LANES), 0) >> group_log2
    c = lax.broadcasted_iota(jnp.int32, (LANES, LANES), 1) >> group_log2
    return jnp.where(r == c, 1.0, 0.0).astype(BF16)


def _full_spec(shape):
    nd = len(shape)
    return pl.BlockSpec(shape, lambda *_: (0,) * nd)


def _mod_kernel(c_ref, w_ref, b_ref, o_ref):
    c = c_ref[...]
    s = (c * _sigmoid(c)).astype(BF16)
    o_ref[0] = _dot(s, w_ref[0].astype(BF16)) + b_ref[0]


def _mod_call(cond, ada_w, ada_b):
    depth = ada_w.shape[0]
    tn = 1024
    return pl.pallas_call(
        _mod_kernel,
        grid=(depth, 6 * D_MODEL // tn),
        in_specs=[
            pl.BlockSpec((MOD_ROWS, D_MODEL), lambda l, j: (0, 0)),
            pl.BlockSpec((1, D_MODEL, tn), lambda l, j: (l, 0, j)),
            pl.BlockSpec((1, 1, tn), lambda l, j: (l, 0, j)),
        ],
        out_specs=pl.BlockSpec((1, MOD_ROWS, tn), lambda l, j: (l, 0, j)),
        out_shape=jax.ShapeDtypeStruct((depth, MOD_ROWS, 6 * D_MODEL), F32),
        compiler_params=_params("parallel", "parallel"),
        name="adaln_mod",
    )(cond, ada_w, ada_b.reshape(depth, 1, 6 * D_MODEL))


EV_W = 1792


def _inproj0_kernel(xc_ref, xl_ref, g_ref, sh_ref, sc_ref, w_ref, qg_ref, kvg_ref, u_ref, qn_ref, ckv_ref, kr_ref):
    h = _rms(_pick(xc_ref, xl_ref), g_ref[...]) * (1.0 + sc_ref[0]) + sh_ref[0]
    p = _dot(h.astype(BF16), w_ref[...])
    u_ref[...] = p[:, :CONV_CH] * _sigmoid(p[:, CONV_CH:2 * CONV_CH])
    o1 = 2 * CONV_CH
    o2 = o1 + MLA_Q_RANK
    o3 = o2 + MLA_KV_RANK
    qn_ref[...] = _rms(p[:, o1:o2], qg_ref[...]).astype(BF16)
    ckv_ref[...] = _rms(p[:, o2:o3], kvg_ref[...])
    kr_ref[...] = p[:, o3:o3 + LANES]


def _inproj0_call(x_ctx, x_lat, mod, g1, w_in, qg, kvg):
    return pl.pallas_call(
        _inproj0_kernel,
        grid=(NB_TOK,),
        in_specs=_split_specs(D_MODEL) + [
            _full_spec((1, D_MODEL)), _mod_spec(0, 0), _mod_spec(0, 1),
            _full_spec((D_MODEL, EV_W)), _full_spec((1, MLA_Q_RANK)), _full_spec((1, MLA_KV_RANK)),
        ],
        out_specs=[_row_spec(CONV_CH), _row_spec(MLA_Q_RANK), _row_spec(MLA_KV_RANK), _row_spec(LANES)],
        out_shape=[
            jax.ShapeDtypeStruct((N_TOK, CONV_CH), F32),
            jax.ShapeDtypeStruct((N_TOK, MLA_Q_RANK), BF16),
            jax.ShapeDtypeStruct((N_TOK, MLA_KV_RANK), F32),
            jax.ShapeDtypeStruct((N_TOK, LANES), F32),
        ],
        compiler_params=_params("parallel"),
        name="inproj0",
    )(x_ctx, x_lat, g1, mod, mod, w_in, qg, kvg)


CONV_CHUNKS_PER_LAT = DEC_SEQ // CONV_T
CONV_NB_CTX = N_CTX // CONV_T
CONV_NB = N_TOK // CONV_T


def _conv_kernel(prev_ref, cur_ref, nxt_ref, w_ref, b_ref, lg_ref, lb_ref, o_ref, buf, shifted):
    i = pl.program_id(0)
    j = (i - CONV_NB_CTX) % CONV_CHUNKS_PER_LAT
    first = jnp.logical_or(i < CONV_NB_CTX, j == 0)
    last = jnp.logical_or(i < CONV_NB_CTX, j == CONV_CHUNKS_PER_LAT - 1)
    buf[0:CONV_HALO, :] = jnp.where(first, 0.0, prev_ref[...])
    buf[CONV_HALO:CONV_HALO + CONV_T, :] = cur_ref[...]
    buf[CONV_HALO + CONV_T:, :] = jnp.where(last, 0.0, nxt_ref[...])
    for s in range(1, SUBLANES):
        shifted[s - 1] = buf[s:s + CONV_SHIFT_ROWS, :]
    for rb in range(CONV_T // CONV_RB):
        acc = jnp.zeros((CONV_RB, CONV_CH), F32)
        for k in range(CONV_WIDTH):
            row = rb * CONV_RB + CONV_HALO - CONV_PAD + k
            s, r0 = row % SUBLANES, row - row % SUBLANES
            src = buf[r0:r0 + CONV_RB, :] if s == 0 else shifted[s - 1, r0:r0 + CONV_RB, :]
            acc = acc + w_ref[k:k + 1, :] * src
        y = acc + b_ref[...]
        mu = jnp.mean(y, axis=-1, keepdims=True)
        yc = y - mu
        var = jnp.mean(yc * yc, axis=-1, keepdims=True)
        z = yc * lax.rsqrt(var + EPS) * lg_ref[...] + lb_ref[...]
        o_ref[rb * CONV_RB:(rb + 1) * CONV_RB, :] = (z * _sigmoid(z)).astype(BF16)


def _conv_call(u, w, b, lg, lb):
    hb = CONV_T // CONV_HALO
    n_halo_blocks = N_TOK // CONV_HALO
    return pl.pallas_call(
        _conv_kernel,
        grid=(CONV_NB,),
        in_specs=[
            pl.BlockSpec((CONV_HALO, CONV_CH), lambda i: (jnp.maximum(i * hb - 1, 0), 0)),
            pl.BlockSpec((CONV_T, CONV_CH), lambda i: (i, 0)),
            pl.BlockSpec((CONV_HALO, CONV_CH), lambda i: (jnp.minimum((i + 1) * hb, n_halo_blocks - 1), 0)),
            _full_spec((CONV_WIDTH, CONV_CH)), _full_spec((1, CONV_CH)),
            _full_spec((1, CONV_CH)), _full_spec((1, CONV_CH)),
        ],
        out_specs=pl.BlockSpec((CONV_T, CONV_CH), lambda i: (i, 0)),
        out_shape=jax.ShapeDtypeStruct((N_TOK, CONV_CH), BF16),
        scratch_shapes=[pltpu.VMEM((CONV_T + 2 * CONV_HALO, CONV_CH), F32),
                        pltpu.VMEM((SUBLANES - 1, CONV_SHIFT_ROWS, CONV_CH), F32)],
        compiler_params=_params("parallel"),
        name="conformer_conv",
    )(u, u, u, w, b, lg, lb)


def _rope_tile(x, c_ref, s1_ref, s2_ref, half):
    up = pltpu.roll(x, LANES - half, 1)
    dn = pltpu.roll(x, half, 1)
    return x * c_ref[...] + up * s1_ref[...] + dn * s2_ref[...]


def _rope_tables(rot_dim, lane_starts, n_tok):
    rows = n_tok // GRID_W
    row_ids = jnp.repeat(jnp.arange(rows, dtype=F32), GRID_W)
    col_ids = jnp.tile(jnp.arange(GRID_W, dtype=F32), rows)
    quarter = rot_dim // 4
    half = rot_dim // 2
    freqs = jnp.power(ROPE_BASE, -jnp.arange(quarter, dtype=F32) / quarter)
    ang = jnp.concatenate([row_ids[:, None] * freqs, col_ids[:, None] * freqs], axis=-1)
    cos, sin = jnp.cos(ang), jnp.sin(ang)
    one = lambda n: jnp.ones((n_tok, n), F32)
    zero = lambda n: jnp.zeros((n_tok, n), F32)
    c, s1, s2, at = [], [], [], 0
    for st in lane_starts:
        c += [one(st - at), cos, cos]
        s1 += [zero(st - at), -sin, zero(half)]
        s2 += [zero(st - at), zero(half), sin]
        at = st + rot_dim
    c, s1, s2 = (jnp.concatenate(t + [fill(LANES - at)], axis=1) for t, fill in ((c, one), (s1, zero), (s2, zero)))
    ident = (jnp.ones((TM, LANES), F32), jnp.zeros((TM, LANES), F32), jnp.zeros((TM, LANES), F32))
    return tuple(jnp.concatenate([t, e], axis=0) for t, e in zip((c, s1, s2), ident))


def _rope_block(i):
    return jnp.where(i < NB_CTX, BLK_PER_LAT, (i - NB_CTX) % BLK_PER_LAT)


def _mla_prep_kernel(qn_ref, ckv_ref, kr_ref, wq_ref, wk_ref, wv_ref, qg_ref, kg_ref, c_ref, s1_ref, s2_ref,
                     q_ref, k_ref, v_ref):
    qf = _dot(qn_ref[...], wq_ref[...])
    ckv = ckv_ref[...].astype(BF16)
    kf = _dot(ckv, wk_ref[...])
    v_ref[...] = _dot(ckv, wv_ref[...]).astype(BF16)
    kr = kr_ref[...]
    half = MLA_ROPE // 2
    ones = _ones_blocks(LANES.bit_length() - 1)
    for h in range(MLA_HEADS):
        sl = slice(h * LANES, (h + 1) * LANES)
        qh = _group_rms(qf[:, sl], qg_ref, ones, MLA_QK)
        q_ref[:, sl] = _rope_tile(qh, c_ref, s1_ref, s2_ref, half).astype(BF16)
        kh = _group_rms(kf[:, sl] + kr, kg_ref, ones, MLA_QK)
        k_ref[:, sl] = _rope_tile(kh, c_ref, s1_ref, s2_ref, half).astype(BF16)


def _mla_prep_call(qn, ckv, kr, wq, wk, wv, qg, kg, tables, rope_block_fn):
    n = qn.shape[0]
    hw = MLA_HEADS * LANES
    tspec = pl.BlockSpec((TM, LANES), lambda i: (rope_block_fn(i), 0))
    return pl.pallas_call(
        _mla_prep_kernel,
        grid=(n // TM,),
        in_specs=[
            _row_spec(MLA_Q_RANK), _row_spec(MLA_KV_RANK), _row_spec(LANES),
            _full_spec((MLA_Q_RANK, hw)), _full_spec((MLA_KV_RANK, hw)), _full_spec((MLA_KV_RANK, MLA_HEADS * MLA_V)),
            _full_spec((1, LANES)), _full_spec((1, LANES)), tspec, tspec, tspec,
        ],
        out_specs=[_row_spec(hw), _row_spec(hw), _row_spec(MLA_HEADS * MLA_V)],
        out_shape=[
            jax.ShapeDtypeStruct((n, hw), BF16),
            jax.ShapeDtypeStruct((n, hw), BF16),
            jax.ShapeDtypeStruct((n, MLA_HEADS * MLA_V), BF16),
        ],
        compiler_params=_params("parallel"),
        name="mla_prep",
    )(qn, ckv, kr, wq, wk, wv, qg, kg, *tables)


LOG2E = 1.4426950408889634
EXP2_SAFE = 100.0
BF16_NORM_MARGIN = 1.02


def _max_sq_norm(x, mask=None):
    xf = x.astype(F32)
    xx = xf * xf
    if mask is not None:
        xx = jnp.where(mask, xx, 0.0)
    return jnp.max(jnp.sum(xx, axis=-1, keepdims=True), axis=0, keepdims=True)


def _gain_sq_bound(g_ref, dims):
    g = g_ref[...]
    return jnp.max(g * g, axis=-1, keepdims=True) * (dims * BF16_NORM_MARGIN)


def _exps(s_list, shifted):
    if shifted:
        m = s_list[0].max(axis=-1, keepdims=True)
        for s in s_list[1:]:
            m = jnp.maximum(m, s.max(axis=-1, keepdims=True))
        es = [jnp.exp2(s - m) for s in s_list]
    else:
        es = [jnp.exp2(s) for s in s_list]
    l = es[0].sum(axis=-1, keepdims=True)
    for e in es[1:]:
        l = l + e.sum(axis=-1, keepdims=True)
    return es, l


def _run_query_blocks(bound_sq, n_blocks, block):
    safe = bound_sq[0, 0] <= EXP2_SAFE * EXP2_SAFE

    @pl.when(safe)
    def _():
        lax.fori_loop(0, n_blocks, functools.partial(block, False), 0)

    @pl.when(jnp.logical_not(safe))
    def _():
        lax.fori_loop(0, n_blocks, functools.partial(block, True), 0)


def _mla_attn_kernel(has_cache, tq, qg_ref, kg_ref, *refs):
    if has_cache:
        q_ref, k_ref, v_ref, kc_ref, vc_ref, o_ref = refs
        k_refs, v_refs = [k_ref, kc_ref], [v_ref, vc_ref]
    else:
        q_ref, k_ref, v_ref, o_ref = refs
        k_refs, v_refs = [k_ref], [v_ref]
    bound_sq = _gain_sq_bound(qg_ref, MLA_QK) * _gain_sq_bound(kg_ref, MLA_QK)

    def block(shifted, qi, carry):
        r0 = pl.multiple_of(qi * tq, tq)
        outs = []
        for hh in range(2):
            sl = slice(hh * LANES, (hh + 1) * LANES)
            q = q_ref[pl.ds(r0, tq), sl]
            es, l = _exps([_dot_nt(q, kr[:, sl]) for kr in k_refs], shifted)
            o = _dot(es[0].astype(BF16), v_refs[0][...])
            for e, vr in zip(es[1:], v_refs[1:]):
                o = o + _dot(e.astype(BF16), vr[...])
            outs.append(o / l)
        lane = lax.broadcasted_iota(jnp.int32, outs[0].shape, 1)
        o_ref[pl.ds(r0, tq), :] = jnp.where(lane < MLA_V, outs[0], outs[1]).astype(BF16)
        return carry

    _run_query_blocks(bound_sq, q_ref.shape[0] // tq, block)


def _mla_attn_ctx_call(qg, kg, q, k, v):
    return pl.pallas_call(
        functools.partial(_mla_attn_kernel, False, SEQ),
        grid=(BATCH, MLA_HEADS // 2),
        in_specs=[
            _full_spec((1, LANES)), _full_spec((1, LANES)),
            pl.BlockSpec((SEQ, 2 * LANES), lambda b, hp: (b, hp)),
            pl.BlockSpec((SEQ, 2 * LANES), lambda b, hp: (b, hp)),
            pl.BlockSpec((SEQ, LANES), lambda b, hp: (b, hp)),
        ],
        out_specs=pl.BlockSpec((SEQ, LANES), lambda b, hp: (b, hp)),
        out_shape=jax.ShapeDtypeStruct((N_CTX, MLA_HEADS * MLA_V), BF16),
        compiler_params=_params("parallel", "parallel"),
        name="mla_attn_ctx",
    )(qg, kg, q, k, v)


def _mla_attn_lat_call(qg, kg, q, k, v, kc, vc):
    k0 = N_CTX // DEC_SEQ
    return pl.pallas_call(
        functools.partial(_mla_attn_kernel, True, TQ),
        grid=(DEC_BATCH, MLA_HEADS // 2),
        in_specs=[
            _full_spec((1, LANES)), _full_spec((1, LANES)),
            pl.BlockSpec((DEC_SEQ, 2 * LANES), lambda b, hp: (k0 + b, hp)),
            pl.BlockSpec((DEC_SEQ, 2 * LANES), lambda b, hp: (k0 + b, hp)),
            pl.BlockSpec((DEC_SEQ, LANES), lambda b, hp: (k0 + b, hp)),
            pl.BlockSpec((PAST_LEN, 2 * LANES), lambda b, hp: (b, hp)),
            pl.BlockSpec((PAST_LEN, LANES), lambda b, hp: (b, hp)),
        ],
        out_specs=pl.BlockSpec((DEC_SEQ, LANES), lambda b, hp: (b, hp)),
        out_shape=jax.ShapeDtypeStruct((N_LAT, MLA_HEADS * MLA_V), BF16),
        compiler_params=_params("parallel", "parallel"),
        name="mla_attn_lat",
    )(qg, kg, q, k, v, kc, vc)


def _route(logits):
    lane = lax.broadcasted_iota(jnp.int32, logits.shape, 1)
    lane_f = lane.astype(F32)
    big = float(LANES)
    neg = -jnp.inf
    gmask = jnp.logical_and(lane >= N_EXPERTS, lane < N_EXPERTS + N_GROUPS)
    gl = jnp.where(gmask, logits, neg)
    ge = jnp.exp(gl - jnp.max(gl, axis=-1, keepdims=True))
    gp = ge / jnp.sum(ge, axis=-1, keepdims=True)
    gate = jnp.max(gp, axis=-1, keepdims=True)
    gidx = jnp.min(jnp.where(jnp.logical_and(gmask, gp == gate), lane_f, big), axis=-1, keepdims=True) - N_EXPERTS
    lo = gidx * EXPERTS_PER_GROUP
    emask = jnp.logical_and(lane_f >= lo, lane_f < lo + EXPERTS_PER_GROUP)
    el = jnp.where(emask, logits, neg)
    ee = jnp.exp(el - jnp.max(el, axis=-1, keepdims=True))
    ep = jnp.where(emask, ee / jnp.sum(ee, axis=-1, keepdims=True), -1.0)
    p1 = jnp.max(ep, axis=-1, keepdims=True)
    i1 = jnp.min(jnp.where(ep == p1, lane_f, big), axis=-1, keepdims=True)
    ep2 = jnp.where(lane_f == i1, -1.0, ep)
    p2 = jnp.max(ep2, axis=-1, keepdims=True)
    i2 = jnp.min(jnp.where(ep2 == p2, lane_f, big), axis=-1, keepdims=True)
    den = p1 + p2
    return jnp.where(lane_f == i1, gate * p1 / den, 0.0) + jnp.where(lane_f == i2, gate * p2 / den, 0.0)


def _outproj_kernel(arity, *refs):
    refs = list(refs)
    rows = []
    for n in arity:
        rows.append(refs[0][...] if n == 1 else _pick(refs[0], refs[1]))
        refs = refs[n:]
    n_act = len(arity) - 1
    w_refs, refs = refs[:n_act], refs[n_act:]
    g1_ref, n2_ref, sh2_ref, sc2_ref, rh_ref, rl_ref, rb_ref, xo_ref, h_ref, cw_ref, cnt_ref = refs
    out = _dot(rows[0], w_refs[0][...])
    for a, w_ref in zip(rows[1:n_act], w_refs[1:]):
        out = out + _dot(a, w_ref[...])
    x = rows[n_act] + g1_ref[0] * out
    xo_ref[...] = x
    h = _rms(x, n2_ref[...]) * (1.0 + sc2_ref[0]) + sh2_ref[0]
    h_hi = h.astype(BF16)
    h_ref[...] = h_hi
    h_lo = (h - h_hi.astype(F32)).astype(BF16)
    logits = _dot(h_hi, rh_ref[...]) + (_dot(h_hi, rl_ref[...]) + _dot(h_lo, rh_ref[...])) + rb_ref[...]
    cw = _route(logits)
    cw_ref[...] = cw
    cnt_ref[0] = jnp.sum(jnp.where(cw > 0.0, 1.0, 0.0), axis=0, keepdims=True)


def _outproj_call(layer, acts, ws, x, mod, n2, r_hi, r_lo, r_b):
    operands = list(acts) + [x]
    row_specs = []
    for op in operands:
        row_specs += [_row_spec(op[0].shape[1])] if len(op) == 1 else _split_specs(op[0].shape[1])
    flat = [a for op in operands for a in op]
    return pl.pallas_call(
        functools.partial(_outproj_kernel, tuple(len(op) for op in operands)),
        grid=(NB_TOK,),
        in_specs=(
            row_specs + [_full_spec(w.shape) for w in ws]
            + [_mod_spec(layer, 2), _full_spec((1, D_MODEL)), _mod_spec(layer, 3),
               _mod_spec(layer, 4), _full_spec((D_MODEL, LANES)), _full_spec((D_MODEL, LANES)), _full_spec((1, LANES))]
        ),
        out_specs=[_row_spec(D_MODEL), _row_spec(D_MODEL), _row_spec(LANES),
                   pl.BlockSpec((1, 1, LANES), lambda i: (i, 0, 0))],
        out_shape=[
            jax.ShapeDtypeStruct((N_TOK, D_MODEL), F32),
            jax.ShapeDtypeStruct((N_TOK, D_MODEL), BF16),
            jax.ShapeDtypeStruct((N_TOK, LANES), F32),
            jax.ShapeDtypeStruct((NB_TOK, 1, LANES), F32),
        ],
        compiler_params=_params("parallel"),
        name=f"outproj{layer}",
    )(*flat, *ws, mod, n2, mod, mod, r_hi, r_lo, r_b)


MOE_CHUNK = 16
MOE_TILE = 512
NW = N_TOK // TM
R_LOC = 2 * TM + N_EXPERTS * MOE_CHUNK
R_SORTED_MAX = 2 * N_TOK + NW * N_EXPERTS * (MOE_CHUNK - 1) + N_EXPERTS * (MOE_TILE - 1)
NT_MAX = -(-R_SORTED_MAX // MOE_TILE)


def _moe_tables(cnt):
    cnt = cnt[:, 0, :N_EXPERTS].astype(jnp.int32)
    padc = (cnt + (MOE_CHUNK - 1)) // MOE_CHUNK * MOE_CHUNK
    loff = jnp.cumsum(padc, axis=1) - padc
    tot = jnp.sum(padc, axis=0)
    seg = (tot + (MOE_TILE - 1)) // MOE_TILE * MOE_TILE
    seg_end = jnp.cumsum(seg)
    base = seg_end - seg
    goff = base[None, :] + jnp.cumsum(padc, axis=0) - padc
    tile_start = jnp.arange(NT_MAX, dtype=jnp.int32) * MOE_TILE
    texp = jnp.minimum(jnp.sum((tile_start[:, None] >= seg_end[None, :]).astype(jnp.int32), axis=1), N_EXPERTS - 1)
    nvalid = (seg_end[-1:] // MOE_TILE).astype(jnp.int32)
    i32 = lambda a: a.reshape(-1).astype(jnp.int32)
    return dict(nchunk=i32(padc // MOE_CHUNK), loff=i32(loff), goff=i32(goff), tail_start=i32(base + tot),
                tail_chunks=i32((seg - tot) // MOE_CHUNK), texp=texp, nvalid=nvalid)


def _local_positions(cw):
    n = cw.shape[0]
    a = cw > 0.0
    a_f = jnp.where(a, 1.0, 0.0)
    below = jnp.where(lax.broadcasted_iota(jnp.int32, (n, n), 1) < lax.broadcasted_iota(jnp.int32, (n, n), 0), 1.0, 0.0)
    rank = _dot(below.astype(BF16), a_f.astype(BF16))
    cnt = jnp.sum(a_f, axis=0, keepdims=True)
    padc = jnp.floor((cnt + (MOE_CHUNK - 1)) * (1.0 / MOE_CHUNK)) * MOE_CHUNK
    before = jnp.where(lax.broadcasted_iota(jnp.int32, (LANES, LANES), 0) < lax.broadcasted_iota(jnp.int32, (LANES, LANES), 1),
                       1.0, 0.0)
    loff = _dot(jnp.broadcast_to(padc, (8, LANES)).astype(BF16), before.astype(BF16))[0:1]
    lpos = rank + loff
    pa = jnp.min(jnp.where(a, lpos, float(R_LOC)), axis=-1, keepdims=True)
    pb = jnp.max(jnp.where(a, lpos, -1.0), axis=-1, keepdims=True)
    wa = jnp.sum(jnp.where(jnp.logical_and(a, lpos == pa), cw, 0.0), axis=-1, keepdims=True)
    wb = jnp.sum(jnp.where(jnp.logical_and(a, lpos == pb), cw, 0.0), axis=-1, keepdims=True)
    return pa, pb, wa, wb


def _for_each_piece(w, nchunk_ref, loff_ref, goff_ref, fn):
    def per_expert(e, total):
        n = nchunk_ref[w * N_EXPERTS + e]
        lo = loff_ref[w * N_EXPERTS + e]
        go = goff_ref[w * N_EXPERTS + e]

        def piece(c, carry):
            fn(pl.multiple_of(lo + c * MOE_CHUNK, MOE_CHUNK), pl.multiple_of(go + c * MOE_CHUNK, MOE_CHUNK))
            return carry

        lax.fori_loop(0, n, piece, 0)
        return total + n

    return lax.fori_loop(0, N_EXPERTS, per_expert, 0)


def _dispatch_kernel(nchunk_ref, loff_ref, goff_ref, tstart_ref, tchunk_ref, h_ref, cw_ref, hs_ref, buf, zbuf, sem):
    w = pl.program_id(0)
    pa, pb, _, _ = _local_positions(cw_ref[...])
    lane = lax.broadcasted_iota(jnp.int32, (TM, LANES), 1)
    pos_t = jnp.where(lane == 0, pa, jnp.where(lane == 1, pb, 0.0)).T
    r = lax.broadcasted_iota(jnp.int32, (R_LOC, TM), 0).astype(F32)
    sel = jnp.logical_or(r == pos_t[0:1, :], r == pos_t[1:2, :])
    buf[...] = _dot(jnp.where(sel, 1.0, 0.0).astype(BF16), h_ref[...]).astype(BF16)

    def piece_copy(local_row, global_row):
        return pltpu.make_async_copy(buf.at[pl.ds(local_row, MOE_CHUNK)], hs_ref.at[pl.ds(global_row, MOE_CHUNK)], sem)

    total = _for_each_piece(w, nchunk_ref, loff_ref, goff_ref, lambda lo, go: piece_copy(lo, go).start())

    def wait_piece(c, carry):
        piece_copy(0, 0).wait()
        return carry

    lax.fori_loop(0, total, wait_piece, 0)

    @pl.when(w == NW - 1)
    def _():
        zbuf[...] = jnp.zeros_like(zbuf)

        def zero_copy(global_row):
            return pltpu.make_async_copy(zbuf, hs_ref.at[pl.ds(global_row, MOE_CHUNK)], sem)

        def per_expert(e, total):
            n = tchunk_ref[e]
            st = tstart_ref[e]

            def piece(c, carry):
                zero_copy(pl.multiple_of(st + c * MOE_CHUNK, MOE_CHUNK)).start()
                return carry

            lax.fori_loop(0, n, piece, 0)
            return total + n

        total_z = lax.fori_loop(0, N_EXPERTS, per_expert, 0)

        def wait_zero(c, carry):
            zero_copy(0).wait()
            return carry

        lax.fori_loop(0, total_z, wait_zero, 0)


def _dispatch_call(layer, h, cw, tb):
    return pl.pallas_call(
        _dispatch_kernel,
        grid_spec=pltpu.PrefetchScalarGridSpec(
            num_scalar_prefetch=5,
            grid=(NW,),
            in_specs=[_row_spec(D_MODEL), _row_spec(LANES)],
            out_specs=pl.BlockSpec(memory_space=pl.ANY),
            scratch_shapes=[pltpu.VMEM((R_LOC, D_MODEL), BF16), pltpu.VMEM((MOE_CHUNK, D_MODEL), BF16),
                            pltpu.SemaphoreType.DMA],
        ),
        out_shape=jax.ShapeDtypeStruct((NT_MAX * MOE_TILE, D_MODEL), BF16),
        compiler_params=_params("arbitrary"),
        name=f"moe_dispatch{layer}",
    )(tb["nchunk"], tb["loff"], tb["goff"], tb["tail_start"], tb["tail_chunks"], h, cw)


def _expert_kernel(texp_ref, nvalid_ref, hs_ref, wg_ref, wu_ref, wd_ref, ys_ref, wg_b, wu_b, wd_b):
    i = pl.program_id(0)
    fresh = jnp.logical_or(i == 0, texp_ref[i] != texp_ref[jnp.maximum(i - 1, 0)])

    @pl.when(fresh)
    def _():
        wg_b[...] = wg_ref[0].astype(BF16)
        wu_b[...] = wu_ref[0].astype(BF16)
        wd_b[...] = wd_ref[0].astype(BF16)

    @pl.when(i < nvalid_ref[0])
    def _():
        h = hs_ref[...]
        a = _dot(h, wg_b[...])
        b = _dot(h, wu_b[...])
        ys_ref[...] = _dot((a * _sigmoid(a) * b).astype(BF16), wd_b[...]).astype(BF16)


def _expert_call(layer, hs, tb, w_gate, w_up, w_down):
    tile_map = lambda i, texp, nv: (jnp.minimum(i, nv[0] - 1), 0)
    w_map = lambda i, texp, nv: (layer, texp[i], 0, 0)
    return pl.pallas_call(
        _expert_kernel,
        grid_spec=pltpu.PrefetchScalarGridSpec(
            num_scalar_prefetch=2,
            grid=(NT_MAX,),
            in_specs=[
                pl.BlockSpec((MOE_TILE, D_MODEL), tile_map),
                pl.BlockSpec((None, 1, D_MODEL, EXPERT_FF), w_map),
                pl.BlockSpec((None, 1, D_MODEL, EXPERT_FF), w_map),
                pl.BlockSpec((None, 1, EXPERT_FF, D_MODEL), w_map),
            ],
            out_specs=pl.BlockSpec((MOE_TILE, D_MODEL), tile_map),
            scratch_shapes=[pltpu.VMEM((D_MODEL, EXPERT_FF), BF16), pltpu.VMEM((D_MODEL, EXPERT_FF), BF16),
                            pltpu.VMEM((EXPERT_FF, D_MODEL), BF16)],
        ),
        out_shape=jax.ShapeDtypeStruct((NT_MAX * MOE_TILE, D_MODEL), BF16),
        compiler_params=_params("arbitrary"),
        name=f"moe_experts{layer}",
    )(tb["texp"], tb["nvalid"], hs, w_gate, w_up, w_down)


def _combine_kernel(split_out, nchunk_ref, loff_ref, goff_ref, cw_ref, x_ref, g2_ref, ys_ref, *refs):
    o_refs, (buf, sem) = refs[:-2], refs[-2:]
    w = pl.program_id(0)

    @pl.when(w == 0)
    def _():
        buf[...] = jnp.zeros_like(buf)

    def piece_copy(local_row, global_row):
        return pltpu.make_async_copy(ys_ref.at[pl.ds(global_row, MOE_CHUNK)], buf.at[pl.ds(local_row, MOE_CHUNK)], sem)

    total = _for_each_piece(w, nchunk_ref, loff_ref, goff_ref, lambda lo, go: piece_copy(lo, go).start())
    pa, pb, wa, wb = _local_positions(cw_ref[...])
    r = lax.broadcasted_iota(jnp.int32, (TM, R_LOC), 1).astype(F32)
    wt = jnp.where(r == pa, wa, jnp.where(r == pb, wb, 0.0)).astype(BF16)

    def wait_piece(c, carry):
        piece_copy(0, 0).wait()
        return carry

    lax.fori_loop(0, total, wait_piece, 0)
    y = x_ref[...] + g2_ref[0] * _dot(wt, buf[...])
    if split_out:
        @pl.when(w < NB_CTX)
        def _():
            o_refs[0][...] = y

        @pl.when(w >= NB_CTX)
        def _():
            o_refs[1][...] = y
    else:
        o_refs[0][...] = y


def _combine_call(layer, cw, x, mod, ys, tb, split_out):
    if split_out:
        out_specs = _split_specs(D_MODEL)
        out_shape = [jax.ShapeDtypeStruct((N_CTX, D_MODEL), F32), jax.ShapeDtypeStruct((N_LAT, D_MODEL), F32)]
    else:
        out_specs = [_row_spec(D_MODEL)]
        out_shape = [jax.ShapeDtypeStruct((N_TOK, D_MODEL), F32)]
    return pl.pallas_call(
        functools.partial(_combine_kernel, split_out),
        grid_spec=pltpu.PrefetchScalarGridSpec(
            num_scalar_prefetch=3,
            grid=(NW,),
            in_specs=[_row_spec(LANES), _row_spec(D_MODEL), _mod_spec(layer, 5), pl.BlockSpec(memory_space=pl.ANY)],
            out_specs=out_specs,
            scratch_shapes=[pltpu.VMEM((R_LOC, D_MODEL), BF16), pltpu.SemaphoreType.DMA],
        ),
        out_shape=out_shape,
        compiler_params=_params("arbitrary"),
        name=f"moe_combine{layer}",
    )(tb["nchunk"], tb["loff"], tb["goff"], cw, x, mod, ys)


def _moe_call(layer, h, cw, cnt, w_gate, w_up, w_down, x, mod, split_out=False):
    tb = _moe_tables(cnt)
    hs = _dispatch_call(layer, h, cw, tb)
    ys = _expert_call(layer, hs, tb, w_gate, w_up, w_down)
    return _combine_call(layer, cw, x, mod, ys, tb, split_out)


def _group_rms(x, g_ref, ones_blocks, n):
    ss = _dot((x * x).astype(BF16), ones_blocks)
    return x * lax.rsqrt(ss * (1.0 / n) + EPS) * g_ref[...]


def _inproj1_kernel(x_ref, g_ref, sh_ref, sc_ref, w_ref, qg_ref, kg_ref, c_ref, s1_ref, s2_ref,
                    q_ref, k_ref, v_ref, kn_ref, vf_ref):
    i = pl.program_id(0)
    h = (_rms(x_ref[...], g_ref[...]) * (1.0 + sc_ref[0]) + sh_ref[0]).astype(BF16)
    hw = DIFF_HEADS * DIFF_DV
    half = DIFF_DK // 2
    ones_blocks = _ones_blocks(DIFF_DK.bit_length() - 1)
    for tp in range(DIFF_HEADS // 2):
        pair = slice(tp * 2 * LANES, (tp + 1) * 2 * LANES)
        pq = _dot(h, w_ref[:, pair])
        pk = _dot(h, w_ref[:, hw + tp * 2 * LANES:hw + (tp + 1) * 2 * LANES])
        for u in range(2):
            sl = slice((2 * tp + u) * LANES, (2 * tp + u + 1) * LANES)
            qt = _group_rms(pq[:, u * LANES:(u + 1) * LANES], qg_ref, ones_blocks, DIFF_DK)
            q_ref[:, sl] = _rope_tile(qt, c_ref, s1_ref, s2_ref, half).astype(BF16)
            kt = _group_rms(pk[:, u * LANES:(u + 1) * LANES], kg_ref, ones_blocks, DIFF_DK)
            k_ref[:, sl] = _rope_tile(kt, c_ref, s1_ref, s2_ref, half).astype(BF16)

            @pl.when(i < NB_CTX)
            def _():
                kn_ref[:, sl] = kt

    v = _dot(h, w_ref[:, 2 * hw:])
    v_ref[...] = v.astype(BF16)

    @pl.when(i < NB_CTX)
    def _():
        vf_ref[...] = v


def _inproj1_call(x, mod, g1, w_qkv, qg, kg, tables):
    hw = DIFF_HEADS * DIFF_DV
    tspec = pl.BlockSpec((TM, LANES), lambda i: (_rope_block(i), 0))
    ctx_spec = pl.BlockSpec((TM, hw), lambda i: (jnp.minimum(i, NB_CTX - 1), 0))
    return pl.pallas_call(
        _inproj1_kernel,
        grid=(NB_TOK,),
        in_specs=[
            _row_spec(D_MODEL), _full_spec((1, D_MODEL)), _mod_spec(1, 0), _mod_spec(1, 1),
            _full_spec((D_MODEL, 3 * hw)), _full_spec((1, LANES)), _full_spec((1, LANES)), tspec, tspec, tspec,
        ],
        out_specs=[_row_spec(hw), _row_spec(hw), _row_spec(hw), ctx_spec, ctx_spec],
        out_shape=[
            jax.ShapeDtypeStruct((N_TOK, hw), BF16),
            jax.ShapeDtypeStruct((N_TOK, hw), BF16),
            jax.ShapeDtypeStruct((N_TOK, hw), BF16),
            jax.ShapeDtypeStruct((N_CTX, hw), F32),
            jax.ShapeDtypeStruct((N_CTX, hw), F32),
        ],
        compiler_params=_params("arbitrary"),
        name="inproj1",
    )(x, g1, mod, mod, w_qkv, qg, kg, *tables)


def _diff_attn_kernel(has_cache, lambda_init, tq, qg_ref, kg_ref, *refs):
    if has_cache:
        q_ref, k_ref, v_ref, kc_ref, vc_ref, lq1, lk1, lq2, lk2, sg_ref, o_ref = refs
        k_refs, v_refs = [k_ref, kc_ref], [v_ref, vc_ref]
    else:
        q_ref, k_ref, v_ref, lq1, lk1, lq2, lk2, sg_ref, o_ref = refs
        k_refs, v_refs = [k_ref], [v_ref]
    lam = (jnp.exp(jnp.sum(lq1[...] * lk1[...], axis=-1, keepdims=True))
           - jnp.exp(jnp.sum(lq2[...] * lk2[...], axis=-1, keepdims=True)) + lambda_init)
    lo = lax.broadcasted_iota(jnp.int32, (1, LANES), 1) < DIFF_DK
    k_sq = _gain_sq_bound(kg_ref, DIFF_DK)
    if has_cache:
        for mask in (lo, jnp.logical_not(lo)):
            k_sq = jnp.maximum(k_sq, _max_sq_norm(kc_ref[...], mask))
    bound_sq = _gain_sq_bound(qg_ref, DIFF_DK) * k_sq

    def block(shifted, qi, carry):
        r0 = pl.multiple_of(qi * tq, tq)
        q = q_ref[pl.ds(r0, tq), :]
        zero = jnp.zeros_like(q)
        e0, l0 = _exps([_dot_nt(jnp.where(lo, q, zero), kr[...]) for kr in k_refs], shifted)
        e1, l1 = _exps([_dot_nt(jnp.where(lo, zero, q), kr[...]) for kr in k_refs], shifted)
        f0 = 1.0 / l0
        f1 = lam / l1
        o = None
        for a, b, vr in zip(e0, e1, v_refs):
            t = _dot((a * f0 - b * f1).astype(BF16), vr[...])
            o = t if o is None else o + t
        o = o * lax.rsqrt(jnp.mean(o * o, axis=-1, keepdims=True) + EPS) * sg_ref[...] * (1.0 - lambda_init)
        o_ref[pl.ds(r0, tq), :] = o.astype(BF16)
        return carry

    _run_query_blocks(bound_sq, q_ref.shape[0] // tq, block)


def _diff_attn_ctx_call(lambda_init, gains, q, k, v, lams, sg):
    lspec = _full_spec((1, DIFF_DK))
    gspec = _full_spec((1, LANES))
    return pl.pallas_call(
        functools.partial(_diff_attn_kernel, False, lambda_init, SEQ),
        grid=(BATCH, DIFF_HEADS),
        in_specs=[gspec, gspec] + [pl.BlockSpec((SEQ, LANES), lambda b, h: (b, h))] * 3 + [lspec] * 4
        + [_full_spec((1, DIFF_DV))],
        out_specs=pl.BlockSpec((SEQ, LANES), lambda b, h: (b, h)),
        out_shape=jax.ShapeDtypeStruct((N_CTX, DIFF_HEADS * DIFF_DV), BF16),
        compiler_params=_params("parallel", "parallel"),
        name="diff_attn_ctx",
    )(*gains, q, k, v, *lams, sg)


def _diff_attn_lat_call(lambda_init, gains, q, k, v, kc, vc, lams, sg):
    k0 = N_CTX // DEC_SEQ
    lspec = _full_spec((1, DIFF_DK))
    gspec = _full_spec((1, LANES))
    seq_spec = pl.BlockSpec((DEC_SEQ, LANES), lambda b, h: (k0 + b, h))
    c_spec = pl.BlockSpec((PAST_LEN, LANES), lambda b, h: (b, h))
    return pl.pallas_call(
        functools.partial(_diff_attn_kernel, True, lambda_init, TQ),
        grid=(DEC_BATCH, DIFF_HEADS),
        in_specs=[gspec, gspec, seq_spec, seq_spec, seq_spec, c_spec, c_spec] + [lspec] * 4 + [_full_spec((1, DIFF_DV))],
        out_specs=pl.BlockSpec((DEC_SEQ, LANES), lambda b, h: (b, h)),
        out_shape=jax.ShapeDtypeStruct((N_LAT, DIFF_HEADS * DIFF_DV), BF16),
        compiler_params=_params("parallel", "parallel"),
        name="diff_attn_lat",
    )(*gains, q, k, v, kc, vc, *lams, sg)


def _router_weights(we, wg, be, bg):
    pad = LANES - N_EXPERTS - N_GROUPS
    w = jnp.concatenate([we, wg, jnp.zeros((D_MODEL, pad), F32)], axis=1)
    b = jnp.concatenate([be, bg, jnp.zeros((pad,), F32)])[None]
    hi = w.astype(BF16)
    lo = (w - hi.astype(F32)).astype(BF16)
    return hi, lo, b


def kernel(x_prompt, x_sample, cache_mla_ckv, cache_mla_krope, cache_diff_k, cache_diff_v, c, c_ctx, ada_w, ada_b, norm1_g, norm2_g, ev_w_in, ev_conv_w, ev_conv_b, ev_conv_ln_g, ev_conv_ln_b, ev_q_norm_g, ev_w_qb, ev_kv_norm_g, ev_w_kvb, ev_qn_g, ev_kn_g, ev_w_o, od_w_qkv, od_qn_g, od_kn_g, od_lambda_q1, od_lambda_k1, od_lambda_q2, od_lambda_k2, od_subln_g, od_w_o, moe_wg, moe_bg, moe_we, moe_be, moe_w_gate, moe_w_up, moe_w_down):
    x_in = (x_prompt.reshape(N_CTX, D_MODEL), x_sample.reshape(N_LAT, D_MODEL))
    cond = jnp.concatenate([c_ctx[None, :], c, jnp.zeros((MOD_ROWS - 1 - DEC_BATCH, D_MODEL), F32)], axis=0)
    mod = _mod_call(cond, ada_w, ada_b).reshape(-1, 1, D_MODEL)

    o3 = 2 * CONV_CH + MLA_Q_RANK + MLA_KV_RANK
    w_in = jnp.concatenate([ev_w_in[0][:, :o3], jnp.zeros((D_MODEL, MLA_NOPE), F32), ev_w_in[0][:, o3:],
                            jnp.zeros((D_MODEL, LANES - MLA_QK), F32)], axis=1).astype(BF16)
    u, qn, ckv, kr = _inproj0_call(*x_in, mod, norm1_g[0][None], w_in, ev_q_norm_g[0][None], ev_kv_norm_g[0][None])
    conv = _conv_call(u, ev_conv_w[0], ev_conv_b[0][None], ev_conv_ln_g[0][None], ev_conv_ln_b[0][None])

    pad_qk = ((0, 0), (0, 0), (0, LANES - MLA_QK))
    wq = jnp.pad(ev_w_qb[0].reshape(MLA_Q_RANK, MLA_HEADS, MLA_QK), pad_qk).reshape(MLA_Q_RANK, -1).astype(BF16)
    wkv = ev_w_kvb[0].reshape(MLA_KV_RANK, MLA_HEADS, MLA_NOPE + MLA_V)
    wk = jnp.pad(wkv[..., :MLA_NOPE], ((0, 0), (0, 0), (0, LANES - MLA_NOPE))).reshape(MLA_KV_RANK, -1).astype(BF16)
    wv = wkv[..., MLA_NOPE:].reshape(MLA_KV_RANK, -1).astype(BF16)
    qg = jnp.pad(ev_qn_g[0] * (MLA_QK ** -0.5 * LOG2E), (0, LANES - MLA_QK))[None]
    kg = jnp.pad(ev_kn_g[0], (0, LANES - MLA_QK))[None]
    tables = _rope_tables(MLA_ROPE, (MLA_NOPE,), DEC_SEQ)
    q0, k0, v0 = _mla_prep_call(qn, ckv, kr, wq, wk, wv, qg, kg, tables, _rope_block)
    kr_cache = jnp.pad(cache_mla_krope[:, 0].reshape(N_CACHE, MLA_ROPE), ((0, 0), (MLA_NOPE, LANES - MLA_QK)))
    _, kc0, vc0 = _mla_prep_call(jnp.zeros((N_CACHE, MLA_Q_RANK), BF16), cache_mla_ckv[:, 0].reshape(N_CACHE, MLA_KV_RANK),
                                 kr_cache, wq, wk, wv, qg, kg, tables, lambda i: BLK_PER_LAT)
    attn = (_mla_attn_ctx_call(qg, kg, q0, k0, v0), _mla_attn_lat_call(qg, kg, q0, k0, v0, kc0, vc0))

    w_o = ev_w_o[0].astype(BF16)
    r_hi, r_lo, r_b = _router_weights(moe_we[0], moe_wg[0], moe_be[0], moe_bg[0])
    x, h2, cw, cnt = _outproj_call(0, [(conv,), attn], [w_o[:CONV_CH], w_o[CONV_CH:]], x_in, mod, norm2_g[0][None], r_hi, r_lo, r_b)
    (x,) = _moe_call(0, h2, cw, cnt, moe_w_gate, moe_w_up, moe_w_down, x, mod)

    lambda_init = 0.8 - 0.6 * math.exp(-0.3 * 1)
    qg1 = jnp.tile(od_qn_g[0] * (DIFF_DK ** -0.5 * LOG2E), 2)[None]
    kg1 = jnp.tile(od_kn_g[0], 2)[None]
    tables1 = _rope_tables(DIFF_DK, (0, DIFF_DK), DEC_SEQ)
    q1, k1, v1, kn1, vf1 = _inproj1_call(x, mod, norm1_g[1][None], od_w_qkv[0].astype(BF16), qg1, kg1, tables1)
    lams = [od_lambda_q1[0][None], od_lambda_k1[0][None], od_lambda_q2[0][None], od_lambda_k2[0][None]]
    sg = od_subln_g[0][None]
    hw = DIFF_HEADS * DIFF_DV
    kc1 = cache_diff_k[:, 0].reshape(N_CACHE, hw).astype(BF16)
    vc1 = cache_diff_v[:, 0].reshape(N_CACHE, hw).astype(BF16)
    attn1 = (_diff_attn_ctx_call(lambda_init, (qg1, kg1), q1, k1, v1, lams, sg),
             _diff_attn_lat_call(lambda_init, (qg1, kg1), q1, k1, v1, kc1, vc1, lams, sg))
    r_hi, r_lo, r_b = _router_weights(moe_we[1], moe_wg[1], moe_be[1], moe_bg[1])
    x, h2, cw, cnt = _outproj_call(1, [attn1], [od_w_o[0].astype(BF16)], (x,), mod, norm2_g[1][None], r_hi, r_lo, r_b)
    y_ctx, y_lat = _moe_call(1, h2, cw, cnt, moe_w_gate, moe_w_up, moe_w_down, x, mod, split_out=True)

    y_prompt = y_ctx.reshape(BATCH, SEQ, D_MODEL)
    y_sample = y_lat.reshape(DEC_BATCH, DEC_SEQ, D_MODEL)
    new_mla_ckv = ckv[:N_CTX].reshape(BATCH, 1, SEQ, MLA_KV_RANK)
    new_mla_krope = kr[:N_CTX, MLA_NOPE:MLA_QK].reshape(BATCH, 1, SEQ, MLA_ROPE)
    new_diff_k = kn1.reshape(BATCH, 1, SEQ, DIFF_HEADS, 2, DIFF_DK)
    new_diff_v = vf1.reshape(BATCH, 1, SEQ, DIFF_HEADS, DIFF_DV)
    return (y_prompt, y_sample, new_mla_ckv, new_mla_krope, new_diff_k, new_diff_v)
```

```python
import functools
import math

import jax
import jax.numpy as jnp
from jax import lax
from jax.experimental import pallas as pl
from jax.experimental.pallas import tpu as pltpu

F32 = jnp.float32
BF16 = jnp.bfloat16

D_MODEL = 1024
BATCH = 16
SEQ = 256
DEC_BATCH = 8
DEC_SEQ = 2048
PAST_LEN = 256
GRID_W = 64
ROPE_BASE = 10000.0
EPS = 1e-6

CONV_CH = 512
CONV_WIDTH = 31
CONV_PAD = 15
MLA_HEADS = 8
MLA_NOPE = 64
MLA_ROPE = 32
MLA_QK = 96
MLA_V = 64
MLA_Q_RANK = 384
MLA_KV_RANK = 256
DIFF_HEADS = 8
DIFF_DK = 64
DIFF_DV = 128
N_GROUPS = 4
EXPERTS_PER_GROUP = 8
N_EXPERTS = 32
EXPERT_FF = 256

LANES = 128
SUBLANES = 8
N_CTX = BATCH * SEQ
N_LAT = DEC_BATCH * DEC_SEQ
N_TOK = N_CTX + N_LAT
N_CACHE = DEC_BATCH * PAST_LEN
MOD_ROWS = 16
VMEM_LIMIT = 48 * 1024 * 1024

TM = 512
NB_CTX = N_CTX // TM
NB_TOK = N_TOK // TM
BLK_PER_LAT = DEC_SEQ // TM
TQ = 256
CONV_T = 256
CONV_HALO = 16
CONV_RB = 32
CONV_SHIFT_ROWS = CONV_T + 2 * CONV_HALO - SUBLANES


def _params(*sem):
    return pltpu.CompilerParams(dimension_semantics=sem, vmem_limit_bytes=VMEM_LIMIT)


def _sigmoid(x):
    return 1.0 / (1.0 + jnp.exp(-x))


def _rms(x, g):
    return x * lax.rsqrt(jnp.mean(x * x, axis=-1, keepdims=True) + EPS) * g


def _dot(a, b):
    return jnp.dot(a, b, preferred_element_type=F32)


def _dot_nt(a, b):
    return lax.dot_general(a, b, (((1,), (1,)), ((), ())), preferred_element_type=F32)


def _seq_of_block(i):
    return jnp.where(i < NB_CTX, 0, 1 + (i - NB_CTX) // BLK_PER_LAT)


def _mod_spec(layer, slot):
    return pl.BlockSpec((1, 1, D_MODEL), lambda i, *_: ((layer * MOD_ROWS + _seq_of_block(i)) * 6 + slot, 0, 0))


def _row_spec(width, tm=TM):
    return pl.BlockSpec((tm, width), lambda i, *_: (i, 0))


def _ctx_spec(width):
    return pl.BlockSpec((TM, width), lambda i, *_: (jnp.minimum(i, NB_CTX - 1), 0))


def _lat_spec(width):
    return pl.BlockSpec((TM, width), lambda i, *_: (jnp.maximum(i - NB_CTX, 0), 0))


def _split_specs(width):
    return [_ctx_spec(width), _lat_spec(width)]


def _pick(ctx_ref, lat_ref):
    return jnp.where(pl.program_id(0) < NB_CTX, ctx_ref[...], lat_ref[...])


def _ones_blocks(group_log2):
    r = lax.broadcasted_iota(jnp.int32, (LANES, LANES), 0) >> group_log2
    c = lax.broadcasted_iota(jnp.int32, (LANES, LANES), 1) >> group_log2
    return jnp.where(r == c, 1.0, 0.0).astype(BF16)


def _full_spec(shape):
    nd = len(shape)
    return pl.BlockSpec(shape, lambda *_: (0,) * nd)


def _mod_kernel(c_ref, w_ref, b_ref, o_ref):
    c = c_ref[...]
    s = (c * _sigmoid(c)).astype(BF16)
    o_ref[0] = _dot(s, w_ref[0].astype(BF16)) + b_ref[0]


def _mod_call(cond, ada_w, ada_b):
    depth = ada_w.shape[0]
    tn = 1024
    return pl.pallas_call(
        _mod_kernel,
        grid=(depth, 6 * D_MODEL // tn),
        in_specs=[
            pl.BlockSpec((MOD_ROWS, D_MODEL), lambda l, j: (0, 0)),
            pl.BlockSpec((1, D_MODEL, tn), lambda l, j: (l, 0, j)),
            pl.BlockSpec((1, 1, tn), lambda l, j: (l, 0, j)),
        ],
        out_specs=pl.BlockSpec((1, MOD_ROWS, tn), lambda l, j: (l, 0, j)),
        out_shape=jax.ShapeDtypeStruct((depth, MOD_ROWS, 6 * D_MODEL), F32),
        compiler_params=_params("parallel", "parallel"),
        name="adaln_mod",
    )(cond, ada_w, ada_b.reshape(depth, 1, 6 * D_MODEL))


EV_W = 1792


def _inproj0_kernel(xc_ref, xl_ref, g_ref, sh_ref, sc_ref, w_ref, qg_ref, kvg_ref, u_ref, qn_ref, ckv_ref, kr_ref):
    h = _rms(_pick(xc_ref, xl_ref), g_ref[...]) * (1.0 + sc_ref[0]) + sh_ref[0]
    p = _dot(h.astype(BF16), w_ref[...])
    u_ref[...] = p[:, :CONV_CH] * _sigmoid(p[:, CONV_CH:2 * CONV_CH])
    o1 = 2 * CONV_CH
    o2 = o1 + MLA_Q_RANK
    o3 = o2 + MLA_KV_RANK
    qn_ref[...] = _rms(p[:, o1:o2], qg_ref[...]).astype(BF16)
    ckv_ref[...] = _rms(p[:, o2:o3], kvg_ref[...])
    kr_ref[...] = p[:, o3:o3 + LANES]


def _inproj0_call(x_ctx, x_lat, mod, g1, w_in, qg, kvg):
    return pl.pallas_call(
        _inproj0_kernel,
        grid=(NB_TOK,),
        in_specs=_split_specs(D_MODEL) + [
            _full_spec((1, D_MODEL)), _mod_spec(0, 0), _mod_spec(0, 1),
            _full_spec((D_MODEL, EV_W)), _full_spec((1, MLA_Q_RANK)), _full_spec((1, MLA_KV_RANK)),
        ],
        out_specs=[_row_spec(CONV_CH), _row_spec(MLA_Q_RANK), _row_spec(MLA_KV_RANK), _row_spec(LANES)],
        out_shape=[
            jax.ShapeDtypeStruct((N_TOK, CONV_CH), F32),
            jax.ShapeDtypeStruct((N_TOK, MLA_Q_RANK), BF16),
            jax.ShapeDtypeStruct((N_TOK, MLA_KV_RANK), F32),
            jax.ShapeDtypeStruct((N_TOK, LANES), F32),
        ],
        compiler_params=_params("parallel"),
        name="inproj0",
    )(x_ctx, x_lat, g1, mod, mod, w_in, qg, kvg)


CONV_CHUNKS_PER_LAT = DEC_SEQ // CONV_T
CONV_NB_CTX = N_CTX // CONV_T
CONV_NB = N_TOK // CONV_T


def _conv_kernel(prev_ref, cur_ref, nxt_ref, w_ref, b_ref, lg_ref, lb_ref, o_ref, buf, shifted):
    i = pl.program_id(0)
    j = (i - CONV_NB_CTX) % CONV_CHUNKS_PER_LAT
    first = jnp.logical_or(i < CONV_NB_CTX, j == 0)
    last = jnp.logical_or(i < CONV_NB_CTX, j == CONV_CHUNKS_PER_LAT - 1)
    buf[0:CONV_HALO, :] = jnp.where(first, 0.0, prev_ref[...])
    buf[CONV_HALO:CONV_HALO + CONV_T, :] = cur_ref[...]
    buf[CONV_HALO + CONV_T:, :] = jnp.where(last, 0.0, nxt_ref[...])
    for s in range(1, SUBLANES):
        shifted[s - 1] = buf[s:s + CONV_SHIFT_ROWS, :]
    for rb in range(CONV_T // CONV_RB):
        acc = jnp.zeros((CONV_RB, CONV_CH), F32)
        for k in range(CONV_WIDTH):
            row = rb * CONV_RB + CONV_HALO - CONV_PAD + k
            s, r0 = row % SUBLANES, row - row % SUBLANES
            src = buf[r0:r0 + CONV_RB, :] if s == 0 else shifted[s - 1, r0:r0 + CONV_RB, :]
            acc = acc + w_ref[k:k + 1, :] * src
        y = acc + b_ref[...]
        mu = jnp.mean(y, axis=-1, keepdims=True)
        yc = y - mu
        var = jnp.mean(yc * yc, axis=-1, keepdims=True)
        z = yc * lax.rsqrt(var + EPS) * lg_ref[...] + lb_ref[...]
        o_ref[rb * CONV_RB:(rb + 1) * CONV_RB, :] = (z * _sigmoid(z)).astype(BF16)


def _conv_call(u, w, b, lg, lb):
    hb = CONV_T // CONV_HALO
    n_halo_blocks = N_TOK // CONV_HALO
    return pl.pallas_call(
        _conv_kernel,
        grid=(CONV_NB,),
        in_specs=[
            pl.BlockSpec((CONV_HALO, CONV_CH), lambda i: (jnp.maximum(i * hb - 1, 0), 0)),
            pl.BlockSpec((CONV_T, CONV_CH), lambda i: (i, 0)),
            pl.BlockSpec((CONV_HALO, CONV_CH), lambda i: (jnp.minimum((i + 1) * hb, n_halo_blocks - 1), 0)),
            _full_spec((CONV_WIDTH, CONV_CH)), _full_spec((1, CONV_CH)),
            _full_spec((1, CONV_CH)), _full_spec((1, CONV_CH)),
        ],
        out_specs=pl.BlockSpec((CONV_T, CONV_CH), lambda i: (i, 0)),
        out_shape=jax.ShapeDtypeStruct((N_TOK, CONV_CH), BF16),
        scratch_shapes=[pltpu.VMEM((CONV_T + 2 * CONV_HALO, CONV_CH), F32),
                        pltpu.VMEM((SUBLANES - 1, CONV_SHIFT_ROWS, CONV_CH), F32)],
        compiler_params=_params("parallel"),
        name="conformer_conv",
    )(u, u, u, w, b, lg, lb)


def _rope_tile(x, c_ref, s1_ref, s2_ref, half):
    up = pltpu.roll(x, LANES - half, 1)
    dn = pltpu.roll(x, half, 1)
    return x * c_ref[...] + up * s1_ref[...] + dn * s2_ref[...]


def _rope_tables(rot_dim, lane_starts, n_tok):
    rows = n_tok // GRID_W
    row_ids = jnp.repeat(jnp.arange(rows, dtype=F32), GRID_W)
    col_ids = jnp.tile(jnp.arange(GRID_W, dtype=F32), rows)
    quarter = rot_dim // 4
    half = rot_dim // 2
    freqs = jnp.power(ROPE_BASE, -jnp.arange(quarter, dtype=F32) / quarter)
    ang = jnp.concatenate([row_ids[:, None] * freqs, col_ids[:, None] * freqs], axis=-1)
    cos, sin = jnp.cos(ang), jnp.sin(ang)
    one = lambda n: jnp.ones((n_tok, n), F32)
    zero = lambda n: jnp.zeros((n_tok, n), F32)
    c, s1, s2, at = [], [], [], 0
    for st in lane_starts:
        c += [one(st - at), cos, cos]
        s1 += [zero(st - at), -sin, zero(half)]
        s2 += [zero(st - at), zero(half), sin]
        at = st + rot_dim
    c, s1, s2 = (jnp.concatenate(t + [fill(LANES - at)], axis=1) for t, fill in ((c, one), (s1, zero), (s2, zero)))
    ident = (jnp.ones((TM, LANES), F32), jnp.zeros((TM, LANES), F32), jnp.zeros((TM, LANES), F32))
    return tuple(jnp.concatenate([t, e], axis=0) for t, e in zip((c, s1, s2), ident))


def _rope_block(i):
    return jnp.where(i < NB_CTX, BLK_PER_LAT, (i - NB_CTX) % BLK_PER_LAT)


def _mla_prep_kernel(qn_ref, ckv_ref, kr_ref, wq_ref, wk_ref, wv_ref, qg_ref, kg_ref, c_ref, s1_ref, s2_ref,
                     q_ref, k_ref, v_ref):
    qf = _dot(qn_ref[...], wq_ref[...])
    ckv = ckv_ref[...].astype(BF16)
    kf = _dot(ckv, wk_ref[...])
    v_ref[...] = _dot(ckv, wv_ref[...]).astype(BF16)
    kr = kr_ref[...]
    half = MLA_ROPE // 2
    ones = _ones_blocks(LANES.bit_length() - 1)
    for h in range(MLA_HEADS):
        sl = slice(h * LANES, (h + 1) * LANES)
        qh = _group_rms(qf[:, sl], qg_ref, ones, MLA_QK)
        q_ref[:, sl] = _rope_tile(qh, c_ref, s1_ref, s2_ref, half).astype(BF16)
        kh = _group_rms(kf[:, sl] + kr, kg_ref, ones, MLA_QK)
        k_ref[:, sl] = _rope_tile(kh, c_ref, s1_ref, s2_ref, half).astype(BF16)


def _mla_prep_call(qn, ckv, kr, wq, wk, wv, qg, kg, tables, rope_block_fn):
    n = qn.shape[0]
    hw = MLA_HEADS * LANES
    tspec = pl.BlockSpec((TM, LANES), lambda i: (rope_block_fn(i), 0))
    return pl.pallas_call(
        _mla_prep_kernel,
        grid=(n // TM,),
        in_specs=[
            _row_spec(MLA_Q_RANK), _row_spec(MLA_KV_RANK), _row_spec(LANES),
            _full_spec((MLA_Q_RANK, hw)), _full_spec((MLA_KV_RANK, hw)), _full_spec((MLA_KV_RANK, MLA_HEADS * MLA_V)),
            _full_spec((1, LANES)), _full_spec((1, LANES)), tspec, tspec, tspec,
        ],
        out_specs=[_row_spec(hw), _row_spec(hw), _row_spec(MLA_HEADS * MLA_V)],
        out_shape=[
            jax.ShapeDtypeStruct((n, hw), BF16),
            jax.ShapeDtypeStruct((n, hw), BF16),
            jax.ShapeDtypeStruct((n, MLA_HEADS * MLA_V), BF16),
        ],
        compiler_params=_params("parallel"),
        name="mla_prep",
    )(qn, ckv, kr, wq, wk, wv, qg, kg, *tables)


LOG2E = 1.4426950408889634
EXP2_SAFE = 100.0
BF16_NORM_MARGIN = 1.02


def _max_sq_norm(x, mask=None):
    xf = x.astype(F32)
    xx = xf * xf
    if mask is not None:
        xx = jnp.where(mask, xx, 0.0)
    return jnp.max(jnp.sum(xx, axis=-1, keepdims=True), axis=0, keepdims=True)


def _gain_sq_bound(g_ref, dims):
    g = g_ref[...]
    return jnp.max(g * g, axis=-1, keepdims=True) * (dims * BF16_NORM_MARGIN)


def _exps(s_list, shifted):
    if shifted:
        m = s_list[0].max(axis=-1, keepdims=True)
        for s in s_list[1:]:
            m = jnp.maximum(m, s.max(axis=-1, keepdims=True))
        es = [jnp.exp2(s - m) for s in s_list]
    else:
        es = [jnp.exp2(s) for s in s_list]
    l = es[0].sum(axis=-1, keepdims=True)
    for e in es[1:]:
        l = l + e.sum(axis=-1, keepdims=True)
    return es, l


def _run_query_blocks(bound_sq, n_blocks, block):
    safe = bound_sq[0, 0] <= EXP2_SAFE * EXP2_SAFE

    @pl.when(safe)
    def _():
        lax.fori_loop(0, n_blocks, functools.partial(block, False), 0)

    @pl.when(jnp.logical_not(safe))
    def _():
        lax.fori_loop(0, n_blocks, functools.partial(block, True), 0)


def _mla_attn_kernel(has_cache, tq, n_pairs, qg_ref, kg_ref, *refs):
    if has_cache:
        q_ref, k_ref, v_ref, kc_ref, vc_ref, o_ref = refs
        k_refs, v_refs = [k_ref, kc_ref], [v_ref, vc_ref]
    else:
        q_ref, k_ref, v_ref, o_ref = refs
        k_refs, v_refs = [k_ref], [v_ref]
    bound_sq = _gain_sq_bound(qg_ref, MLA_QK) * _gain_sq_bound(kg_ref, MLA_QK)

    def block(shifted, qi, carry):
        r0 = pl.multiple_of(qi * tq, tq)
        for hp in range(n_pairs):
            vsl = slice(hp * LANES, (hp + 1) * LANES)
            outs = []
            for hh in range(2):
                sl = slice((2 * hp + hh) * LANES, (2 * hp + hh + 1) * LANES)
                q = q_ref[pl.ds(r0, tq), sl]
                es, l = _exps([_dot_nt(q, kr[:, sl]) for kr in k_refs], shifted)
                o = _dot(es[0].astype(BF16), v_refs[0][:, vsl])
                for e, vr in zip(es[1:], v_refs[1:]):
                    o = o + _dot(e.astype(BF16), vr[:, vsl])
                outs.append(o / l)
            lane = lax.broadcasted_iota(jnp.int32, outs[0].shape, 1)
            o_ref[pl.ds(r0, tq), vsl] = jnp.where(lane < MLA_V, outs[0], outs[1]).astype(BF16)
        return carry

    _run_query_blocks(bound_sq, q_ref.shape[0] // tq, block)


def _mla_attn_ctx_call(qg, kg, q, k, v):
    return pl.pallas_call(
        functools.partial(_mla_attn_kernel, False, SEQ, MLA_HEADS // 2),
        grid=(BATCH,),
        in_specs=[
            _full_spec((1, LANES)), _full_spec((1, LANES)),
            pl.BlockSpec((SEQ, MLA_HEADS * LANES), lambda b: (b, 0)),
            pl.BlockSpec((SEQ, MLA_HEADS * LANES), lambda b: (b, 0)),
            pl.BlockSpec((SEQ, MLA_HEADS * MLA_V), lambda b: (b, 0)),
        ],
        out_specs=pl.BlockSpec((SEQ, MLA_HEADS * MLA_V), lambda b: (b, 0)),
        out_shape=jax.ShapeDtypeStruct((N_CTX, MLA_HEADS * MLA_V), BF16),
        compiler_params=_params("parallel"),
        name="mla_attn_ctx",
    )(qg, kg, q, k, v)


def _mla_attn_lat_call(qg, kg, q, k, v, kc, vc):
    k0 = N_CTX // DEC_SEQ
    return pl.pallas_call(
        functools.partial(_mla_attn_kernel, True, TQ, 1),
        grid=(DEC_BATCH, MLA_HEADS // 2),
        in_specs=[
            _full_spec((1, LANES)), _full_spec((1, LANES)),
            pl.BlockSpec((DEC_SEQ, 2 * LANES), lambda b, hp: (k0 + b, hp)),
            pl.BlockSpec((DEC_SEQ, 2 * LANES), lambda b, hp: (k0 + b, hp)),
            pl.BlockSpec((DEC_SEQ, LANES), lambda b, hp: (k0 + b, hp)),
            pl.BlockSpec((PAST_LEN, 2 * LANES), lambda b, hp: (b, hp)),
            pl.BlockSpec((PAST_LEN, LANES), lambda b, hp: (b, hp)),
        ],
        out_specs=pl.BlockSpec((DEC_SEQ, LANES), lambda b, hp: (b, hp)),
        out_shape=jax.ShapeDtypeStruct((N_LAT, MLA_HEADS * MLA_V), BF16),
        compiler_params=_params("parallel", "parallel"),
        name="mla_attn_lat",
    )(qg, kg, q, k, v, kc, vc)


def _route(logits):
    lane = lax.broadcasted_iota(jnp.int32, logits.shape, 1)
    lane_f = lane.astype(F32)
    big = float(LANES)
    neg = -jnp.inf
    gmask = jnp.logical_and(lane >= N_EXPERTS, lane < N_EXPERTS + N_GROUPS)
    gl = jnp.where(gmask, logits, neg)
    ge = jnp.exp(gl - jnp.max(gl, axis=-1, keepdims=True))
    gp = ge / jnp.sum(ge, axis=-1, keepdims=True)
    gate = jnp.max(gp, axis=-1, keepdims=True)
    gidx = jnp.min(jnp.where(jnp.logical_and(gmask, gp == gate), lane_f, big), axis=-1, keepdims=True) - N_EXPERTS
    lo = gidx * EXPERTS_PER_GROUP
    emask = jnp.logical_and(lane_f >= lo, lane_f < lo + EXPERTS_PER_GROUP)
    el = jnp.where(emask, logits, neg)
    ee = jnp.exp(el - jnp.max(el, axis=-1, keepdims=True))
    ep = jnp.where(emask, ee / jnp.sum(ee, axis=-1, keepdims=True), -1.0)
    p1 = jnp.max(ep, axis=-1, keepdims=True)
    i1 = jnp.min(jnp.where(ep == p1, lane_f, big), axis=-1, keepdims=True)
    ep2 = jnp.where(lane_f == i1, -1.0, ep)
    p2 = jnp.max(ep2, axis=-1, keepdims=True)
    i2 = jnp.min(jnp.where(ep2 == p2, lane_f, big), axis=-1, keepdims=True)
    den = p1 + p2
    return jnp.where(lane_f == i1, gate * p1 / den, 0.0) + jnp.where(lane_f == i2, gate * p2 / den, 0.0)


def _outproj_kernel(arity, *refs):
    refs = list(refs)
    rows = []
    for n in arity:
        rows.append(refs[0][...] if n == 1 else _pick(refs[0], refs[1]))
        refs = refs[n:]
    n_act = len(arity) - 1
    w_refs, refs = refs[:n_act], refs[n_act:]
    g1_ref, n2_ref, sh2_ref, sc2_ref, rh_ref, rl_ref, rb_ref, xo_ref, h_ref, cw_ref, cnt_ref = refs
    out = _dot(rows[0], w_refs[0][...])
    for a, w_ref in zip(rows[1:n_act], w_refs[1:]):
        out = out + _dot(a, w_ref[...])
    x = rows[n_act] + g1_ref[0] * out
    xo_ref[...] = x
    h = _rms(x, n2_ref[...]) * (1.0 + sc2_ref[0]) + sh2_ref[0]
    h_hi = h.astype(BF16)
    h_ref[...] = h_hi
    h_lo = (h - h_hi.astype(F32)).astype(BF16)
    logits = _dot(h_hi, rh_ref[...]) + (_dot(h_hi, rl_ref[...]) + _dot(h_lo, rh_ref[...])) + rb_ref[...]
    cw = _route(logits)
    cw_ref[...] = cw
    cnt_ref[0] = jnp.sum(jnp.where(cw > 0.0, 1.0, 0.0), axis=0, keepdims=True)


def _outproj_call(layer, acts, ws, x, mod, n2, r_hi, r_lo, r_b):
    operands = list(acts) + [x]
    row_specs = []
    for op in operands:
        row_specs += [_row_spec(op[0].shape[1])] if len(op) == 1 else _split_specs(op[0].shape[1])
    flat = [a for op in operands for a in op]
    return pl.pallas_call(
        functools.partial(_outproj_kernel, tuple(len(op) for op in operands)),
        grid=(NB_TOK,),
        in_specs=(
            row_specs + [_full_spec(w.shape) for w in ws]
            + [_mod_spec(layer, 2), _full_spec((1, D_MODEL)), _mod_spec(layer, 3),
               _mod_spec(layer, 4), _full_spec((D_MODEL, LANES)), _full_spec((D_MODEL, LANES)), _full_spec((1, LANES))]
        ),
        out_specs=[_row_spec(D_MODEL), _row_spec(D_MODEL), _row_spec(LANES),
                   pl.BlockSpec((1, 1, LANES), lambda i: (i, 0, 0))],
        out_shape=[
            jax.ShapeDtypeStruct((N_TOK, D_MODEL), F32),
            jax.ShapeDtypeStruct((N_TOK, D_MODEL), BF16),
            jax.ShapeDtypeStruct((N_TOK, LANES), F32),
            jax.ShapeDtypeStruct((NB_TOK, 1, LANES), F32),
        ],
        compiler_params=_params("parallel"),
        name=f"outproj{layer}",
    )(*flat, *ws, mod, n2, mod, mod, r_hi, r_lo, r_b)


MOE_CHUNK = 16
MOE_TILE = 512
NW = N_TOK // TM
R_LOC = 2 * TM + N_EXPERTS * MOE_CHUNK
R_SORTED_MAX = 2 * N_TOK + NW * N_EXPERTS * (MOE_CHUNK - 1) + N_EXPERTS * (MOE_TILE - 1)
NT_MAX = -(-R_SORTED_MAX // MOE_TILE)
P_MAX = R_LOC // MOE_CHUNK
PIECE_GROUP = 8
TRASH_ROW = NT_MAX * MOE_TILE
SORTED_ROWS = TRASH_ROW + PIECE_GROUP * MOE_CHUNK


def _moe_tables(cnt):
    cnt = cnt[:, 0, :N_EXPERTS].astype(jnp.int32)
    padc = (cnt + (MOE_CHUNK - 1)) // MOE_CHUNK * MOE_CHUNK
    loff = jnp.cumsum(padc, axis=1) - padc
    tot = jnp.sum(padc, axis=0)
    seg = (tot + (MOE_TILE - 1)) // MOE_TILE * MOE_TILE
    seg_end = jnp.cumsum(seg)
    base = seg_end - seg
    goff = base[None, :] + jnp.cumsum(padc, axis=0) - padc
    tile_start = jnp.arange(NT_MAX, dtype=jnp.int32) * MOE_TILE
    texp = jnp.minimum(jnp.sum((tile_start[:, None] >= seg_end[None, :]).astype(jnp.int32), axis=1), N_EXPERTS - 1)
    nvalid = (seg_end[-1:] // MOE_TILE).astype(jnp.int32)
    p_row = jnp.arange(P_MAX, dtype=jnp.int32)[None, :, None] * MOE_CHUNK
    inside = jnp.logical_and(p_row >= loff[:, None, :], p_row < (loff + padc)[:, None, :])
    grow = jnp.sum(jnp.where(inside, goff[:, None, :] + p_row - loff[:, None, :], 0), axis=-1)
    used = jnp.any(inside, axis=-1)
    filler = TRASH_ROW + (jnp.arange(P_MAX, dtype=jnp.int32) % PIECE_GROUP)[None, :] * MOE_CHUNK
    npieces = jnp.sum(padc, axis=1) // MOE_CHUNK
    i32 = lambda a: a.reshape(-1).astype(jnp.int32)
    return dict(dst=i32(jnp.where(used, grow, filler)), src=i32(jnp.where(used, grow, 0)),
                ngroups=i32((npieces + (PIECE_GROUP - 1)) // PIECE_GROUP), tail_start=i32(base + tot),
                tail_chunks=i32((seg - tot) // MOE_CHUNK), texp=texp, nvalid=nvalid)


def _local_positions(cw):
    n = cw.shape[0]
    a = cw > 0.0
    a_f = jnp.where(a, 1.0, 0.0)
    below = jnp.where(lax.broadcasted_iota(jnp.int32, (n, n), 1) < lax.broadcasted_iota(jnp.int32, (n, n), 0), 1.0, 0.0)
    rank = _dot(below.astype(BF16), a_f.astype(BF16))
    cnt = jnp.sum(a_f, axis=0, keepdims=True)
    padc = jnp.floor((cnt + (MOE_CHUNK - 1)) * (1.0 / MOE_CHUNK)) * MOE_CHUNK
    before = jnp.where(lax.broadcasted_iota(jnp.int32, (LANES, LANES), 0) < lax.broadcasted_iota(jnp.int32, (LANES, LANES), 1),
                       1.0, 0.0)
    loff = _dot(jnp.broadcast_to(padc, (8, LANES)).astype(BF16), before.astype(BF16))[0:1]
    lpos = rank + loff
    pa = jnp.min(jnp.where(a, lpos, float(R_LOC)), axis=-1, keepdims=True)
    pb = jnp.max(jnp.where(a, lpos, -1.0), axis=-1, keepdims=True)
    wa = jnp.sum(jnp.where(jnp.logical_and(a, lpos == pa), cw, 0.0), axis=-1, keepdims=True)
    wb = jnp.sum(jnp.where(jnp.logical_and(a, lpos == pb), cw, 0.0), axis=-1, keepdims=True)
    return pa, pb, wa, wb


def _for_each_piece(w, ngroups_ref, grow_ref, fn):
    n = ngroups_ref[w]

    def group(j, carry):
        for u in range(PIECE_GROUP):
            p = j * PIECE_GROUP + u
            fn(pl.multiple_of(p * MOE_CHUNK, MOE_CHUNK), pl.multiple_of(grow_ref[w * P_MAX + p], MOE_CHUNK))
        return carry

    lax.fori_loop(0, n, group, 0)
    return n


def _dispatch_kernel(ngroups_ref, dst_ref, tstart_ref, tchunk_ref, h_ref, cw_ref, hs_ref, buf, zbuf, sem):
    w = pl.program_id(0)
    pa, pb, _, _ = _local_positions(cw_ref[...])
    lane = lax.broadcasted_iota(jnp.int32, (TM, LANES), 1)
    pos_t = jnp.where(lane == 0, pa, jnp.where(lane == 1, pb, 0.0)).T
    r = lax.broadcasted_iota(jnp.int32, (R_LOC, TM), 0).astype(F32)
    sel = jnp.logical_or(r == pos_t[0:1, :], r == pos_t[1:2, :])
    buf[...] = _dot(jnp.where(sel, 1.0, 0.0).astype(BF16), h_ref[...]).astype(BF16)

    def piece_copy(local_row, global_row):
        return pltpu.make_async_copy(buf.at[pl.ds(local_row, MOE_CHUNK)], hs_ref.at[pl.ds(global_row, MOE_CHUNK)], sem)

    ngroups = _for_each_piece(w, ngroups_ref, dst_ref, lambda lo, go: piece_copy(lo, go).start())

    def wait_group(c, carry):
        pltpu.make_async_copy(buf.at[pl.ds(0, PIECE_GROUP * MOE_CHUNK)],
                              hs_ref.at[pl.ds(0, PIECE_GROUP * MOE_CHUNK)], sem).wait()
        return carry

    lax.fori_loop(0, ngroups, wait_group, 0)

    @pl.when(w == NW - 1)
    def _():
        zbuf[...] = jnp.zeros_like(zbuf)

        def zero_copy(global_row):
            return pltpu.make_async_copy(zbuf, hs_ref.at[pl.ds(global_row, MOE_CHUNK)], sem)

        def per_expert(e, total):
            n = tchunk_ref[e]
            st = tstart_ref[e]

            def piece(c, carry):
                zero_copy(pl.multiple_of(st + c * MOE_CHUNK, MOE_CHUNK)).start()
                return carry

            lax.fori_loop(0, n, piece, 0)
            return total + n

        total_z = lax.fori_loop(0, N_EXPERTS, per_expert, 0)

        def wait_zero(c, carry):
            zero_copy(0).wait()
            return carry

        lax.fori_loop(0, total_z, wait_zero, 0)


def _dispatch_call(layer, h, cw, tb):
    return pl.pallas_call(
        _dispatch_kernel,
        grid_spec=pltpu.PrefetchScalarGridSpec(
            num_scalar_prefetch=4,
            grid=(NW,),
            in_specs=[_row_spec(D_MODEL), _row_spec(LANES)],
            out_specs=pl.BlockSpec(memory_space=pl.ANY),
            scratch_shapes=[pltpu.VMEM((R_LOC, D_MODEL), BF16), pltpu.VMEM((MOE_CHUNK, D_MODEL), BF16),
                            pltpu.SemaphoreType.DMA],
        ),
        out_shape=jax.ShapeDtypeStruct((SORTED_ROWS, D_MODEL), BF16),
        compiler_params=_params("arbitrary"),
        name=f"moe_dispatch{layer}",
    )(tb["ngroups"], tb["dst"], tb["tail_start"], tb["tail_chunks"], h, cw)


def _expert_kernel(texp_ref, nvalid_ref, hs_ref, wg_ref, wu_ref, wd_ref, ys_ref, wg_b, wu_b, wd_b):
    i = pl.program_id(0)
    fresh = jnp.logical_or(i == 0, texp_ref[i] != texp_ref[jnp.maximum(i - 1, 0)])

    @pl.when(fresh)
    def _():
        wg_b[...] = wg_ref[0].astype(BF16)
        wu_b[...] = wu_ref[0].astype(BF16)
        wd_b[...] = wd_ref[0].astype(BF16)

    @pl.when(i < nvalid_ref[0])
    def _():
        h = hs_ref[...]
        a = _dot(h, wg_b[...])
        b = _dot(h, wu_b[...])
        ys_ref[...] = _dot((a * _sigmoid(a) * b).astype(BF16), wd_b[...]).astype(BF16)


def _expert_call(layer, hs, tb, w_gate, w_up, w_down):
    tile_map = lambda i, texp, nv: (jnp.minimum(i, nv[0] - 1), 0)
    w_map = lambda i, texp, nv: (layer, texp[i], 0, 0)
    return pl.pallas_call(
        _expert_kernel,
        grid_spec=pltpu.PrefetchScalarGridSpec(
            num_scalar_prefetch=2,
            grid=(NT_MAX,),
            in_specs=[
                pl.BlockSpec((MOE_TILE, D_MODEL), tile_map),
                pl.BlockSpec((None, 1, D_MODEL, EXPERT_FF), w_map),
                pl.BlockSpec((None, 1, D_MODEL, EXPERT_FF), w_map),
                pl.BlockSpec((None, 1, EXPERT_FF, D_MODEL), w_map),
            ],
            out_specs=pl.BlockSpec((MOE_TILE, D_MODEL), tile_map),
            scratch_shapes=[pltpu.VMEM((D_MODEL, EXPERT_FF), BF16), pltpu.VMEM((D_MODEL, EXPERT_FF), BF16),
                            pltpu.VMEM((EXPERT_FF, D_MODEL), BF16)],
        ),
        out_shape=jax.ShapeDtypeStruct((NT_MAX * MOE_TILE, D_MODEL), BF16),
        compiler_params=_params("arbitrary"),
        name=f"moe_experts{layer}",
    )(tb["texp"], tb["nvalid"], hs, w_gate, w_up, w_down)


def _combine_kernel(split_out, ngroups_ref, src_ref, cw_ref, x_ref, g2_ref, ys_ref, *refs):
    o_refs, (buf, sem) = refs[:-2], refs[-2:]
    w = pl.program_id(0)

    @pl.when(w == 0)
    def _():
        buf[...] = jnp.zeros_like(buf)

    def piece_copy(local_row, global_row):
        return pltpu.make_async_copy(ys_ref.at[pl.ds(global_row, MOE_CHUNK)], buf.at[pl.ds(local_row, MOE_CHUNK)], sem)

    ngroups = _for_each_piece(w, ngroups_ref, src_ref, lambda lo, go: piece_copy(lo, go).start())
    pa, pb, wa, wb = _local_positions(cw_ref[...])
    r = lax.broadcasted_iota(jnp.int32, (TM, R_LOC), 1).astype(F32)
    wt = jnp.where(r == pa, wa, jnp.where(r == pb, wb, 0.0)).astype(BF16)

    def wait_group(c, carry):
        pltpu.make_async_copy(ys_ref.at[pl.ds(0, PIECE_GROUP * MOE_CHUNK)],
                              buf.at[pl.ds(0, PIECE_GROUP * MOE_CHUNK)], sem).wait()
        return carry

    lax.fori_loop(0, ngroups, wait_group, 0)
    y = x_ref[...] + g2_ref[0] * _dot(wt, buf[...])
    if split_out:
        @pl.when(w < NB_CTX)
        def _():
            o_refs[0][...] = y

        @pl.when(w >= NB_CTX)
        def _():
            o_refs[1][...] = y
    else:
        o_refs[0][...] = y


def _combine_call(layer, cw, x, mod, ys, tb, split_out):
    if split_out:
        out_specs = _split_specs(D_MODEL)
        out_shape = [jax.ShapeDtypeStruct((N_CTX, D_MODEL), F32), jax.ShapeDtypeStruct((N_LAT, D_MODEL), F32)]
    else:
        out_specs = [_row_spec(D_MODEL)]
        out_shape = [jax.ShapeDtypeStruct((N_TOK, D_MODEL), F32)]
    return pl.pallas_call(
        functools.partial(_combine_kernel, split_out),
        grid_spec=pltpu.PrefetchScalarGridSpec(
            num_scalar_prefetch=2,
            grid=(NW,),
            in_specs=[_row_spec(LANES), _row_spec(D_MODEL), _mod_spec(layer, 5), pl.BlockSpec(memory_space=pl.ANY)],
            out_specs=out_specs,
            scratch_shapes=[pltpu.VMEM((R_LOC, D_MODEL), BF16), pltpu.SemaphoreType.DMA],
        ),
        out_shape=out_shape,
        compiler_params=_params("arbitrary"),
        name=f"moe_combine{layer}",
    )(tb["ngroups"], tb["src"], cw, x, mod, ys)


def _moe_call(layer, h, cw, cnt, w_gate, w_up, w_down, x, mod, split_out=False):
    tb = _moe_tables(cnt)
    hs = _dispatch_call(layer, h, cw, tb)
    ys = _expert_call(layer, hs, tb, w_gate, w_up, w_down)
    return _combine_call(layer, cw, x, mod, ys, tb, split_out)


def _group_rms(x, g_ref, ones_blocks, n):
    ss = _dot((x * x).astype(BF16), ones_blocks)
    return x * lax.rsqrt(ss * (1.0 / n) + EPS) * g_ref[...]


def _inproj1_kernel(x_ref, g_ref, sh_ref, sc_ref, w_ref, qg_ref, kg_ref, c_ref, s1_ref, s2_ref,
                    q_ref, k_ref, v_ref, kn_ref, vf_ref):
    i = pl.program_id(0)
    h = (_rms(x_ref[...], g_ref[...]) * (1.0 + sc_ref[0]) + sh_ref[0]).astype(BF16)
    hw = DIFF_HEADS * DIFF_DV
    half = DIFF_DK // 2
    ones_blocks = _ones_blocks(DIFF_DK.bit_length() - 1)
    for tp in range(DIFF_HEADS // 2):
        pair = slice(tp * 2 * LANES, (tp + 1) * 2 * LANES)
        pq = _dot(h, w_ref[:, pair])
        pk = _dot(h, w_ref[:, hw + tp * 2 * LANES:hw + (tp + 1) * 2 * LANES])
        for u in range(2):
            sl = slice((2 * tp + u) * LANES, (2 * tp + u + 1) * LANES)
            qt = _group_rms(pq[:, u * LANES:(u + 1) * LANES], qg_ref, ones_blocks, DIFF_DK)
            q_ref[:, sl] = _rope_tile(qt, c_ref, s1_ref, s2_ref, half).astype(BF16)
            kt = _group_rms(pk[:, u * LANES:(u + 1) * LANES], kg_ref, ones_blocks, DIFF_DK)
            k_ref[:, sl] = _rope_tile(kt, c_ref, s1_ref, s2_ref, half).astype(BF16)

            @pl.when(i < NB_CTX)
            def _():
                kn_ref[:, sl] = kt

    v = _dot(h, w_ref[:, 2 * hw:])
    v_ref[...] = v.astype(BF16)

    @pl.when(i < NB_CTX)
    def _():
        vf_ref[...] = v


def _inproj1_call(x, mod, g1, w_qkv, qg, kg, tables):
    hw = DIFF_HEADS * DIFF_DV
    tspec = pl.BlockSpec((TM, LANES), lambda i: (_rope_block(i), 0))
    ctx_spec = pl.BlockSpec((TM, hw), lambda i: (jnp.minimum(i, NB_CTX - 1), 0))
    return pl.pallas_call(
        _inproj1_kernel,
        grid=(NB_TOK,),
        in_specs=[
            _row_spec(D_MODEL), _full_spec((1, D_MODEL)), _mod_spec(1, 0), _mod_spec(1, 1),
            _full_spec((D_MODEL, 3 * hw)), _full_spec((1, LANES)), _full_spec((1, LANES)), tspec, tspec, tspec,
        ],
        out_specs=[_row_spec(hw), _row_spec(hw), _row_spec(hw), ctx_spec, ctx_spec],
        out_shape=[
            jax.ShapeDtypeStruct((N_TOK, hw), BF16),
            jax.ShapeDtypeStruct((N_TOK, hw), BF16),
            jax.ShapeDtypeStruct((N_TOK, hw), BF16),
            jax.ShapeDtypeStruct((N_CTX, hw), F32),
            jax.ShapeDtypeStruct((N_CTX, hw), F32),
        ],
        compiler_params=_params("arbitrary"),
        name="inproj1",
    )(x, g1, mod, mod, w_qkv, qg, kg, *tables)


def _diff_attn_kernel(has_cache, lambda_init, tq, n_heads, qg_ref, kg_ref, *refs):
    if has_cache:
        q_ref, k_ref, v_ref, kc_ref, vc_ref, lq1, lk1, lq2, lk2, sg_ref, o_ref = refs
        k_refs, v_refs = [k_ref, kc_ref], [v_ref, vc_ref]
    else:
        q_ref, k_ref, v_ref, lq1, lk1, lq2, lk2, sg_ref, o_ref = refs
        k_refs, v_refs = [k_ref], [v_ref]
    lam = (jnp.exp(jnp.sum(lq1[...] * lk1[...], axis=-1, keepdims=True))
           - jnp.exp(jnp.sum(lq2[...] * lk2[...], axis=-1, keepdims=True)) + lambda_init)
    lo = lax.broadcasted_iota(jnp.int32, (1, LANES), 1) < DIFF_DK
    k_sq = _gain_sq_bound(kg_ref, DIFF_DK)
    if has_cache:
        for mask in (lo, jnp.logical_not(lo)):
            k_sq = jnp.maximum(k_sq, _max_sq_norm(kc_ref[...], mask))
    bound_sq = _gain_sq_bound(qg_ref, DIFF_DK) * k_sq

    def block(shifted, qi, carry):
        r0 = pl.multiple_of(qi * tq, tq)
        for hh in range(n_heads):
            sl = slice(hh * LANES, (hh + 1) * LANES)
            q = q_ref[pl.ds(r0, tq), sl]
            zero = jnp.zeros_like(q)
            e0, l0 = _exps([_dot_nt(jnp.where(lo, q, zero), kr[:, sl]) for kr in k_refs], shifted)
            e1, l1 = _exps([_dot_nt(jnp.where(lo, zero, q), kr[:, sl]) for kr in k_refs], shifted)
            f0 = 1.0 / l0
            f1 = lam / l1
            o = None
            for a, b, vr in zip(e0, e1, v_refs):
                t = _dot((a * f0 - b * f1).astype(BF16), vr[:, sl])
                o = t if o is None else o + t
            o = o * lax.rsqrt(jnp.mean(o * o, axis=-1, keepdims=True) + EPS) * sg_ref[...] * (1.0 - lambda_init)
            o_ref[pl.ds(r0, tq), sl] = o.astype(BF16)
        return carry

    _run_query_blocks(bound_sq, q_ref.shape[0] // tq, block)


def _diff_attn_ctx_call(lambda_init, gains, q, k, v, lams, sg):
    lspec = _full_spec((1, DIFF_DK))
    gspec = _full_spec((1, LANES))
    return pl.pallas_call(
        functools.partial(_diff_attn_kernel, False, lambda_init, SEQ, DIFF_HEADS),
        grid=(BATCH,),
        in_specs=[gspec, gspec] + [pl.BlockSpec((SEQ, DIFF_HEADS * LANES), lambda b: (b, 0))] * 3 + [lspec] * 4
        + [_full_spec((1, DIFF_DV))],
        out_specs=pl.BlockSpec((SEQ, DIFF_HEADS * LANES), lambda b: (b, 0)),
        out_shape=jax.ShapeDtypeStruct((N_CTX, DIFF_HEADS * DIFF_DV), BF16),
        compiler_params=_params("parallel"),
        name="diff_attn_ctx",
    )(*gains, q, k, v, *lams, sg)


def _diff_attn_lat_call(lambda_init, gains, q, k, v, kc, vc, lams, sg):
    k0 = N_CTX // DEC_SEQ
    lspec = _full_spec((1, DIFF_DK))
    gspec = _full_spec((1, LANES))
    seq_spec = pl.BlockSpec((DEC_SEQ, LANES), lambda b, h: (k0 + b, h))
    c_spec = pl.BlockSpec((PAST_LEN, LANES), lambda b, h: (b, h))
    return pl.pallas_call(
        functools.partial(_diff_attn_kernel, True, lambda_init, TQ, 1),
        grid=(DEC_BATCH, DIFF_HEADS),
        in_specs=[gspec, gspec, seq_spec, seq_spec, seq_spec, c_spec, c_spec] + [lspec] * 4 + [_full_spec((1, DIFF_DV))],
        out_specs=pl.BlockSpec((DEC_SEQ, LANES), lambda b, h: (b, h)),
        out_shape=jax.ShapeDtypeStruct((N_LAT, DIFF_HEADS * DIFF_DV), BF16),
        compiler_params=_params("parallel", "parallel"),
        name="diff_attn_lat",
    )(*gains, q, k, v, kc, vc, *lams, sg)


def _router_weights(we, wg, be, bg):
    pad = LANES - N_EXPERTS - N_GROUPS
    w = jnp.concatenate([we, wg, jnp.zeros((D_MODEL, pad), F32)], axis=1)
    b = jnp.concatenate([be, bg, jnp.zeros((pad,), F32)])[None]
    hi = w.astype(BF16)
    lo = (w - hi.astype(F32)).astype(BF16)
    return hi, lo, b


def kernel(x_prompt, x_sample, cache_mla_ckv, cache_mla_krope, cache_diff_k, cache_diff_v, c, c_ctx, ada_w, ada_b, norm1_g, norm2_g, ev_w_in, ev_conv_w, ev_conv_b, ev_conv_ln_g, ev_conv_ln_b, ev_q_norm_g, ev_w_qb, ev_kv_norm_g, ev_w_kvb, ev_qn_g, ev_kn_g, ev_w_o, od_w_qkv, od_qn_g, od_kn_g, od_lambda_q1, od_lambda_k1, od_lambda_q2, od_lambda_k2, od_subln_g, od_w_o, moe_wg, moe_bg, moe_we, moe_be, moe_w_gate, moe_w_up, moe_w_down):
    x_in = (x_prompt.reshape(N_CTX, D_MODEL), x_sample.reshape(N_LAT, D_MODEL))
    cond = jnp.concatenate([c_ctx[None, :], c, jnp.zeros((MOD_ROWS - 1 - DEC_BATCH, D_MODEL), F32)], axis=0)
    mod = _mod_call(cond, ada_w, ada_b).reshape(-1, 1, D_MODEL)

    o3 = 2 * CONV_CH + MLA_Q_RANK + MLA_KV_RANK
    w_in = jnp.concatenate([ev_w_in[0][:, :o3], jnp.zeros((D_MODEL, MLA_NOPE), F32), ev_w_in[0][:, o3:],
                            jnp.zeros((D_MODEL, LANES - MLA_QK), F32)], axis=1).astype(BF16)
    u, qn, ckv, kr = _inproj0_call(*x_in, mod, norm1_g[0][None], w_in, ev_q_norm_g[0][None], ev_kv_norm_g[0][None])
    conv = _conv_call(u, ev_conv_w[0], ev_conv_b[0][None], ev_conv_ln_g[0][None], ev_conv_ln_b[0][None])

    pad_qk = ((0, 0), (0, 0), (0, LANES - MLA_QK))
    wq = jnp.pad(ev_w_qb[0].reshape(MLA_Q_RANK, MLA_HEADS, MLA_QK), pad_qk).reshape(MLA_Q_RANK, -1).astype(BF16)
    wkv = ev_w_kvb[0].reshape(MLA_KV_RANK, MLA_HEADS, MLA_NOPE + MLA_V)
    wk = jnp.pad(wkv[..., :MLA_NOPE], ((0, 0), (0, 0), (0, LANES - MLA_NOPE))).reshape(MLA_KV_RANK, -1).astype(BF16)
    wv = wkv[..., MLA_NOPE:].reshape(MLA_KV_RANK, -1).astype(BF16)
    qg = jnp.pad(ev_qn_g[0] * (MLA_QK ** -0.5 * LOG2E), (0, LANES - MLA_QK))[None]
    kg = jnp.pad(ev_kn_g[0], (0, LANES - MLA_QK))[None]
    tables = _rope_tables(MLA_ROPE, (MLA_NOPE,), DEC_SEQ)
    q0, k0, v0 = _mla_prep_call(qn, ckv, kr, wq, wk, wv, qg, kg, tables, _rope_block)
    kr_cache = jnp.pad(cache_mla_krope[:, 0].reshape(N_CACHE, MLA_ROPE), ((0, 0), (MLA_NOPE, LANES - MLA_QK)))
    _, kc0, vc0 = _mla_prep_call(jnp.zeros((N_CACHE, MLA_Q_RANK), BF16), cache_mla_ckv[:, 0].reshape(N_CACHE, MLA_KV_RANK),
                                 kr_cache, wq, wk, wv, qg, kg, tables, lambda i: BLK_PER_LAT)
    attn = (_mla_attn_ctx_call(qg, kg, q0, k0, v0), _mla_attn_lat_call(qg, kg, q0, k0, v0, kc0, vc0))

    w_o = ev_w_o[0].astype(BF16)
    r_hi, r_lo, r_b = _router_weights(moe_we[0], moe_wg[0], moe_be[0], moe_bg[0])
    x, h2, cw, cnt = _outproj_call(0, [(conv,), attn], [w_o[:CONV_CH], w_o[CONV_CH:]], x_in, mod, norm2_g[0][None], r_hi, r_lo, r_b)
    (x,) = _moe_call(0, h2, cw, cnt, moe_w_gate, moe_w_up, moe_w_down, x, mod)

    lambda_init = 0.8 - 0.6 * math.exp(-0.3 * 1)
    qg1 = jnp.tile(od_qn_g[0] * (DIFF_DK ** -0.5 * LOG2E), 2)[None]
    kg1 = jnp.tile(od_kn_g[0], 2)[None]
    tables1 = _rope_tables(DIFF_DK, (0, DIFF_DK), DEC_SEQ)
    q1, k1, v1, kn1, vf1 = _inproj1_call(x, mod, norm1_g[1][None], od_w_qkv[0].astype(BF16), qg1, kg1, tables1)
    lams = [od_lambda_q1[0][None], od_lambda_k1[0][None], od_lambda_q2[0][None], od_lambda_k2[0][None]]
    sg = od_subln_g[0][None]
    hw = DIFF_HEADS * DIFF_DV
    kc1 = cache_diff_k[:, 0].reshape(N_CACHE, hw).astype(BF16)
    vc1 = cache_diff_v[:, 0].reshape(N_CACHE, hw).astype(BF16)
    attn1 = (_diff_attn_ctx_call(lambda_init, (qg1, kg1), q1, k1, v1, lams, sg),
             _diff_attn_lat_call(lambda_init, (qg1, kg1), q1, k1, v1, kc1, vc1, lams, sg))
    r_hi, r_lo, r_b = _router_weights(moe_we[1], moe_wg[1], moe_be[1], moe_bg[1])
    x, h2, cw, cnt = _outproj_call(1, [attn1], [od_w_o[0].astype(BF16)], (x,), mod, norm2_g[1][None], r_hi, r_lo, r_b)
    y_ctx, y_lat = _moe_call(1, h2, cw, cnt, moe_w_gate, moe_w_up, moe_w_down, x, mod, split_out=True)

    y_prompt = y_ctx.reshape(BATCH, SEQ, D_MODEL)
    y_sample = y_lat.reshape(DEC_BATCH, DEC_SEQ, D_MODEL)
    new_mla_ckv = ckv[:N_CTX].reshape(BATCH, 1, SEQ, MLA_KV_RANK)
    new_mla_krope = kr[:N_CTX, MLA_NOPE:MLA_QK].reshape(BATCH, 1, SEQ, MLA_ROPE)
    new_diff_k = kn1.reshape(BATCH, 1, SEQ, DIFF_HEADS, 2, DIFF_DK)
    new_diff_v = vf1.reshape(BATCH, 1, SEQ, DIFF_HEADS, DIFF_DV)
    return (y_prompt, y_sample, new_mla_ckv, new_mla_krope, new_diff_k, new_diff_v)
```

```python
import functools
import math

import jax
import jax.numpy as jnp
from jax import lax
from jax.experimental import pallas as pl
from jax.experimental.pallas import tpu as pltpu

F32 = jnp.float32
BF16 = jnp.bfloat16

D_MODEL = 1024
BATCH = 16
SEQ = 256
DEC_BATCH = 8
DEC_SEQ = 2048
PAST_LEN = 256
GRID_W = 64
ROPE_BASE = 10000.0
EPS = 1e-6

CONV_CH = 512
CONV_WIDTH = 31
CONV_PAD = 15
MLA_HEADS = 8
MLA_NOPE = 64
MLA_ROPE = 32
MLA_QK = 96
MLA_V = 64
MLA_Q_RANK = 384
MLA_KV_RANK = 256
DIFF_HEADS = 8
DIFF_DK = 64
DIFF_DV = 128
N_GROUPS = 4
EXPERTS_PER_GROUP = 8
N_EXPERTS = 32
EXPERT_FF = 256

LANES = 128
SUBLANES = 8
N_CTX = BATCH * SEQ
N_LAT = DEC_BATCH * DEC_SEQ
N_TOK = N_CTX + N_LAT
N_CACHE = DEC_BATCH * PAST_LEN
MOD_ROWS = 16
VMEM_LIMIT = 48 * 1024 * 1024

TM = 512
NB_CTX = N_CTX // TM
NB_TOK = N_TOK // TM
BLK_PER_LAT = DEC_SEQ // TM
TQ = 256
CONV_T = 256
CONV_HALO = 16
CONV_RB = 32
CONV_SHIFT_ROWS = CONV_T + 2 * CONV_HALO - SUBLANES


def _params(*sem):
    return pltpu.CompilerParams(dimension_semantics=sem, vmem_limit_bytes=VMEM_LIMIT)


def _sigmoid(x):
    return 1.0 / (1.0 + jnp.exp(-x))


def _rms(x, g):
    return x * lax.rsqrt(jnp.mean(x * x, axis=-1, keepdims=True) + EPS) * g


def _dot(a, b):
    return jnp.dot(a, b, preferred_element_type=F32)


def _dot_nt(a, b):
    return lax.dot_general(a, b, (((1,), (1,)), ((), ())), preferred_element_type=F32)


def _seq_of_block(i):
    return jnp.where(i < NB_CTX, 0, 1 + (i - NB_CTX) // BLK_PER_LAT)


def _mod_spec(layer, slot):
    return pl.BlockSpec((1, 1, D_MODEL), lambda i, *_: ((layer * MOD_ROWS + _seq_of_block(i)) * 6 + slot, 0, 0))


def _row_spec(width, tm=TM):
    return pl.BlockSpec((tm, width), lambda i, *_: (i, 0))


def _ctx_spec(width):
    return pl.BlockSpec((TM, width), lambda i, *_: (jnp.minimum(i, NB_CTX - 1), 0))


def _lat_spec(width):
    return pl.BlockSpec((TM, width), lambda i, *_: (jnp.maximum(i - NB_CTX, 0), 0))


def _split_specs(width):
    return [_ctx_spec(width), _lat_spec(width)]


def _pick(ctx_ref, lat_ref):
    return jnp.where(pl.program_id(0) < NB_CTX, ctx_ref[...], lat_ref[...])


def _ones_blocks(group_log2):
    r = lax.broadcasted_iota(jnp.int32, (LANES, LANES), 0) >> group_log2
    c = lax.broadcasted_iota(jnp.int32, (LANES, LANES), 1) >> group_log2
    return jnp.where(r == c, 1.0, 0.0).astype(BF16)


def _full_spec(shape):
    nd = len(shape)
    return pl.BlockSpec(shape, lambda *_: (0,) * nd)


def _mod_kernel(c_ref, w_ref, b_ref, o_ref):
    c = c_ref[...]
    s = (c * _sigmoid(c)).astype(BF16)
    o_ref[0] = _dot(s, w_ref[0].astype(BF16)) + b_ref[0]


def _mod_call(cond, ada_w, ada_b):
    depth = ada_w.shape[0]
    tn = 1024
    return pl.pallas_call(
        _mod_kernel,
        grid=(depth, 6 * D_MODEL // tn),
        in_specs=[
            pl.BlockSpec((MOD_ROWS, D_MODEL), lambda l, j: (0, 0)),
            pl.BlockSpec((1, D_MODEL, tn), lambda l, j: (l, 0, j)),
            pl.BlockSpec((1, 1, tn), lambda l, j: (l, 0, j)),
        ],
        out_specs=pl.BlockSpec((1, MOD_ROWS, tn), lambda l, j: (l, 0, j)),
        out_shape=jax.ShapeDtypeStruct((depth, MOD_ROWS, 6 * D_MODEL), F32),
        compiler_params=_params("parallel", "parallel"),
        name="adaln_mod",
    )(cond, ada_w, ada_b.reshape(depth, 1, 6 * D_MODEL))


EV_W = 1792


def _inproj0_kernel(xc_ref, xl_ref, g_ref, sh_ref, sc_ref, w_ref, qg_ref, kvg_ref, u_ref, qn_ref, ckv_ref, kr_ref):
    h = _rms(_pick(xc_ref, xl_ref), g_ref[...]) * (1.0 + sc_ref[0]) + sh_ref[0]
    p = _dot(h.astype(BF16), w_ref[...])
    u_ref[...] = p[:, :CONV_CH] * _sigmoid(p[:, CONV_CH:2 * CONV_CH])
    o1 = 2 * CONV_CH
    o2 = o1 + MLA_Q_RANK
    o3 = o2 + MLA_KV_RANK
    qn_ref[...] = _rms(p[:, o1:o2], qg_ref[...]).astype(BF16)
    ckv_ref[...] = _rms(p[:, o2:o3], kvg_ref[...])
    kr_ref[...] = p[:, o3:o3 + LANES]


def _inproj0_call(x_ctx, x_lat, mod, g1, w_in, qg, kvg):
    return pl.pallas_call(
        _inproj0_kernel,
        grid=(NB_TOK,),
        in_specs=_split_specs(D_MODEL) + [
            _full_spec((1, D_MODEL)), _mod_spec(0, 0), _mod_spec(0, 1),
            _full_spec((D_MODEL, EV_W)), _full_spec((1, MLA_Q_RANK)), _full_spec((1, MLA_KV_RANK)),
        ],
        out_specs=[_row_spec(CONV_CH), _row_spec(MLA_Q_RANK), _row_spec(MLA_KV_RANK), _row_spec(LANES)],
        out_shape=[
            jax.ShapeDtypeStruct((N_TOK, CONV_CH), F32),
            jax.ShapeDtypeStruct((N_TOK, MLA_Q_RANK), BF16),
            jax.ShapeDtypeStruct((N_TOK, MLA_KV_RANK), F32),
            jax.ShapeDtypeStruct((N_TOK, LANES), F32),
        ],
        compiler_params=_params("parallel"),
        name="inproj0",
    )(x_ctx, x_lat, g1, mod, mod, w_in, qg, kvg)


CONV_CHUNKS_PER_LAT = DEC_SEQ // CONV_T
CONV_NB_CTX = N_CTX // CONV_T
CONV_NB = N_TOK // CONV_T


def _conv_kernel(prev_ref, cur_ref, nxt_ref, w_ref, b_ref, lg_ref, lb_ref, o_ref, buf, shifted):
    i = pl.program_id(0)
    j = (i - CONV_NB_CTX) % CONV_CHUNKS_PER_LAT
    first = jnp.logical_or(i < CONV_NB_CTX, j == 0)
    last = jnp.logical_or(i < CONV_NB_CTX, j == CONV_CHUNKS_PER_LAT - 1)
    buf[0:CONV_HALO, :] = jnp.where(first, 0.0, prev_ref[...])
    buf[CONV_HALO:CONV_HALO + CONV_T, :] = cur_ref[...]
    buf[CONV_HALO + CONV_T:, :] = jnp.where(last, 0.0, nxt_ref[...])
    for s in range(1, SUBLANES):
        shifted[s - 1] = buf[s:s + CONV_SHIFT_ROWS, :]
    for rb in range(CONV_T // CONV_RB):
        acc = jnp.zeros((CONV_RB, CONV_CH), F32)
        for k in range(CONV_WIDTH):
            row = rb * CONV_RB + CONV_HALO - CONV_PAD + k
            s, r0 = row % SUBLANES, row - row % SUBLANES
            src = buf[r0:r0 + CONV_RB, :] if s == 0 else shifted[s - 1, r0:r0 + CONV_RB, :]
            acc = acc + w_ref[k:k + 1, :] * src
        y = acc + b_ref[...]
        mu = jnp.mean(y, axis=-1, keepdims=True)
        yc = y - mu
        var = jnp.mean(yc * yc, axis=-1, keepdims=True)
        z = yc * lax.rsqrt(var + EPS) * lg_ref[...] + lb_ref[...]
        o_ref[rb * CONV_RB:(rb + 1) * CONV_RB, :] = (z * _sigmoid(z)).astype(BF16)


def _conv_call(u, w, b, lg, lb):
    hb = CONV_T // CONV_HALO
    n_halo_blocks = N_TOK // CONV_HALO
    return pl.pallas_call(
        _conv_kernel,
        grid=(CONV_NB,),
        in_specs=[
            pl.BlockSpec((CONV_HALO, CONV_CH), lambda i: (jnp.maximum(i * hb - 1, 0), 0)),
            pl.BlockSpec((CONV_T, CONV_CH), lambda i: (i, 0)),
            pl.BlockSpec((CONV_HALO, CONV_CH), lambda i: (jnp.minimum((i + 1) * hb, n_halo_blocks - 1), 0)),
            _full_spec((CONV_WIDTH, CONV_CH)), _full_spec((1, CONV_CH)),
            _full_spec((1, CONV_CH)), _full_spec((1, CONV_CH)),
        ],
        out_specs=pl.BlockSpec((CONV_T, CONV_CH), lambda i: (i, 0)),
        out_shape=jax.ShapeDtypeStruct((N_TOK, CONV_CH), BF16),
        scratch_shapes=[pltpu.VMEM((CONV_T + 2 * CONV_HALO, CONV_CH), F32),
                        pltpu.VMEM((SUBLANES - 1, CONV_SHIFT_ROWS, CONV_CH), F32)],
        compiler_params=_params("parallel"),
        name="conformer_conv",
    )(u, u, u, w, b, lg, lb)


def _rope_tile(x, c_ref, s1_ref, s2_ref, half):
    up = pltpu.roll(x, LANES - half, 1)
    dn = pltpu.roll(x, half, 1)
    return x * c_ref[...] + up * s1_ref[...] + dn * s2_ref[...]


def _rope_tables(rot_dim, lane_starts, n_tok):
    rows = n_tok // GRID_W
    row_ids = jnp.repeat(jnp.arange(rows, dtype=F32), GRID_W)
    col_ids = jnp.tile(jnp.arange(GRID_W, dtype=F32), rows)
    quarter = rot_dim // 4
    half = rot_dim // 2
    freqs = jnp.power(ROPE_BASE, -jnp.arange(quarter, dtype=F32) / quarter)
    ang = jnp.concatenate([row_ids[:, None] * freqs, col_ids[:, None] * freqs], axis=-1)
    cos, sin = jnp.cos(ang), jnp.sin(ang)
    one = lambda n: jnp.ones((n_tok, n), F32)
    zero = lambda n: jnp.zeros((n_tok, n), F32)
    c, s1, s2, at = [], [], [], 0
    for st in lane_starts:
        c += [one(st - at), cos, cos]
        s1 += [zero(st - at), -sin, zero(half)]
        s2 += [zero(st - at), zero(half), sin]
        at = st + rot_dim
    c, s1, s2 = (jnp.concatenate(t + [fill(LANES - at)], axis=1) for t, fill in ((c, one), (s1, zero), (s2, zero)))
    ident = (jnp.ones((TM, LANES), F32), jnp.zeros((TM, LANES), F32), jnp.zeros((TM, LANES), F32))
    return tuple(jnp.concatenate([t, e], axis=0) for t, e in zip((c, s1, s2), ident))


def _rope_block(i):
    return jnp.where(i < NB_CTX, BLK_PER_LAT, (i - NB_CTX) % BLK_PER_LAT)


def _mla_prep_kernel(qn_ref, ckv_ref, kr_ref, wq_ref, wk_ref, wv_ref, qg_ref, kg_ref, c_ref, s1_ref, s2_ref,
                     q_ref, k_ref, v_ref):
    qf = _dot(qn_ref[...], wq_ref[...])
    ckv = ckv_ref[...].astype(BF16)
    kf = _dot(ckv, wk_ref[...])
    v_ref[...] = _dot(ckv, wv_ref[...]).astype(BF16)
    kr = kr_ref[...]
    half = MLA_ROPE // 2
    ones = _ones_blocks(LANES.bit_length() - 1)
    for h in range(MLA_HEADS):
        sl = slice(h * LANES, (h + 1) * LANES)
        qh = _group_rms(qf[:, sl], qg_ref, ones, MLA_QK)
        q_ref[:, sl] = _rope_tile(qh, c_ref, s1_ref, s2_ref, half).astype(BF16)
        kh = _group_rms(kf[:, sl] + kr, kg_ref, ones, MLA_QK)
        k_ref[:, sl] = _rope_tile(kh, c_ref, s1_ref, s2_ref, half).astype(BF16)


def _mla_prep_call(qn, ckv, kr, wq, wk, wv, qg, kg, tables, rope_block_fn):
    n = qn.shape[0]
    hw = MLA_HEADS * LANES
    tspec = pl.BlockSpec((TM, LANES), lambda i: (rope_block_fn(i), 0))
    return pl.pallas_call(
        _mla_prep_kernel,
        grid=(n // TM,),
        in_specs=[
            _row_spec(MLA_Q_RANK), _row_spec(MLA_KV_RANK), _row_spec(LANES),
            _full_spec((MLA_Q_RANK, hw)), _full_spec((MLA_KV_RANK, hw)), _full_spec((MLA_KV_RANK, MLA_HEADS * MLA_V)),
            _full_spec((1, LANES)), _full_spec((1, LANES)), tspec, tspec, tspec,
        ],
        out_specs=[_row_spec(hw), _row_spec(hw), _row_spec(MLA_HEADS * MLA_V)],
        out_shape=[
            jax.ShapeDtypeStruct((n, hw), BF16),
            jax.ShapeDtypeStruct((n, hw), BF16),
            jax.ShapeDtypeStruct((n, MLA_HEADS * MLA_V), BF16),
        ],
        compiler_params=_params("parallel"),
        name="mla_prep",
    )(qn, ckv, kr, wq, wk, wv, qg, kg, *tables)


LOG2E = 1.4426950408889634
EXP2_SAFE = 100.0
BF16_NORM_MARGIN = 1.02


def _max_sq_norm(x, mask=None):
    xf = x.astype(F32)
    xx = xf * xf
    if mask is not None:
        xx = jnp.where(mask, xx, 0.0)
    return jnp.max(jnp.sum(xx, axis=-1, keepdims=True), axis=0, keepdims=True)


def _gain_sq_bound(g_ref, dims):
    g = g_ref[...]
    return jnp.max(g * g, axis=-1, keepdims=True) * (dims * BF16_NORM_MARGIN)


def _exps(s_list, shifted):
    if shifted:
        m = s_list[0].max(axis=-1, keepdims=True)
        for s in s_list[1:]:
            m = jnp.maximum(m, s.max(axis=-1, keepdims=True))
        es = [jnp.exp2(s - m) for s in s_list]
    else:
        es = [jnp.exp2(s) for s in s_list]
    l = es[0].sum(axis=-1, keepdims=True)
    for e in es[1:]:
        l = l + e.sum(axis=-1, keepdims=True)
    return es, l


def _run_query_blocks(bound_sq, n_blocks, block):
    safe = bound_sq[0, 0] <= EXP2_SAFE * EXP2_SAFE

    @pl.when(safe)
    def _():
        lax.fori_loop(0, n_blocks, functools.partial(block, False), 0)

    @pl.when(jnp.logical_not(safe))
    def _():
        lax.fori_loop(0, n_blocks, functools.partial(block, True), 0)


def _mla_attn_kernel(has_cache, tq, n_pairs, qg_ref, kg_ref, *refs):
    if has_cache:
        q_ref, k_ref, v_ref, kc_ref, vc_ref, o_ref = refs
        k_refs, v_refs = [k_ref, kc_ref], [v_ref, vc_ref]
    else:
        q_ref, k_ref, v_ref, o_ref = refs
        k_refs, v_refs = [k_ref], [v_ref]
    bound_sq = _gain_sq_bound(qg_ref, MLA_QK) * _gain_sq_bound(kg_ref, MLA_QK)

    def block(shifted, qi, carry):
        r0 = pl.multiple_of(qi * tq, tq)
        for hp in range(n_pairs):
            vsl = slice(hp * LANES, (hp + 1) * LANES)
            outs = []
            for hh in range(2):
                sl = slice((2 * hp + hh) * LANES, (2 * hp + hh + 1) * LANES)
                q = q_ref[pl.ds(r0, tq), sl]
                es, l = _exps([_dot_nt(q, kr[:, sl]) for kr in k_refs], shifted)
                o = _dot(es[0].astype(BF16), v_refs[0][:, vsl])
                for e, vr in zip(es[1:], v_refs[1:]):
                    o = o + _dot(e.astype(BF16), vr[:, vsl])
                outs.append(o / l)
            lane = lax.broadcasted_iota(jnp.int32, outs[0].shape, 1)
            o_ref[pl.ds(r0, tq), vsl] = jnp.where(lane < MLA_V, outs[0], outs[1]).astype(BF16)
        return carry

    _run_query_blocks(bound_sq, q_ref.shape[0] // tq, block)


def _mla_attn_ctx_call(qg, kg, q, k, v):
    return pl.pallas_call(
        functools.partial(_mla_attn_kernel, False, SEQ, MLA_HEADS // 2),
        grid=(BATCH,),
        in_specs=[
            _full_spec((1, LANES)), _full_spec((1, LANES)),
            pl.BlockSpec((SEQ, MLA_HEADS * LANES), lambda b: (b, 0)),
            pl.BlockSpec((SEQ, MLA_HEADS * LANES), lambda b: (b, 0)),
            pl.BlockSpec((SEQ, MLA_HEADS * MLA_V), lambda b: (b, 0)),
        ],
        out_specs=pl.BlockSpec((SEQ, MLA_HEADS * MLA_V), lambda b: (b, 0)),
        out_shape=jax.ShapeDtypeStruct((N_CTX, MLA_HEADS * MLA_V), BF16),
        compiler_params=_params("parallel"),
        name="mla_attn_ctx",
    )(qg, kg, q, k, v)


def _mla_attn_lat_call(qg, kg, q, k, v, kc, vc):
    k0 = N_CTX // DEC_SEQ
    return pl.pallas_call(
        functools.partial(_mla_attn_kernel, True, TQ, 1),
        grid=(DEC_BATCH, MLA_HEADS // 2),
        in_specs=[
            _full_spec((1, LANES)), _full_spec((1, LANES)),
            pl.BlockSpec((DEC_SEQ, 2 * LANES), lambda b, hp: (k0 + b, hp)),
            pl.BlockSpec((DEC_SEQ, 2 * LANES), lambda b, hp: (k0 + b, hp)),
            pl.BlockSpec((DEC_SEQ, LANES), lambda b, hp: (k0 + b, hp)),
            pl.BlockSpec((PAST_LEN, 2 * LANES), lambda b, hp: (b, hp)),
            pl.BlockSpec((PAST_LEN, LANES), lambda b, hp: (b, hp)),
        ],
        out_specs=pl.BlockSpec((DEC_SEQ, LANES), lambda b, hp: (b, hp)),
        out_shape=jax.ShapeDtypeStruct((N_LAT, MLA_HEADS * MLA_V), BF16),
        compiler_params=_params("parallel", "parallel"),
        name="mla_attn_lat",
    )(qg, kg, q, k, v, kc, vc)


def _route(logits):
    lane = lax.broadcasted_iota(jnp.int32, logits.shape, 1)
    lane_f = lane.astype(F32)
    big = float(LANES)
    neg = -jnp.inf
    gmask = jnp.logical_and(lane >= N_EXPERTS, lane < N_EXPERTS + N_GROUPS)
    gl = jnp.where(gmask, logits, neg)
    ge = jnp.exp(gl - jnp.max(gl, axis=-1, keepdims=True))
    gp = ge / jnp.sum(ge, axis=-1, keepdims=True)
    gate = jnp.max(gp, axis=-1, keepdims=True)
    gidx = jnp.min(jnp.where(jnp.logical_and(gmask, gp == gate), lane_f, big), axis=-1, keepdims=True) - N_EXPERTS
    lo = gidx * EXPERTS_PER_GROUP
    emask = jnp.logical_and(lane_f >= lo, lane_f < lo + EXPERTS_PER_GROUP)
    el = jnp.where(emask, logits, neg)
    ee = jnp.exp(el - jnp.max(el, axis=-1, keepdims=True))
    ep = jnp.where(emask, ee / jnp.sum(ee, axis=-1, keepdims=True), -1.0)
    p1 = jnp.max(ep, axis=-1, keepdims=True)
    i1 = jnp.min(jnp.where(ep == p1, lane_f, big), axis=-1, keepdims=True)
    ep2 = jnp.where(lane_f == i1, -1.0, ep)
    p2 = jnp.max(ep2, axis=-1, keepdims=True)
    i2 = jnp.min(jnp.where(ep2 == p2, lane_f, big), axis=-1, keepdims=True)
    den = p1 + p2
    return jnp.where(lane_f == i1, gate * p1 / den, 0.0) + jnp.where(lane_f == i2, gate * p2 / den, 0.0)


def _outproj_kernel(arity, *refs):
    refs = list(refs)
    rows = []
    for n in arity:
        rows.append(refs[0][...] if n == 1 else _pick(refs[0], refs[1]))
        refs = refs[n:]
    n_act = len(arity) - 1
    w_refs, refs = refs[:n_act], refs[n_act:]
    g1_ref, n2_ref, sh2_ref, sc2_ref, rh_ref, rl_ref, rb_ref, xo_ref, h_ref, cw_ref, cnt_ref = refs
    out = _dot(rows[0], w_refs[0][...])
    for a, w_ref in zip(rows[1:n_act], w_refs[1:]):
        out = out + _dot(a, w_ref[...])
    x = rows[n_act] + g1_ref[0] * out
    xo_ref[...] = x
    h = _rms(x, n2_ref[...]) * (1.0 + sc2_ref[0]) + sh2_ref[0]
    h_hi = h.astype(BF16)
    h_ref[...] = h_hi
    h_lo = (h - h_hi.astype(F32)).astype(BF16)
    logits = _dot(h_hi, rh_ref[...]) + (_dot(h_hi, rl_ref[...]) + _dot(h_lo, rh_ref[...])) + rb_ref[...]
    cw = _route(logits)
    cw_ref[...] = cw
    cnt_ref[0] = jnp.sum(jnp.where(cw > 0.0, 1.0, 0.0), axis=0, keepdims=True)


def _outproj_call(layer, acts, ws, x, mod, n2, r_hi, r_lo, r_b):
    operands = list(acts) + [x]
    row_specs = []
    for op in operands:
        row_specs += [_row_spec(op[0].shape[1])] if len(op) == 1 else _split_specs(op[0].shape[1])
    flat = [a for op in operands for a in op]
    return pl.pallas_call(
        functools.partial(_outproj_kernel, tuple(len(op) for op in operands)),
        grid=(NB_TOK,),
        in_specs=(
            row_specs + [_full_spec(w.shape) for w in ws]
            + [_mod_spec(layer, 2), _full_spec((1, D_MODEL)), _mod_spec(layer, 3),
               _mod_spec(layer, 4), _full_spec((D_MODEL, LANES)), _full_spec((D_MODEL, LANES)), _full_spec((1, LANES))]
        ),
        out_specs=[_row_spec(D_MODEL), _row_spec(D_MODEL), _row_spec(LANES),
                   pl.BlockSpec((1, 1, LANES), lambda i: (i, 0, 0))],
        out_shape=[
            jax.ShapeDtypeStruct((N_TOK, D_MODEL), F32),
            jax.ShapeDtypeStruct((N_TOK, D_MODEL), BF16),
            jax.ShapeDtypeStruct((N_TOK, LANES), F32),
            jax.ShapeDtypeStruct((NB_TOK, 1, LANES), F32),
        ],
        compiler_params=_params("parallel"),
        name=f"outproj{layer}",
    )(*flat, *ws, mod, n2, mod, mod, r_hi, r_lo, r_b)


MOE_CHUNK = 16
MOE_TILE = 512
NW = N_TOK // TM
R_LOC = 2 * TM + N_EXPERTS * MOE_CHUNK
R_SORTED_MAX = 2 * N_TOK + NW * N_EXPERTS * (MOE_CHUNK - 1) + N_EXPERTS * (MOE_TILE - 1)
NT_MAX = -(-R_SORTED_MAX // MOE_TILE)
P_MAX = R_LOC // MOE_CHUNK
PIECE_GROUP = 8
TRASH_ROW = NT_MAX * MOE_TILE
SORTED_ROWS = TRASH_ROW + PIECE_GROUP * MOE_CHUNK


def _moe_tables(cnt):
    cnt = cnt[:, 0, :N_EXPERTS].astype(jnp.int32)
    padc = (cnt + (MOE_CHUNK - 1)) // MOE_CHUNK * MOE_CHUNK
    loff = jnp.cumsum(padc, axis=1) - padc
    tot = jnp.sum(padc, axis=0)
    seg = (tot + (MOE_TILE - 1)) // MOE_TILE * MOE_TILE
    seg_end = jnp.cumsum(seg)
    base = seg_end - seg
    goff = base[None, :] + jnp.cumsum(padc, axis=0) - padc
    tile_start = jnp.arange(NT_MAX, dtype=jnp.int32) * MOE_TILE
    texp = jnp.minimum(jnp.sum((tile_start[:, None] >= seg_end[None, :]).astype(jnp.int32), axis=1), N_EXPERTS - 1)
    nvalid = (seg_end[-1:] // MOE_TILE).astype(jnp.int32)
    p_row = jnp.arange(P_MAX, dtype=jnp.int32)[None, :, None] * MOE_CHUNK
    inside = jnp.logical_and(p_row >= loff[:, None, :], p_row < (loff + padc)[:, None, :])
    grow = jnp.sum(jnp.where(inside, goff[:, None, :] + p_row - loff[:, None, :], 0), axis=-1)
    used = jnp.any(inside, axis=-1)
    filler = TRASH_ROW + (jnp.arange(P_MAX, dtype=jnp.int32) % PIECE_GROUP)[None, :] * MOE_CHUNK
    npieces = jnp.sum(padc, axis=1) // MOE_CHUNK
    i32 = lambda a: a.reshape(-1).astype(jnp.int32)
    return dict(dst=i32(jnp.where(used, grow, filler)), src=i32(jnp.where(used, grow, 0)),
                ngroups=i32((npieces + (PIECE_GROUP - 1)) // PIECE_GROUP), tail_start=i32(base + tot),
                tail_chunks=i32((seg - tot) // MOE_CHUNK), texp=texp, nvalid=nvalid)


def _local_positions(cw):
    n = cw.shape[0]
    a = cw > 0.0
    a_f = jnp.where(a, 1.0, 0.0)
    below = jnp.where(lax.broadcasted_iota(jnp.int32, (n, n), 1) < lax.broadcasted_iota(jnp.int32, (n, n), 0), 1.0, 0.0)
    rank = _dot(below.astype(BF16), a_f.astype(BF16))
    cnt = jnp.sum(a_f, axis=0, keepdims=True)
    padc = jnp.floor((cnt + (MOE_CHUNK - 1)) * (1.0 / MOE_CHUNK)) * MOE_CHUNK
    before = jnp.where(lax.broadcasted_iota(jnp.int32, (LANES, LANES), 0) < lax.broadcasted_iota(jnp.int32, (LANES, LANES), 1),
                       1.0, 0.0)
    loff = _dot(jnp.broadcast_to(padc, (8, LANES)).astype(BF16), before.astype(BF16))[0:1]
    lpos = rank + loff
    pa = jnp.min(jnp.where(a, lpos, float(R_LOC)), axis=-1, keepdims=True)
    pb = jnp.max(jnp.where(a, lpos, -1.0), axis=-1, keepdims=True)
    wa = jnp.sum(jnp.where(jnp.logical_and(a, lpos == pa), cw, 0.0), axis=-1, keepdims=True)
    wb = jnp.sum(jnp.where(jnp.logical_and(a, lpos == pb), cw, 0.0), axis=-1, keepdims=True)
    return pa, pb, wa, wb


def _for_each_piece(w, ngroups_ref, grow_ref, fn):
    n = ngroups_ref[w]

    def group(j, carry):
        for u in range(PIECE_GROUP):
            p = j * PIECE_GROUP + u
            fn(pl.multiple_of(p * MOE_CHUNK, MOE_CHUNK), pl.multiple_of(grow_ref[w * P_MAX + p], MOE_CHUNK))
        return carry

    lax.fori_loop(0, n, group, 0)
    return n


def _dispatch_kernel(ngroups_ref, dst_ref, tstart_ref, tchunk_ref, h_ref, cw_ref, hs_ref, buf, zbuf, sems):
    w = pl.program_id(0)
    slot = w % 2
    pa, pb, _, _ = _local_positions(cw_ref[...])
    lane = lax.broadcasted_iota(jnp.int32, (TM, LANES), 1)
    pos_t = jnp.where(lane == 0, pa, jnp.where(lane == 1, pb, 0.0)).T
    r = lax.broadcasted_iota(jnp.int32, (R_LOC, TM), 0).astype(F32)
    sel = jnp.logical_or(r == pos_t[0:1, :], r == pos_t[1:2, :])
    buf[slot] = _dot(jnp.where(sel, 1.0, 0.0).astype(BF16), h_ref[...]).astype(BF16)

    def wait_groups(window, s):
        def wait_group(c, carry):
            pltpu.make_async_copy(buf.at[s, pl.ds(0, PIECE_GROUP * MOE_CHUNK)],
                                  hs_ref.at[pl.ds(0, PIECE_GROUP * MOE_CHUNK)], sems.at[s]).wait()
            return carry

        lax.fori_loop(0, ngroups_ref[window], wait_group, 0)

    @pl.when(w > 0)
    def _():
        wait_groups(w - 1, 1 - slot)

    def piece_copy(local_row, global_row):
        return pltpu.make_async_copy(buf.at[slot, pl.ds(local_row, MOE_CHUNK)],
                                     hs_ref.at[pl.ds(global_row, MOE_CHUNK)], sems.at[slot])

    _for_each_piece(w, ngroups_ref, dst_ref, lambda lo, go: piece_copy(lo, go).start())

    @pl.when(w == NW - 1)
    def _():
        wait_groups(w, slot)
        sem = sems.at[slot]
        zbuf[...] = jnp.zeros_like(zbuf)

        def zero_copy(global_row):
            return pltpu.make_async_copy(zbuf, hs_ref.at[pl.ds(global_row, MOE_CHUNK)], sem)

        def per_expert(e, total):
            n = tchunk_ref[e]
            st = tstart_ref[e]

            def piece(c, carry):
                zero_copy(pl.multiple_of(st + c * MOE_CHUNK, MOE_CHUNK)).start()
                return carry

            lax.fori_loop(0, n, piece, 0)
            return total + n

        total_z = lax.fori_loop(0, N_EXPERTS, per_expert, 0)

        def wait_zero(c, carry):
            zero_copy(0).wait()
            return carry

        lax.fori_loop(0, total_z, wait_zero, 0)


def _dispatch_call(layer, h, cw, tb):
    return pl.pallas_call(
        _dispatch_kernel,
        grid_spec=pltpu.PrefetchScalarGridSpec(
            num_scalar_prefetch=4,
            grid=(NW,),
            in_specs=[_row_spec(D_MODEL), _row_spec(LANES)],
            out_specs=pl.BlockSpec(memory_space=pl.ANY),
            scratch_shapes=[pltpu.VMEM((2, R_LOC, D_MODEL), BF16), pltpu.VMEM((MOE_CHUNK, D_MODEL), BF16),
                            pltpu.SemaphoreType.DMA((2,))],
        ),
        out_shape=jax.ShapeDtypeStruct((SORTED_ROWS, D_MODEL), BF16),
        compiler_params=_params("arbitrary"),
        name=f"moe_dispatch{layer}",
    )(tb["ngroups"], tb["dst"], tb["tail_start"], tb["tail_chunks"], h, cw)


def _expert_kernel(texp_ref, nvalid_ref, hs_ref, wg_ref, wu_ref, wd_ref, ys_ref, wg_b, wu_b, wd_b):
    i = pl.program_id(0)
    fresh = jnp.logical_or(i == 0, texp_ref[i] != texp_ref[jnp.maximum(i - 1, 0)])

    @pl.when(fresh)
    def _():
        wg_b[...] = wg_ref[0].astype(BF16)
        wu_b[...] = wu_ref[0].astype(BF16)
        wd_b[...] = wd_ref[0].astype(BF16)

    @pl.when(i < nvalid_ref[0])
    def _():
        h = hs_ref[...]
        a = _dot(h, wg_b[...])
        b = _dot(h, wu_b[...])
        ys_ref[...] = _dot((a * _sigmoid(a) * b).astype(BF16), wd_b[...]).astype(BF16)


def _expert_call(layer, hs, tb, w_gate, w_up, w_down):
    tile_map = lambda i, texp, nv: (jnp.minimum(i, nv[0] - 1), 0)
    w_map = lambda i, texp, nv: (layer, texp[i], 0, 0)
    return pl.pallas_call(
        _expert_kernel,
        grid_spec=pltpu.PrefetchScalarGridSpec(
            num_scalar_prefetch=2,
            grid=(NT_MAX,),
            in_specs=[
                pl.BlockSpec((MOE_TILE, D_MODEL), tile_map),
                pl.BlockSpec((None, 1, D_MODEL, EXPERT_FF), w_map),
                pl.BlockSpec((None, 1, D_MODEL, EXPERT_FF), w_map),
                pl.BlockSpec((None, 1, EXPERT_FF, D_MODEL), w_map),
            ],
            out_specs=pl.BlockSpec((MOE_TILE, D_MODEL), tile_map),
            scratch_shapes=[pltpu.VMEM((D_MODEL, EXPERT_FF), BF16), pltpu.VMEM((D_MODEL, EXPERT_FF), BF16),
                            pltpu.VMEM((EXPERT_FF, D_MODEL), BF16)],
        ),
        out_shape=jax.ShapeDtypeStruct((NT_MAX * MOE_TILE, D_MODEL), BF16),
        compiler_params=_params("arbitrary"),
        name=f"moe_experts{layer}",
    )(tb["texp"], tb["nvalid"], hs, w_gate, w_up, w_down)


def _combine_kernel(split_out, ngroups_ref, src_ref, cw_ref, x_ref, g2_ref, ys_ref, *refs):
    o_refs, (buf, sems) = refs[:-2], refs[-2:]
    w = pl.program_id(0)
    slot = w % 2

    def fetch(window, s):
        def piece_copy(local_row, global_row):
            return pltpu.make_async_copy(ys_ref.at[pl.ds(global_row, MOE_CHUNK)],
                                         buf.at[s, pl.ds(local_row, MOE_CHUNK)], sems.at[s])

        _for_each_piece(window, ngroups_ref, src_ref, lambda lo, go: piece_copy(lo, go).start())

    @pl.when(w == 0)
    def _():
        buf[...] = jnp.zeros_like(buf)
        fetch(0, 0)

    @pl.when(w + 1 < NW)
    def _():
        fetch(w + 1, 1 - slot)

    pa, pb, wa, wb = _local_positions(cw_ref[...])
    r = lax.broadcasted_iota(jnp.int32, (TM, R_LOC), 1).astype(F32)
    wt = jnp.where(r == pa, wa, jnp.where(r == pb, wb, 0.0)).astype(BF16)

    def wait_group(c, carry):
        pltpu.make_async_copy(ys_ref.at[pl.ds(0, PIECE_GROUP * MOE_CHUNK)],
                              buf.at[slot, pl.ds(0, PIECE_GROUP * MOE_CHUNK)], sems.at[slot]).wait()
        return carry

    lax.fori_loop(0, ngroups_ref[w], wait_group, 0)
    y = x_ref[...] + g2_ref[0] * _dot(wt, buf[slot])
    if split_out:
        @pl.when(w < NB_CTX)
        def _():
            o_refs[0][...] = y

        @pl.when(w >= NB_CTX)
        def _():
            o_refs[1][...] = y
    else:
        o_refs[0][...] = y


def _combine_call(layer, cw, x, mod, ys, tb, split_out):
    if split_out:
        out_specs = _split_specs(D_MODEL)
        out_shape = [jax.ShapeDtypeStruct((N_CTX, D_MODEL), F32), jax.ShapeDtypeStruct((N_LAT, D_MODEL), F32)]
    else:
        out_specs = [_row_spec(D_MODEL)]
        out_shape = [jax.ShapeDtypeStruct((N_TOK, D_MODEL), F32)]
    return pl.pallas_call(
        functools.partial(_combine_kernel, split_out),
        grid_spec=pltpu.PrefetchScalarGridSpec(
            num_scalar_prefetch=2,
            grid=(NW,),
            in_specs=[_row_spec(LANES), _row_spec(D_MODEL), _mod_spec(layer, 5), pl.BlockSpec(memory_space=pl.ANY)],
            out_specs=out_specs,
            scratch_shapes=[pltpu.VMEM((2, R_LOC, D_MODEL), BF16), pltpu.SemaphoreType.DMA((2,))],
        ),
        out_shape=out_shape,
        compiler_params=_params("arbitrary"),
        name=f"moe_combine{layer}",
    )(tb["ngroups"], tb["src"], cw, x, mod, ys)


def _moe_call(layer, h, cw, cnt, w_gate, w_up, w_down, x, mod, split_out=False):
    tb = _moe_tables(cnt)
    hs = _dispatch_call(layer, h, cw, tb)
    ys = _expert_call(layer, hs, tb, w_gate, w_up, w_down)
    return _combine_call(layer, cw, x, mod, ys, tb, split_out)


def _group_rms(x, g_ref, ones_blocks, n):
    ss = _dot((x * x).astype(BF16), ones_blocks)
    return x * lax.rsqrt(ss * (1.0 / n) + EPS) * g_ref[...]


def _inproj1_kernel(x_ref, g_ref, sh_ref, sc_ref, w_ref, qg_ref, kg_ref, c_ref, s1_ref, s2_ref,
                    q_ref, k_ref, v_ref, kn_ref, vf_ref):
    i = pl.program_id(0)
    h = (_rms(x_ref[...], g_ref[...]) * (1.0 + sc_ref[0]) + sh_ref[0]).astype(BF16)
    hw = DIFF_HEADS * DIFF_DV
    half = DIFF_DK // 2
    ones_blocks = _ones_blocks(DIFF_DK.bit_length() - 1)
    for tp in range(DIFF_HEADS // 2):
        pair = slice(tp * 2 * LANES, (tp + 1) * 2 * LANES)
        pq = _dot(h, w_ref[:, pair])
        pk = _dot(h, w_ref[:, hw + tp * 2 * LANES:hw + (tp + 1) * 2 * LANES])
        for u in range(2):
            sl = slice((2 * tp + u) * LANES, (2 * tp + u + 1) * LANES)
            qt = _group_rms(pq[:, u * LANES:(u + 1) * LANES], qg_ref, ones_blocks, DIFF_DK)
            q_ref[:, sl] = _rope_tile(qt, c_ref, s1_ref, s2_ref, half).astype(BF16)
            kt = _group_rms(pk[:, u * LANES:(u + 1) * LANES], kg_ref, ones_blocks, DIFF_DK)
            k_ref[:, sl] = _rope_tile(kt, c_ref, s1_ref, s2_ref, half).astype(BF16)

            @pl.when(i < NB_CTX)
            def _():
                kn_ref[:, sl] = kt

    v = _dot(h, w_ref[:, 2 * hw:])
    v_ref[...] = v.astype(BF16)

    @pl.when(i < NB_CTX)
    def _():
        vf_ref[...] = v


def _inproj1_call(x, mod, g1, w_qkv, qg, kg, tables):
    hw = DIFF_HEADS * DIFF_DV
    tspec = pl.BlockSpec((TM, LANES), lambda i: (_rope_block(i), 0))
    ctx_spec = pl.BlockSpec((TM, hw), lambda i: (jnp.minimum(i, NB_CTX - 1), 0))
    return pl.pallas_call(
        _inproj1_kernel,
        grid=(NB_TOK,),
        in_specs=[
            _row_spec(D_MODEL), _full_spec((1, D_MODEL)), _mod_spec(1, 0), _mod_spec(1, 1),
            _full_spec((D_MODEL, 3 * hw)), _full_spec((1, LANES)), _full_spec((1, LANES)), tspec, tspec, tspec,
        ],
        out_specs=[_row_spec(hw), _row_spec(hw), _row_spec(hw), ctx_spec, ctx_spec],
        out_shape=[
            jax.ShapeDtypeStruct((N_TOK, hw), BF16),
            jax.ShapeDtypeStruct((N_TOK, hw), BF16),
            jax.ShapeDtypeStruct((N_TOK, hw), BF16),
            jax.ShapeDtypeStruct((N_CTX, hw), F32),
            jax.ShapeDtypeStruct((N_CTX, hw), F32),
        ],
        compiler_params=_params("arbitrary"),
        name="inproj1",
    )(x, g1, mod, mod, w_qkv, qg, kg, *tables)


def _diff_attn_kernel(has_cache, lambda_init, tq, n_heads, qg_ref, kg_ref, *refs):
    if has_cache:
        q_ref, k_ref, v_ref, kc_ref, vc_ref, lq1, lk1, lq2, lk2, sg_ref, o_ref = refs
        k_refs, v_refs = [k_ref, kc_ref], [v_ref, vc_ref]
    else:
        q_ref, k_ref, v_ref, lq1, lk1, lq2, lk2, sg_ref, o_ref = refs
        k_refs, v_refs = [k_ref], [v_ref]
    lam = (jnp.exp(jnp.sum(lq1[...] * lk1[...], axis=-1, keepdims=True))
           - jnp.exp(jnp.sum(lq2[...] * lk2[...], axis=-1, keepdims=True)) + lambda_init)
    lo = lax.broadcasted_iota(jnp.int32, (1, LANES), 1) < DIFF_DK
    k_sq = _gain_sq_bound(kg_ref, DIFF_DK)
    if has_cache:
        for mask in (lo, jnp.logical_not(lo)):
            k_sq = jnp.maximum(k_sq, _max_sq_norm(kc_ref[...], mask))
    bound_sq = _gain_sq_bound(qg_ref, DIFF_DK) * k_sq

    def block(shifted, qi, carry):
        r0 = pl.multiple_of(qi * tq, tq)
        for hh in range(n_heads):
            sl = slice(hh * LANES, (hh + 1) * LANES)
            q = q_ref[pl.ds(r0, tq), sl]
            zero = jnp.zeros_like(q)
            e0, l0 = _exps([_dot_nt(jnp.where(lo, q, zero), kr[:, sl]) for kr in k_refs], shifted)
            e1, l1 = _exps([_dot_nt(jnp.where(lo, zero, q), kr[:, sl]) for kr in k_refs], shifted)
            f0 = 1.0 / l0
            f1 = lam / l1
            o = None
            for a, b, vr in zip(e0, e1, v_refs):
                t = _dot((a * f0 - b * f1).astype(BF16), vr[:, sl])
                o = t if o is None else o + t
            o = o * lax.rsqrt(jnp.mean(o * o, axis=-1, keepdims=True) + EPS) * sg_ref[...] * (1.0 - lambda_init)
            o_ref[pl.ds(r0, tq), sl] = o.astype(BF16)
        return carry

    _run_query_blocks(bound_sq, q_ref.shape[0] // tq, block)


def _diff_attn_ctx_call(lambda_init, gains, q, k, v, lams, sg):
    lspec = _full_spec((1, DIFF_DK))
    gspec = _full_spec((1, LANES))
    return pl.pallas_call(
        functools.partial(_diff_attn_kernel, False, lambda_init, SEQ, DIFF_HEADS),
        grid=(BATCH,),
        in_specs=[gspec, gspec] + [pl.BlockSpec((SEQ, DIFF_HEADS * LANES), lambda b: (b, 0))] * 3 + [lspec] * 4
        + [_full_spec((1, DIFF_DV))],
        out_specs=pl.BlockSpec((SEQ, DIFF_HEADS * LANES), lambda b: (b, 0)),
        out_shape=jax.ShapeDtypeStruct((N_CTX, DIFF_HEADS * DIFF_DV), BF16),
        compiler_params=_params("parallel"),
        name="diff_attn_ctx",
    )(*gains, q, k, v, *lams, sg)


def _diff_attn_lat_call(lambda_init, gains, q, k, v, kc, vc, lams, sg):
    k0 = N_CTX // DEC_SEQ
    lspec = _full_spec((1, DIFF_DK))
    gspec = _full_spec((1, LANES))
    seq_spec = pl.BlockSpec((DEC_SEQ, LANES), lambda b, h: (k0 + b, h))
    c_spec = pl.BlockSpec((PAST_LEN, LANES), lambda b, h: (b, h))
    return pl.pallas_call(
        functools.partial(_diff_attn_kernel, True, lambda_init, TQ, 1),
        grid=(DEC_BATCH, DIFF_HEADS),
        in_specs=[gspec, gspec, seq_spec, seq_spec, seq_spec, c_spec, c_spec] + [lspec] * 4 + [_full_spec((1, DIFF_DV))],
        out_specs=pl.BlockSpec((DEC_SEQ, LANES), lambda b, h: (b, h)),
        out_shape=jax.ShapeDtypeStruct((N_LAT, DIFF_HEADS * DIFF_DV), BF16),
        compiler_params=_params("parallel", "parallel"),
        name="diff_attn_lat",
    )(*gains, q, k, v, kc, vc, *lams, sg)


def _router_weights(we, wg, be, bg):
    pad = LANES - N_EXPERTS - N_GROUPS
    w = jnp.concatenate([we, wg, jnp.zeros((D_MODEL, pad), F32)], axis=1)
    b = jnp.concatenate([be, bg, jnp.zeros((pad,), F32)])[None]
    hi = w.astype(BF16)
    lo = (w - hi.astype(F32)).astype(BF16)
    return hi, lo, b


def kernel(x_prompt, x_sample, cache_mla_ckv, cache_mla_krope, cache_diff_k, cache_diff_v, c, c_ctx, ada_w, ada_b, norm1_g, norm2_g, ev_w_in, ev_conv_w, ev_conv_b, ev_conv_ln_g, ev_conv_ln_b, ev_q_norm_g, ev_w_qb, ev_kv_norm_g, ev_w_kvb, ev_qn_g, ev_kn_g, ev_w_o, od_w_qkv, od_qn_g, od_kn_g, od_lambda_q1, od_lambda_k1, od_lambda_q2, od_lambda_k2, od_subln_g, od_w_o, moe_wg, moe_bg, moe_we, moe_be, moe_w_gate, moe_w_up, moe_w_down):
    x_in = (x_prompt.reshape(N_CTX, D_MODEL), x_sample.reshape(N_LAT, D_MODEL))
    cond = jnp.concatenate([c_ctx[None, :], c, jnp.zeros((MOD_ROWS - 1 - DEC_BATCH, D_MODEL), F32)], axis=0)
    mod = _mod_call(cond, ada_w, ada_b).reshape(-1, 1, D_MODEL)

    o3 = 2 * CONV_CH + MLA_Q_RANK + MLA_KV_RANK
    w_in = jnp.concatenate([ev_w_in[0][:, :o3], jnp.zeros((D_MODEL, MLA_NOPE), F32), ev_w_in[0][:, o3:],
                            jnp.zeros((D_MODEL, LANES - MLA_QK), F32)], axis=1).astype(BF16)
    u, qn, ckv, kr = _inproj0_call(*x_in, mod, norm1_g[0][None], w_in, ev_q_norm_g[0][None], ev_kv_norm_g[0][None])
    conv = _conv_call(u, ev_conv_w[0], ev_conv_b[0][None], ev_conv_ln_g[0][None], ev_conv_ln_b[0][None])

    pad_qk = ((0, 0), (0, 0), (0, LANES - MLA_QK))
    wq = jnp.pad(ev_w_qb[0].reshape(MLA_Q_RANK, MLA_HEADS, MLA_QK), pad_qk).reshape(MLA_Q_RANK, -1).astype(BF16)
    wkv = ev_w_kvb[0].reshape(MLA_KV_RANK, MLA_HEADS, MLA_NOPE + MLA_V)
    wk = jnp.pad(wkv[..., :MLA_NOPE], ((0, 0), (0, 0), (0, LANES - MLA_NOPE))).reshape(MLA_KV_RANK, -1).astype(BF16)
    wv = wkv[..., MLA_NOPE:].reshape(MLA_KV_RANK, -1).astype(BF16)
    qg = jnp.pad(ev_qn_g[0] * (MLA_QK ** -0.5 * LOG2E), (0, LANES - MLA_QK))[None]
    kg = jnp.pad(ev_kn_g[0], (0, LANES - MLA_QK))[None]
    tables = _rope_tables(MLA_ROPE, (MLA_NOPE,), DEC_SEQ)
    q0, k0, v0 = _mla_prep_call(qn, ckv, kr, wq, wk, wv, qg, kg, tables, _rope_block)
    kr_cache = jnp.pad(cache_mla_krope[:, 0].reshape(N_CACHE, MLA_ROPE), ((0, 0), (MLA_NOPE, LANES - MLA_QK)))
    _, kc0, vc0 = _mla_prep_call(jnp.zeros((N_CACHE, MLA_Q_RANK), BF16), cache_mla_ckv[:, 0].reshape(N_CACHE, MLA_KV_RANK),
                                 kr_cache, wq, wk, wv, qg, kg, tables, lambda i: BLK_PER_LAT)
    attn = (_mla_attn_ctx_call(qg, kg, q0, k0, v0), _mla_attn_lat_call(qg, kg, q0, k0, v0, kc0, vc0))

    w_o = ev_w_o[0].astype(BF16)
    r_hi, r_lo, r_b = _router_weights(moe_we[0], moe_wg[0], moe_be[0], moe_bg[0])
    x, h2, cw, cnt = _outproj_call(0, [(conv,), attn], [w_o[:CONV_CH], w_o[CONV_CH:]], x_in, mod, norm2_g[0][None], r_hi, r_lo, r_b)
    (x,) = _moe_call(0, h2, cw, cnt, moe_w_gate, moe_w_up, moe_w_down, x, mod)

    lambda_init = 0.8 - 0.6 * math.exp(-0.3 * 1)
    qg1 = jnp.tile(od_qn_g[0] * (DIFF_DK ** -0.5 * LOG2E), 2)[None]
    kg1 = jnp.tile(od_kn_g[0], 2)[None]
    tables1 = _rope_tables(DIFF_DK, (0, DIFF_DK), DEC_SEQ)
    q1, k1, v1, kn1, vf1 = _inproj1_call(x, mod, norm1_g[1][None], od_w_qkv[0].astype(BF16), qg1, kg1, tables1)
    lams = [od_lambda_q1[0][None], od_lambda_k1[0][None], od_lambda_q2[0][None], od_lambda_k2[0][None]]
    sg = od_subln_g[0][None]
    hw = DIFF_HEADS * DIFF_DV
    kc1 = cache_diff_k[:, 0].reshape(N_CACHE, hw).astype(BF16)
    vc1 = cache_diff_v[:, 0].reshape(N_CACHE, hw).astype(BF16)
    attn1 = (_diff_attn_ctx_call(lambda_init, (qg1, kg1), q1, k1, v1, lams, sg),
             _diff_attn_lat_call(lambda_init, (qg1, kg1), q1, k1, v1, kc1, vc1, lams, sg))
    r_hi, r_lo, r_b = _router_weights(moe_we[1], moe_wg[1], moe_be[1], moe_bg[1])
    x, h2, cw, cnt = _outproj_call(1, [attn1], [od_w_o[0].astype(BF16)], (x,), mod, norm2_g[1][None], r_hi, r_lo, r_b)
    y_ctx, y_lat = _moe_call(1, h2, cw, cnt, moe_w_gate, moe_w_up, moe_w_down, x, mod, split_out=True)

    y_prompt = y_ctx.reshape(BATCH, SEQ, D_MODEL)
    y_sample = y_lat.reshape(DEC_BATCH, DEC_SEQ, D_MODEL)
    new_mla_ckv = ckv[:N_CTX].reshape(BATCH, 1, SEQ, MLA_KV_RANK)
    new_mla_krope = kr[:N_CTX, MLA_NOPE:MLA_QK].reshape(BATCH, 1, SEQ, MLA_ROPE)
    new_diff_k = kn1.reshape(BATCH, 1, SEQ, DIFF_HEADS, 2, DIFF_DK)
    new_diff_v = vf1.reshape(BATCH, 1, SEQ, DIFF_HEADS, DIFF_DV)
    return (y_prompt, y_sample, new_mla_ckv, new_mla_krope, new_diff_k, new_diff_v)
```

```python
import functools
import math

import jax
import jax.numpy as jnp
from jax import lax
from jax.experimental import pallas as pl
from jax.experimental.pallas import tpu as pltpu

F32 = jnp.float32
BF16 = jnp.bfloat16

D_MODEL = 1024
BATCH = 16
SEQ = 256
DEC_BATCH = 8
DEC_SEQ = 2048
PAST_LEN = 256
GRID_W = 64
ROPE_BASE = 10000.0
EPS = 1e-6

CONV_CH = 512
CONV_WIDTH = 31
CONV_PAD = 15
MLA_HEADS = 8
MLA_NOPE = 64
MLA_ROPE = 32
MLA_QK = 96
MLA_V = 64
MLA_Q_RANK = 384
MLA_KV_RANK = 256
DIFF_HEADS = 8
DIFF_DK = 64
DIFF_DV = 128
N_GROUPS = 4
EXPERTS_PER_GROUP = 8
N_EXPERTS = 32
EXPERT_FF = 256

LANES = 128
SUBLANES = 8
N_CTX = BATCH * SEQ
N_LAT = DEC_BATCH * DEC_SEQ
N_TOK = N_CTX + N_LAT
N_CACHE = DEC_BATCH * PAST_LEN
MOD_ROWS = 16
VMEM_LIMIT = 48 * 1024 * 1024

TM = 512
NB_CTX = N_CTX // TM
NB_TOK = N_TOK // TM
BLK_PER_LAT = DEC_SEQ // TM
TQ = 256
CONV_T = 256
CONV_HALO = 16
CONV_RB = 32
CONV_SHIFT_ROWS = CONV_T + 2 * CONV_HALO - SUBLANES


def _params(*sem):
    return pltpu.CompilerParams(dimension_semantics=sem, vmem_limit_bytes=VMEM_LIMIT)


def _sigmoid(x):
    return 1.0 / (1.0 + jnp.exp(-x))


def _rms(x, g):
    return x * lax.rsqrt(jnp.mean(x * x, axis=-1, keepdims=True) + EPS) * g


def _dot(a, b):
    return jnp.dot(a, b, preferred_element_type=F32)


def _dot_nt(a, b):
    return lax.dot_general(a, b, (((1,), (1,)), ((), ())), preferred_element_type=F32)


def _seq_of_block(i):
    return jnp.where(i < NB_CTX, 0, 1 + (i - NB_CTX) // BLK_PER_LAT)


def _mod_spec(layer, slot):
    return pl.BlockSpec((1, 1, D_MODEL), lambda i, *_: ((layer * MOD_ROWS + _seq_of_block(i)) * 6 + slot, 0, 0))


def _row_spec(width, tm=TM):
    return pl.BlockSpec((tm, width), lambda i, *_: (i, 0))


def _ctx_spec(width):
    return pl.BlockSpec((TM, width), lambda i, *_: (jnp.minimum(i, NB_CTX - 1), 0))


def _lat_spec(width):
    return pl.BlockSpec((TM, width), lambda i, *_: (jnp.maximum(i - NB_CTX, 0), 0))


def _split_specs(width):
    return [_ctx_spec(width), _lat_spec(width)]


def _pick(ctx_ref, lat_ref):
    return jnp.where(pl.program_id(0) < NB_CTX, ctx_ref[...], lat_ref[...])


def _ones_blocks(group_log2):
    r = lax.broadcasted_iota(jnp.int32, (LANES, LANES), 0) >> group_log2
    c = lax.broadcasted_iota(jnp.int32, (LANES, LANES), 1) >> group_log2
    return jnp.where(r == c, 1.0, 0.0).astype(BF16)


def _full_spec(shape):
    nd = len(shape)
    return pl.BlockSpec(shape, lambda *_: (0,) * nd)


def _mod_kernel(c_ref, w_ref, b_ref, o_ref):
    c = c_ref[...]
    s = (c * _sigmoid(c)).astype(BF16)
    o_ref[0] = _dot(s, w_ref[0].astype(BF16)) + b_ref[0]


def _mod_call(cond, ada_w, ada_b):
    depth = ada_w.shape[0]
    tn = 1024
    return pl.pallas_call(
        _mod_kernel,
        grid=(depth, 6 * D_MODEL // tn),
        in_specs=[
            pl.BlockSpec((MOD_ROWS, D_MODEL), lambda l, j: (0, 0)),
            pl.BlockSpec((1, D_MODEL, tn), lambda l, j: (l, 0, j)),
            pl.BlockSpec((1, 1, tn), lambda l, j: (l, 0, j)),
        ],
        out_specs=pl.BlockSpec((1, MOD_ROWS, tn), lambda l, j: (l, 0, j)),
        out_shape=jax.ShapeDtypeStruct((depth, MOD_ROWS, 6 * D_MODEL), F32),
        compiler_params=_params("parallel", "parallel"),
        name="adaln_mod",
    )(cond, ada_w, ada_b.reshape(depth, 1, 6 * D_MODEL))


EV_W = 1792


def _inproj0_kernel(xc_ref, xl_ref, g_ref, sh_ref, sc_ref, w_ref, qg_ref, kvg_ref, u_ref, qn_ref, ckv_ref, kr_ref):
    h = _rms(_pick(xc_ref, xl_ref), g_ref[...]) * (1.0 + sc_ref[0]) + sh_ref[0]
    p = _dot(h.astype(BF16), w_ref[...])
    u_ref[...] = p[:, :CONV_CH] * _sigmoid(p[:, CONV_CH:2 * CONV_CH])
    o1 = 2 * CONV_CH
    o2 = o1 + MLA_Q_RANK
    o3 = o2 + MLA_KV_RANK
    qn_ref[...] = _rms(p[:, o1:o2], qg_ref[...]).astype(BF16)
    ckv_ref[...] = _rms(p[:, o2:o3], kvg_ref[...])
    kr_ref[...] = p[:, o3:o3 + LANES]


def _inproj0_call(x_ctx, x_lat, mod, g1, w_in, qg, kvg):
    return pl.pallas_call(
        _inproj0_kernel,
        grid=(NB_TOK,),
        in_specs=_split_specs(D_MODEL) + [
            _full_spec((1, D_MODEL)), _mod_spec(0, 0), _mod_spec(0, 1),
            _full_spec((D_MODEL, EV_W)), _full_spec((1, MLA_Q_RANK)), _full_spec((1, MLA_KV_RANK)),
        ],
        out_specs=[_row_spec(CONV_CH), _row_spec(MLA_Q_RANK), _row_spec(MLA_KV_RANK), _row_spec(LANES)],
        out_shape=[
            jax.ShapeDtypeStruct((N_TOK, CONV_CH), F32),
            jax.ShapeDtypeStruct((N_TOK, MLA_Q_RANK), BF16),
            jax.ShapeDtypeStruct((N_TOK, MLA_KV_RANK), F32),
            jax.ShapeDtypeStruct((N_TOK, LANES), F32),
        ],
        compiler_params=_params("parallel"),
        name="inproj0",
    )(x_ctx, x_lat, g1, mod, mod, w_in, qg, kvg)


CONV_CHUNKS_PER_LAT = DEC_SEQ // CONV_T
CONV_NB_CTX = N_CTX // CONV_T
CONV_NB = N_TOK // CONV_T


def _conv_kernel(prev_ref, cur_ref, nxt_ref, w_ref, b_ref, lg_ref, lb_ref, o_ref, buf, shifted):
    i = pl.program_id(0)
    j = (i - CONV_NB_CTX) % CONV_CHUNKS_PER_LAT
    first = jnp.logical_or(i < CONV_NB_CTX, j == 0)
    last = jnp.logical_or(i < CONV_NB_CTX, j == CONV_CHUNKS_PER_LAT - 1)
    buf[0:CONV_HALO, :] = jnp.where(first, 0.0, prev_ref[...])
    buf[CONV_HALO:CONV_HALO + CONV_T, :] = cur_ref[...]
    buf[CONV_HALO + CONV_T:, :] = jnp.where(last, 0.0, nxt_ref[...])
    for s in range(1, SUBLANES):
        shifted[s - 1] = buf[s:s + CONV_SHIFT_ROWS, :]
    for rb in range(CONV_T // CONV_RB):
        acc = jnp.zeros((CONV_RB, CONV_CH), F32)
        for k in range(CONV_WIDTH):
            row = rb * CONV_RB + CONV_HALO - CONV_PAD + k
            s, r0 = row % SUBLANES, row - row % SUBLANES
            src = buf[r0:r0 + CONV_RB, :] if s == 0 else shifted[s - 1, r0:r0 + CONV_RB, :]
            acc = acc + w_ref[k:k + 1, :] * src
        y = acc + b_ref[...]
        mu = jnp.mean(y, axis=-1, keepdims=True)
        yc = y - mu
        var = jnp.mean(yc * yc, axis=-1, keepdims=True)
        z = yc * lax.rsqrt(var + EPS) * lg_ref[...] + lb_ref[...]
        o_ref[rb * CONV_RB:(rb + 1) * CONV_RB, :] = (z * _sigmoid(z)).astype(BF16)


def _conv_call(u, w, b, lg, lb):
    hb = CONV_T // CONV_HALO
    n_halo_blocks = N_TOK // CONV_HALO
    return pl.pallas_call(
        _conv_kernel,
        grid=(CONV_NB,),
        in_specs=[
            pl.BlockSpec((CONV_HALO, CONV_CH), lambda i: (jnp.maximum(i * hb - 1, 0), 0)),
            pl.BlockSpec((CONV_T, CONV_CH), lambda i: (i, 0)),
            pl.BlockSpec((CONV_HALO, CONV_CH), lambda i: (jnp.minimum((i + 1) * hb, n_halo_blocks - 1), 0)),
            _full_spec((CONV_WIDTH, CONV_CH)), _full_spec((1, CONV_CH)),
            _full_spec((1, CONV_CH)), _full_spec((1, CONV_CH)),
        ],
        out_specs=pl.BlockSpec((CONV_T, CONV_CH), lambda i: (i, 0)),
        out_shape=jax.ShapeDtypeStruct((N_TOK, CONV_CH), BF16),
        scratch_shapes=[pltpu.VMEM((CONV_T + 2 * CONV_HALO, CONV_CH), F32),
                        pltpu.VMEM((SUBLANES - 1, CONV_SHIFT_ROWS, CONV_CH), F32)],
        compiler_params=_params("parallel"),
        name="conformer_conv",
    )(u, u, u, w, b, lg, lb)


def _rope_tile(x, c_ref, s1_ref, s2_ref, half):
    up = pltpu.roll(x, LANES - half, 1)
    dn = pltpu.roll(x, half, 1)
    return x * c_ref[...] + up * s1_ref[...] + dn * s2_ref[...]


def _rope_tables(rot_dim, lane_starts, n_tok):
    rows = n_tok // GRID_W
    row_ids = jnp.repeat(jnp.arange(rows, dtype=F32), GRID_W)
    col_ids = jnp.tile(jnp.arange(GRID_W, dtype=F32), rows)
    quarter = rot_dim // 4
    half = rot_dim // 2
    freqs = jnp.power(ROPE_BASE, -jnp.arange(quarter, dtype=F32) / quarter)
    ang = jnp.concatenate([row_ids[:, None] * freqs, col_ids[:, None] * freqs], axis=-1)
    cos, sin = jnp.cos(ang), jnp.sin(ang)
    one = lambda n: jnp.ones((n_tok, n), F32)
    zero = lambda n: jnp.zeros((n_tok, n), F32)
    c, s1, s2, at = [], [], [], 0
    for st in lane_starts:
        c += [one(st - at), cos, cos]
        s1 += [zero(st - at), -sin, zero(half)]
        s2 += [zero(st - at), zero(half), sin]
        at = st + rot_dim
    c, s1, s2 = (jnp.concatenate(t + [fill(LANES - at)], axis=1) for t, fill in ((c, one), (s1, zero), (s2, zero)))
    ident = (jnp.ones((TM, LANES), F32), jnp.zeros((TM, LANES), F32), jnp.zeros((TM, LANES), F32))
    return tuple(jnp.concatenate([t, e], axis=0) for t, e in zip((c, s1, s2), ident))


def _rope_block(i):
    return jnp.where(i < NB_CTX, BLK_PER_LAT, (i - NB_CTX) % BLK_PER_LAT)


def _mla_prep_kernel(qn_ref, ckv_ref, kr_ref, wq_ref, wk_ref, wv_ref, qg_ref, kg_ref, c_ref, s1_ref, s2_ref,
                     q_ref, k_ref, v_ref):
    qf = _dot(qn_ref[...], wq_ref[...])
    ckv = ckv_ref[...].astype(BF16)
    kf = _dot(ckv, wk_ref[...])
    v_ref[...] = _dot(ckv, wv_ref[...]).astype(BF16)
    kr = kr_ref[...]
    half = MLA_ROPE // 2
    ones = _ones_blocks(LANES.bit_length() - 1)
    for h in range(MLA_HEADS):
        sl = slice(h * LANES, (h + 1) * LANES)
        qh = _group_rms(qf[:, sl], qg_ref, ones, MLA_QK)
        q_ref[:, sl] = _rope_tile(qh, c_ref, s1_ref, s2_ref, half).astype(BF16)
        kh = _group_rms(kf[:, sl] + kr, kg_ref, ones, MLA_QK)
        k_ref[:, sl] = _rope_tile(kh, c_ref, s1_ref, s2_ref, half).astype(BF16)


def _mla_prep_call(qn, ckv, kr, wq, wk, wv, qg, kg, tables, rope_block_fn):
    n = qn.shape[0]
    hw = MLA_HEADS * LANES
    tspec = pl.BlockSpec((TM, LANES), lambda i: (rope_block_fn(i), 0))
    return pl.pallas_call(
        _mla_prep_kernel,
        grid=(n // TM,),
        in_specs=[
            _row_spec(MLA_Q_RANK), _row_spec(MLA_KV_RANK), _row_spec(LANES),
            _full_spec((MLA_Q_RANK, hw)), _full_spec((MLA_KV_RANK, hw)), _full_spec((MLA_KV_RANK, MLA_HEADS * MLA_V)),
            _full_spec((1, LANES)), _full_spec((1, LANES)), tspec, tspec, tspec,
        ],
        out_specs=[_row_spec(hw), _row_spec(hw), _row_spec(MLA_HEADS * MLA_V)],
        out_shape=[
            jax.ShapeDtypeStruct((n, hw), BF16),
            jax.ShapeDtypeStruct((n, hw), BF16),
            jax.ShapeDtypeStruct((n, MLA_HEADS * MLA_V), BF16),
        ],
        compiler_params=_params("parallel"),
        name="mla_prep",
    )(qn, ckv, kr, wq, wk, wv, qg, kg, *tables)


LOG2E = 1.4426950408889634
EXP2_SAFE = 100.0
BF16_NORM_MARGIN = 1.02


def _max_sq_norm(x, mask=None):
    xf = x.astype(F32)
    xx = xf * xf
    if mask is not None:
        xx = jnp.where(mask, xx, 0.0)
    return jnp.max(jnp.sum(xx, axis=-1, keepdims=True), axis=0, keepdims=True)


def _gain_sq_bound(g_ref, dims):
    g = g_ref[...]
    return jnp.max(g * g, axis=-1, keepdims=True) * (dims * BF16_NORM_MARGIN)


def _exps(s_list, shifted):
    if shifted:
        m = s_list[0].max(axis=-1, keepdims=True)
        for s in s_list[1:]:
            m = jnp.maximum(m, s.max(axis=-1, keepdims=True))
        es = [jnp.exp2(s - m) for s in s_list]
    else:
        es = [jnp.exp2(s) for s in s_list]
    l = es[0].sum(axis=-1, keepdims=True)
    for e in es[1:]:
        l = l + e.sum(axis=-1, keepdims=True)
    return es, l


def _run_query_blocks(bound_sq, n_blocks, tq, units_per_block, logits, finish):
    per_trip = 2 if n_blocks % 2 == 0 else 1

    def trip(shifted, t, carry):
        units = [(pl.multiple_of((t * per_trip + b) * tq, tq), u)
                 for b in range(per_trip) for u in range(units_per_block)]
        pending = logits(*units[0])
        for k, unit in enumerate(units):
            nxt = logits(*units[k + 1]) if k + 1 < len(units) else None
            finish(shifted, *unit, pending)
            pending = nxt
        return carry

    safe = bound_sq[0, 0] <= EXP2_SAFE * EXP2_SAFE

    @pl.when(safe)
    def _():
        lax.fori_loop(0, n_blocks // per_trip, functools.partial(trip, False), 0)

    @pl.when(jnp.logical_not(safe))
    def _():
        lax.fori_loop(0, n_blocks // per_trip, functools.partial(trip, True), 0)


def _mla_attn_kernel(has_cache, tq, n_pairs, qg_ref, kg_ref, *refs):
    if has_cache:
        q_ref, k_ref, v_ref, kc_ref, vc_ref, o_ref = refs
        k_refs, v_refs = [k_ref, kc_ref], [v_ref, vc_ref]
    else:
        q_ref, k_ref, v_ref, o_ref = refs
        k_refs, v_refs = [k_ref], [v_ref]
    bound_sq = _gain_sq_bound(qg_ref, MLA_QK) * _gain_sq_bound(kg_ref, MLA_QK)

    def logits(r0, head):
        sl = slice(head * LANES, (head + 1) * LANES)
        q = q_ref[pl.ds(r0, tq), sl]
        return [_dot_nt(q, kr[:, sl]) for kr in k_refs]

    held = []

    def finish(shifted, r0, head, s_list):
        vsl = slice(head // 2 * LANES, (head // 2 + 1) * LANES)
        es, l = _exps(s_list, shifted)
        o = _dot(es[0].astype(BF16), v_refs[0][:, vsl])
        for e, vr in zip(es[1:], v_refs[1:]):
            o = o + _dot(e.astype(BF16), vr[:, vsl])
        o = o / l
        if head % 2 == 0:
            held.append(o)
        else:
            lane = lax.broadcasted_iota(jnp.int32, o.shape, 1)
            o_ref[pl.ds(r0, tq), vsl] = jnp.where(lane < MLA_V, held.pop(), o).astype(BF16)

    _run_query_blocks(bound_sq, q_ref.shape[0] // tq, tq, 2 * n_pairs, logits, finish)


def _mla_attn_ctx_call(qg, kg, q, k, v):
    return pl.pallas_call(
        functools.partial(_mla_attn_kernel, False, SEQ, MLA_HEADS // 2),
        grid=(BATCH,),
        in_specs=[
            _full_spec((1, LANES)), _full_spec((1, LANES)),
            pl.BlockSpec((SEQ, MLA_HEADS * LANES), lambda b: (b, 0)),
            pl.BlockSpec((SEQ, MLA_HEADS * LANES), lambda b: (b, 0)),
            pl.BlockSpec((SEQ, MLA_HEADS * MLA_V), lambda b: (b, 0)),
        ],
        out_specs=pl.BlockSpec((SEQ, MLA_HEADS * MLA_V), lambda b: (b, 0)),
        out_shape=jax.ShapeDtypeStruct((N_CTX, MLA_HEADS * MLA_V), BF16),
        compiler_params=_params("parallel"),
        name="mla_attn_ctx",
    )(qg, kg, q, k, v)


def _mla_attn_lat_call(qg, kg, q, k, v, kc, vc):
    k0 = N_CTX // DEC_SEQ
    return pl.pallas_call(
        functools.partial(_mla_attn_kernel, True, TQ, 1),
        grid=(DEC_BATCH, MLA_HEADS // 2),
        in_specs=[
            _full_spec((1, LANES)), _full_spec((1, LANES)),
            pl.BlockSpec((DEC_SEQ, 2 * LANES), lambda b, hp: (k0 + b, hp)),
            pl.BlockSpec((DEC_SEQ, 2 * LANES), lambda b, hp: (k0 + b, hp)),
            pl.BlockSpec((DEC_SEQ, LANES), lambda b, hp: (k0 + b, hp)),
            pl.BlockSpec((PAST_LEN, 2 * LANES), lambda b, hp: (b, hp)),
            pl.BlockSpec((PAST_LEN, LANES), lambda b, hp: (b, hp)),
        ],
        out_specs=pl.BlockSpec((DEC_SEQ, LANES), lambda b, hp: (b, hp)),
        out_shape=jax.ShapeDtypeStruct((N_LAT, MLA_HEADS * MLA_V), BF16),
        compiler_params=_params("parallel", "parallel"),
        name="mla_attn_lat",
    )(qg, kg, q, k, v, kc, vc)


def _route(logits):
    lane = lax.broadcasted_iota(jnp.int32, logits.shape, 1)
    lane_f = lane.astype(F32)
    big = float(LANES)
    neg = -jnp.inf
    gmask = jnp.logical_and(lane >= N_EXPERTS, lane < N_EXPERTS + N_GROUPS)
    gl = jnp.where(gmask, logits, neg)
    ge = jnp.exp(gl - jnp.max(gl, axis=-1, keepdims=True))
    gp = ge / jnp.sum(ge, axis=-1, keepdims=True)
    gate = jnp.max(gp, axis=-1, keepdims=True)
    gidx = jnp.min(jnp.where(jnp.logical_and(gmask, gp == gate), lane_f, big), axis=-1, keepdims=True) - N_EXPERTS
    lo = gidx * EXPERTS_PER_GROUP
    emask = jnp.logical_and(lane_f >= lo, lane_f < lo + EXPERTS_PER_GROUP)
    el = jnp.where(emask, logits, neg)
    ee = jnp.exp(el - jnp.max(el, axis=-1, keepdims=True))
    ep = jnp.where(emask, ee / jnp.sum(ee, axis=-1, keepdims=True), -1.0)
    p1 = jnp.max(ep, axis=-1, keepdims=True)
    i1 = jnp.min(jnp.where(ep == p1, lane_f, big), axis=-1, keepdims=True)
    ep2 = jnp.where(lane_f == i1, -1.0, ep)
    p2 = jnp.max(ep2, axis=-1, keepdims=True)
    i2 = jnp.min(jnp.where(ep2 == p2, lane_f, big), axis=-1, keepdims=True)
    den = p1 + p2
    return jnp.where(lane_f == i1, gate * p1 / den, 0.0) + jnp.where(lane_f == i2, gate * p2 / den, 0.0)


def _outproj_kernel(arity, *refs):
    refs = list(refs)
    rows = []
    for n in arity:
        rows.append(refs[0][...] if n == 1 else _pick(refs[0], refs[1]))
        refs = refs[n:]
    n_act = len(arity) - 1
    w_refs, refs = refs[:n_act], refs[n_act:]
    g1_ref, n2_ref, sh2_ref, sc2_ref, rh_ref, rl_ref, rb_ref, xo_ref, h_ref, cw_ref, cnt_ref = refs
    out = _dot(rows[0], w_refs[0][...])
    for a, w_ref in zip(rows[1:n_act], w_refs[1:]):
        out = out + _dot(a, w_ref[...])
    x = rows[n_act] + g1_ref[0] * out
    xo_ref[...] = x
    h = _rms(x, n2_ref[...]) * (1.0 + sc2_ref[0]) + sh2_ref[0]
    h_hi = h.astype(BF16)
    h_ref[...] = h_hi
    h_lo = (h - h_hi.astype(F32)).astype(BF16)
    logits = _dot(h_hi, rh_ref[...]) + (_dot(h_hi, rl_ref[...]) + _dot(h_lo, rh_ref[...])) + rb_ref[...]
    cw = _route(logits)
    cw_ref[...] = cw
    cnt_ref[0] = jnp.sum(jnp.where(cw > 0.0, 1.0, 0.0), axis=0, keepdims=True)


def _outproj_call(layer, acts, ws, x, mod, n2, r_hi, r_lo, r_b):
    operands = list(acts) + [x]
    row_specs = []
    for op in operands:
        row_specs += [_row_spec(op[0].shape[1])] if len(op) == 1 else _split_specs(op[0].shape[1])
    flat = [a for op in operands for a in op]
    return pl.pallas_call(
        functools.partial(_outproj_kernel, tuple(len(op) for op in operands)),
        grid=(NB_TOK,),
        in_specs=(
            row_specs + [_full_spec(w.shape) for w in ws]
            + [_mod_spec(layer, 2), _full_spec((1, D_MODEL)), _mod_spec(layer, 3),
               _mod_spec(layer, 4), _full_spec((D_MODEL, LANES)), _full_spec((D_MODEL, LANES)), _full_spec((1, LANES))]
        ),
        out_specs=[_row_spec(D_MODEL), _row_spec(D_MODEL), _row_spec(LANES),
                   pl.BlockSpec((1, 1, LANES), lambda i: (i, 0, 0))],
        out_shape=[
            jax.ShapeDtypeStruct((N_TOK, D_MODEL), F32),
            jax.ShapeDtypeStruct((N_TOK, D_MODEL), BF16),
            jax.ShapeDtypeStruct((N_TOK, LANES), F32),
            jax.ShapeDtypeStruct((NB_TOK, 1, LANES), F32),
        ],
        compiler_params=_params("parallel"),
        name=f"outproj{layer}",
    )(*flat, *ws, mod, n2, mod, mod, r_hi, r_lo, r_b)


MOE_CHUNK = 16
MOE_TILE = 512
NW = N_TOK // TM
R_LOC = 2 * TM + N_EXPERTS * MOE_CHUNK
R_SORTED_MAX = 2 * N_TOK + NW * N_EXPERTS * (MOE_CHUNK - 1) + N_EXPERTS * (MOE_TILE - 1)
NT_MAX = -(-R_SORTED_MAX // MOE_TILE)
P_MAX = R_LOC // MOE_CHUNK
PIECE_GROUP = 8
TRASH_ROW = NT_MAX * MOE_TILE
SORTED_ROWS = TRASH_ROW + PIECE_GROUP * MOE_CHUNK


def _moe_tables(cnt):
    cnt = cnt[:, 0, :N_EXPERTS].astype(jnp.int32)
    padc = (cnt + (MOE_CHUNK - 1)) // MOE_CHUNK * MOE_CHUNK
    loff = jnp.cumsum(padc, axis=1) - padc
    tot = jnp.sum(padc, axis=0)
    seg = (tot + (MOE_TILE - 1)) // MOE_TILE * MOE_TILE
    seg_end = jnp.cumsum(seg)
    base = seg_end - seg
    goff = base[None, :] + jnp.cumsum(padc, axis=0) - padc
    tile_start = jnp.arange(NT_MAX, dtype=jnp.int32) * MOE_TILE
    texp = jnp.minimum(jnp.sum((tile_start[:, None] >= seg_end[None, :]).astype(jnp.int32), axis=1), N_EXPERTS - 1)
    nvalid = (seg_end[-1:] // MOE_TILE).astype(jnp.int32)
    p_row = jnp.arange(P_MAX, dtype=jnp.int32)[None, :, None] * MOE_CHUNK
    inside = jnp.logical_and(p_row >= loff[:, None, :], p_row < (loff + padc)[:, None, :])
    grow = jnp.sum(jnp.where(inside, goff[:, None, :] + p_row - loff[:, None, :], 0), axis=-1)
    used = jnp.any(inside, axis=-1)
    filler = TRASH_ROW + (jnp.arange(P_MAX, dtype=jnp.int32) % PIECE_GROUP)[None, :] * MOE_CHUNK
    npieces = jnp.sum(padc, axis=1) // MOE_CHUNK
    i32 = lambda a: a.reshape(-1).astype(jnp.int32)
    return dict(dst=i32(jnp.where(used, grow, filler)), src=i32(jnp.where(used, grow, 0)),
                ngroups=i32((npieces + (PIECE_GROUP - 1)) // PIECE_GROUP), tail_start=i32(base + tot),
                tail_chunks=i32((seg - tot) // MOE_CHUNK), texp=texp, nvalid=nvalid)


def _local_positions(cw):
    n = cw.shape[0]
    a = cw > 0.0
    a_f = jnp.where(a, 1.0, 0.0)
    below = jnp.where(lax.broadcasted_iota(jnp.int32, (n, n), 1) < lax.broadcasted_iota(jnp.int32, (n, n), 0), 1.0, 0.0)
    rank = _dot(below.astype(BF16), a_f.astype(BF16))
    cnt = jnp.sum(a_f, axis=0, keepdims=True)
    padc = jnp.floor((cnt + (MOE_CHUNK - 1)) * (1.0 / MOE_CHUNK)) * MOE_CHUNK
    before = jnp.where(lax.broadcasted_iota(jnp.int32, (LANES, LANES), 0) < lax.broadcasted_iota(jnp.int32, (LANES, LANES), 1),
                       1.0, 0.0)
    loff = _dot(jnp.broadcast_to(padc, (8, LANES)).astype(BF16), before.astype(BF16))[0:1]
    lpos = rank + loff
    pa = jnp.min(jnp.where(a, lpos, float(R_LOC)), axis=-1, keepdims=True)
    pb = jnp.max(jnp.where(a, lpos, -1.0), axis=-1, keepdims=True)
    wa = jnp.sum(jnp.where(jnp.logical_and(a, lpos == pa), cw, 0.0), axis=-1, keepdims=True)
    wb = jnp.sum(jnp.where(jnp.logical_and(a, lpos == pb), cw, 0.0), axis=-1, keepdims=True)
    return pa, pb, wa, wb


def _for_each_piece(w, ngroups_ref, grow_ref, fn):
    n = ngroups_ref[w]

    def group(j, carry):
        for u in range(PIECE_GROUP):
            p = j * PIECE_GROUP + u
            fn(pl.multiple_of(p * MOE_CHUNK, MOE_CHUNK), pl.multiple_of(grow_ref[w * P_MAX + p], MOE_CHUNK))
        return carry

    lax.fori_loop(0, n, group, 0)
    return n


def _dispatch_kernel(ngroups_ref, dst_ref, tstart_ref, tchunk_ref, h_ref, cw_ref, hs_ref, buf, zbuf, sems):
    w = pl.program_id(0)
    slot = w % 2
    pa, pb, _, _ = _local_positions(cw_ref[...])
    lane = lax.broadcasted_iota(jnp.int32, (TM, LANES), 1)
    pos_t = jnp.where(lane == 0, pa, jnp.where(lane == 1, pb, 0.0)).T
    r = lax.broadcasted_iota(jnp.int32, (R_LOC, TM), 0).astype(F32)
    sel = jnp.logical_or(r == pos_t[0:1, :], r == pos_t[1:2, :])
    buf[slot] = _dot(jnp.where(sel, 1.0, 0.0).astype(BF16), h_ref[...]).astype(BF16)

    def wait_groups(window, s):
        def wait_group(c, carry):
            pltpu.make_async_copy(buf.at[s, pl.ds(0, PIECE_GROUP * MOE_CHUNK)],
                                  hs_ref.at[pl.ds(0, PIECE_GROUP * MOE_CHUNK)], sems.at[s]).wait()
            return carry

        lax.fori_loop(0, ngroups_ref[window], wait_group, 0)

    @pl.when(w > 0)
    def _():
        wait_groups(w - 1, 1 - slot)

    def piece_copy(local_row, global_row):
        return pltpu.make_async_copy(buf.at[slot, pl.ds(local_row, MOE_CHUNK)],
                                     hs_ref.at[pl.ds(global_row, MOE_CHUNK)], sems.at[slot])

    _for_each_piece(w, ngroups_ref, dst_ref, lambda lo, go: piece_copy(lo, go).start())

    @pl.when(w == NW - 1)
    def _():
        wait_groups(w, slot)
        sem = sems.at[slot]
        zbuf[...] = jnp.zeros_like(zbuf)

        def zero_copy(global_row):
            return pltpu.make_async_copy(zbuf, hs_ref.at[pl.ds(global_row, MOE_CHUNK)], sem)

        def per_expert(e, total):
            n = tchunk_ref[e]
            st = tstart_ref[e]

            def piece(c, carry):
                zero_copy(pl.multiple_of(st + c * MOE_CHUNK, MOE_CHUNK)).start()
                return carry

            lax.fori_loop(0, n, piece, 0)
            return total + n

        total_z = lax.fori_loop(0, N_EXPERTS, per_expert, 0)

        def wait_zero(c, carry):
            zero_copy(0).wait()
            return carry

        lax.fori_loop(0, total_z, wait_zero, 0)


def _dispatch_call(layer, h, cw, tb):
    return pl.pallas_call(
        _dispatch_kernel,
        grid_spec=pltpu.PrefetchScalarGridSpec(
            num_scalar_prefetch=4,
            grid=(NW,),
            in_specs=[_row_spec(D_MODEL), _row_spec(LANES)],
            out_specs=pl.BlockSpec(memory_space=pl.ANY),
            scratch_shapes=[pltpu.VMEM((2, R_LOC, D_MODEL), BF16), pltpu.VMEM((MOE_CHUNK, D_MODEL), BF16),
                            pltpu.SemaphoreType.DMA((2,))],
        ),
        out_shape=jax.ShapeDtypeStruct((SORTED_ROWS, D_MODEL), BF16),
        compiler_params=_params("arbitrary"),
        name=f"moe_dispatch{layer}",
    )(tb["ngroups"], tb["dst"], tb["tail_start"], tb["tail_chunks"], h, cw)


def _expert_kernel(texp_ref, nvalid_ref, hs_ref, wg_ref, wu_ref, wd_ref, ys_ref, wg_b, wu_b, wd_b):
    i = pl.program_id(0)
    fresh = jnp.logical_or(i == 0, texp_ref[i] != texp_ref[jnp.maximum(i - 1, 0)])

    @pl.when(fresh)
    def _():
        wg_b[...] = wg_ref[0].astype(BF16)
        wu_b[...] = wu_ref[0].astype(BF16)
        wd_b[...] = wd_ref[0].astype(BF16)

    @pl.when(i < nvalid_ref[0])
    def _():
        h = hs_ref[...]
        a = _dot(h, wg_b[...])
        b = _dot(h, wu_b[...])
        ys_ref[...] = _dot((a * _sigmoid(a) * b).astype(BF16), wd_b[...]).astype(BF16)


def _expert_call(layer, hs, tb, w_gate, w_up, w_down):
    tile_map = lambda i, texp, nv: (jnp.minimum(i, nv[0] - 1), 0)
    w_map = lambda i, texp, nv: (layer, texp[i], 0, 0)
    return pl.pallas_call(
        _expert_kernel,
        grid_spec=pltpu.PrefetchScalarGridSpec(
            num_scalar_prefetch=2,
            grid=(NT_MAX,),
            in_specs=[
                pl.BlockSpec((MOE_TILE, D_MODEL), tile_map),
                pl.BlockSpec((None, 1, D_MODEL, EXPERT_FF), w_map),
                pl.BlockSpec((None, 1, D_MODEL, EXPERT_FF), w_map),
                pl.BlockSpec((None, 1, EXPERT_FF, D_MODEL), w_map),
            ],
            out_specs=pl.BlockSpec((MOE_TILE, D_MODEL), tile_map),
            scratch_shapes=[pltpu.VMEM((D_MODEL, EXPERT_FF), BF16), pltpu.VMEM((D_MODEL, EXPERT_FF), BF16),
                            pltpu.VMEM((EXPERT_FF, D_MODEL), BF16)],
        ),
        out_shape=jax.ShapeDtypeStruct((NT_MAX * MOE_TILE, D_MODEL), BF16),
        compiler_params=_params("arbitrary"),
        name=f"moe_experts{layer}",
    )(tb["texp"], tb["nvalid"], hs, w_gate, w_up, w_down)


def _combine_kernel(split_out, ngroups_ref, src_ref, cw_ref, x_ref, g2_ref, ys_ref, *refs):
    o_refs, (buf, sems) = refs[:-2], refs[-2:]
    w = pl.program_id(0)
    slot = w % 2

    def fetch(window, s):
        def piece_copy(local_row, global_row):
            return pltpu.make_async_copy(ys_ref.at[pl.ds(global_row, MOE_CHUNK)],
                                         buf.at[s, pl.ds(local_row, MOE_CHUNK)], sems.at[s])

        _for_each_piece(window, ngroups_ref, src_ref, lambda lo, go: piece_copy(lo, go).start())

    @pl.when(w == 0)
    def _():
        buf[...] = jnp.zeros_like(buf)
        fetch(0, 0)

    @pl.when(w + 1 < NW)
    def _():
        fetch(w + 1, 1 - slot)

    pa, pb, wa, wb = _local_positions(cw_ref[...])
    r = lax.broadcasted_iota(jnp.int32, (TM, R_LOC), 1).astype(F32)
    wt = jnp.where(r == pa, wa, jnp.where(r == pb, wb, 0.0)).astype(BF16)

    def wait_group(c, carry):
        pltpu.make_async_copy(ys_ref.at[pl.ds(0, PIECE_GROUP * MOE_CHUNK)],
                              buf.at[slot, pl.ds(0, PIECE_GROUP * MOE_CHUNK)], sems.at[slot]).wait()
        return carry

    lax.fori_loop(0, ngroups_ref[w], wait_group, 0)
    y = x_ref[...] + g2_ref[0] * _dot(wt, buf[slot])
    if split_out:
        @pl.when(w < NB_CTX)
        def _():
            o_refs[0][...] = y

        @pl.when(w >= NB_CTX)
        def _():
            o_refs[1][...] = y
    else:
        o_refs[0][...] = y


def _combine_call(layer, cw, x, mod, ys, tb, split_out):
    if split_out:
        out_specs = _split_specs(D_MODEL)
        out_shape = [jax.ShapeDtypeStruct((N_CTX, D_MODEL), F32), jax.ShapeDtypeStruct((N_LAT, D_MODEL), F32)]
    else:
        out_specs = [_row_spec(D_MODEL)]
        out_shape = [jax.ShapeDtypeStruct((N_TOK, D_MODEL), F32)]
    return pl.pallas_call(
        functools.partial(_combine_kernel, split_out),
        grid_spec=pltpu.PrefetchScalarGridSpec(
            num_scalar_prefetch=2,
            grid=(NW,),
            in_specs=[_row_spec(LANES), _row_spec(D_MODEL), _mod_spec(layer, 5), pl.BlockSpec(memory_space=pl.ANY)],
            out_specs=out_specs,
            scratch_shapes=[pltpu.VMEM((2, R_LOC, D_MODEL), BF16), pltpu.SemaphoreType.DMA((2,))],
        ),
        out_shape=out_shape,
        compiler_params=_params("arbitrary"),
        name=f"moe_combine{layer}",
    )(tb["ngroups"], tb["src"], cw, x, mod, ys)


def _moe_call(layer, h, cw, cnt, w_gate, w_up, w_down, x, mod, split_out=False):
    tb = _moe_tables(cnt)
    hs = _dispatch_call(layer, h, cw, tb)
    ys = _expert_call(layer, hs, tb, w_gate, w_up, w_down)
    return _combine_call(layer, cw, x, mod, ys, tb, split_out)


def _group_rms(x, g_ref, ones_blocks, n):
    ss = _dot((x * x).astype(BF16), ones_blocks)
    return x * lax.rsqrt(ss * (1.0 / n) + EPS) * g_ref[...]


def _inproj1_kernel(x_ref, g_ref, sh_ref, sc_ref, w_ref, qg_ref, kg_ref, c_ref, s1_ref, s2_ref,
                    q_ref, k_ref, v_ref, kn_ref, vf_ref, kn_buf):
    i = pl.program_id(0)
    h = (_rms(x_ref[...], g_ref[...]) * (1.0 + sc_ref[0]) + sh_ref[0]).astype(BF16)
    hw = DIFF_HEADS * DIFF_DV
    half = DIFF_DK // 2
    ones_blocks = _ones_blocks(DIFF_DK.bit_length() - 1)
    n_pairs = DIFF_HEADS // 2

    def project(tp):
        return (_dot(h, w_ref[:, tp * 2 * LANES:(tp + 1) * 2 * LANES]),
                _dot(h, w_ref[:, hw + tp * 2 * LANES:hw + (tp + 1) * 2 * LANES]))

    nxt = project(0)
    for tp in range(n_pairs):
        pq, pk = nxt
        if tp + 1 < n_pairs:
            nxt = project(tp + 1)
        for u in range(2):
            sl = slice((2 * tp + u) * LANES, (2 * tp + u + 1) * LANES)
            qt = _group_rms(pq[:, u * LANES:(u + 1) * LANES], qg_ref, ones_blocks, DIFF_DK)
            q_ref[:, sl] = _rope_tile(qt, c_ref, s1_ref, s2_ref, half).astype(BF16)
            kt = _group_rms(pk[:, u * LANES:(u + 1) * LANES], kg_ref, ones_blocks, DIFF_DK)
            k_ref[:, sl] = _rope_tile(kt, c_ref, s1_ref, s2_ref, half).astype(BF16)
            kn_buf[:, sl] = kt

    v = _dot(h, w_ref[:, 2 * hw:])
    v_ref[...] = v.astype(BF16)

    @pl.when(i < NB_CTX)
    def _():
        kn_ref[...] = kn_buf[...]
        vf_ref[...] = v


def _inproj1_call(x, mod, g1, w_qkv, qg, kg, tables):
    hw = DIFF_HEADS * DIFF_DV
    tspec = pl.BlockSpec((TM, LANES), lambda i: (_rope_block(i), 0))
    ctx_spec = pl.BlockSpec((TM, hw), lambda i: (jnp.minimum(i, NB_CTX - 1), 0))
    return pl.pallas_call(
        _inproj1_kernel,
        grid=(NB_TOK,),
        in_specs=[
            _row_spec(D_MODEL), _full_spec((1, D_MODEL)), _mod_spec(1, 0), _mod_spec(1, 1),
            _full_spec((D_MODEL, 3 * hw)), _full_spec((1, LANES)), _full_spec((1, LANES)), tspec, tspec, tspec,
        ],
        out_specs=[_row_spec(hw), _row_spec(hw), _row_spec(hw), ctx_spec, ctx_spec],
        out_shape=[
            jax.ShapeDtypeStruct((N_TOK, hw), BF16),
            jax.ShapeDtypeStruct((N_TOK, hw), BF16),
            jax.ShapeDtypeStruct((N_TOK, hw), BF16),
            jax.ShapeDtypeStruct((N_CTX, hw), F32),
            jax.ShapeDtypeStruct((N_CTX, hw), F32),
        ],
        scratch_shapes=[pltpu.VMEM((TM, DIFF_HEADS * DIFF_DV), F32)],
        compiler_params=_params("arbitrary"),
        name="inproj1",
    )(x, g1, mod, mod, w_qkv, qg, kg, *tables)


def _diff_attn_kernel(has_cache, lambda_init, tq, n_heads, qg_ref, kg_ref, *refs):
    if has_cache:
        q_ref, k_ref, v_ref, kc_ref, vc_ref, lq1, lk1, lq2, lk2, sg_ref, o_ref = refs
        k_refs, v_refs = [k_ref, kc_ref], [v_ref, vc_ref]
    else:
        q_ref, k_ref, v_ref, lq1, lk1, lq2, lk2, sg_ref, o_ref = refs
        k_refs, v_refs = [k_ref], [v_ref]
    lam = (jnp.exp(jnp.sum(lq1[...] * lk1[...], axis=-1, keepdims=True))
           - jnp.exp(jnp.sum(lq2[...] * lk2[...], axis=-1, keepdims=True)) + lambda_init)
    lo = lax.broadcasted_iota(jnp.int32, (1, LANES), 1) < DIFF_DK
    k_sq = _gain_sq_bound(kg_ref, DIFF_DK)
    if has_cache:
        for mask in (lo, jnp.logical_not(lo)):
            k_sq = jnp.maximum(k_sq, _max_sq_norm(kc_ref[...], mask))
    bound_sq = _gain_sq_bound(qg_ref, DIFF_DK) * k_sq

    def logits(r0, head):
        sl = slice(head * LANES, (head + 1) * LANES)
        q = q_ref[pl.ds(r0, tq), sl]
        zero = jnp.zeros_like(q)
        return ([_dot_nt(jnp.where(lo, q, zero), kr[:, sl]) for kr in k_refs],
                [_dot_nt(jnp.where(lo, zero, q), kr[:, sl]) for kr in k_refs])

    def finish(shifted, r0, head, s):
        sl = slice(head * LANES, (head + 1) * LANES)
        e0, l0 = _exps(s[0], shifted)
        e1, l1 = _exps(s[1], shifted)
        f0 = 1.0 / l0
        f1 = lam / l1
        o = None
        for a, b, vr in zip(e0, e1, v_refs):
            t = _dot((a * f0 - b * f1).astype(BF16), vr[:, sl])
            o = t if o is None else o + t
        o = o * lax.rsqrt(jnp.mean(o * o, axis=-1, keepdims=True) + EPS) * sg_ref[...] * (1.0 - lambda_init)
        o_ref[pl.ds(r0, tq), sl] = o.astype(BF16)

    _run_query_blocks(bound_sq, q_ref.shape[0] // tq, tq, n_heads, logits, finish)


def _diff_attn_ctx_call(lambda_init, gains, q, k, v, lams, sg):
    lspec = _full_spec((1, DIFF_DK))
    gspec = _full_spec((1, LANES))
    return pl.pallas_call(
        functools.partial(_diff_attn_kernel, False, lambda_init, SEQ, DIFF_HEADS),
        grid=(BATCH,),
        in_specs=[gspec, gspec] + [pl.BlockSpec((SEQ, DIFF_HEADS * LANES), lambda b: (b, 0))] * 3 + [lspec] * 4
        + [_full_spec((1, DIFF_DV))],
        out_specs=pl.BlockSpec((SEQ, DIFF_HEADS * LANES), lambda b: (b, 0)),
        out_shape=jax.ShapeDtypeStruct((N_CTX, DIFF_HEADS * DIFF_DV), BF16),
        compiler_params=_params("parallel"),
        name="diff_attn_ctx",
    )(*gains, q, k, v, *lams, sg)


def _diff_attn_lat_call(lambda_init, gains, q, k, v, kc, vc, lams, sg):
    k0 = N_CTX // DEC_SEQ
    lspec = _full_spec((1, DIFF_DK))
    gspec = _full_spec((1, LANES))
    seq_spec = pl.BlockSpec((DEC_SEQ, LANES), lambda b, h: (k0 + b, h))
    c_spec = pl.BlockSpec((PAST_LEN, LANES), lambda b, h: (b, h))
    return pl.pallas_call(
        functools.partial(_diff_attn_kernel, True, lambda_init, TQ, 1),
        grid=(DEC_BATCH, DIFF_HEADS),
        in_specs=[gspec, gspec, seq_spec, seq_spec, seq_spec, c_spec, c_spec] + [lspec] * 4 + [_full_spec((1, DIFF_DV))],
        out_specs=pl.BlockSpec((DEC_SEQ, LANES), lambda b, h: (b, h)),
        out_shape=jax.ShapeDtypeStruct((N_LAT, DIFF_HEADS * DIFF_DV), BF16),
        compiler_params=_params("parallel", "parallel"),
        name="diff_attn_lat",
    )(*gains, q, k, v, kc, vc, *lams, sg)


def _router_weights(we, wg, be, bg):
    pad = LANES - N_EXPERTS - N_GROUPS
    w = jnp.concatenate([we, wg, jnp.zeros((D_MODEL, pad), F32)], axis=1)
    b = jnp.concatenate([be, bg, jnp.zeros((pad,), F32)])[None]
    hi = w.astype(BF16)
    lo = (w - hi.astype(F32)).astype(BF16)
    return hi, lo, b


def kernel(x_prompt, x_sample, cache_mla_ckv, cache_mla_krope, cache_diff_k, cache_diff_v, c, c_ctx, ada_w, ada_b, norm1_g, norm2_g, ev_w_in, ev_conv_w, ev_conv_b, ev_conv_ln_g, ev_conv_ln_b, ev_q_norm_g, ev_w_qb, ev_kv_norm_g, ev_w_kvb, ev_qn_g, ev_kn_g, ev_w_o, od_w_qkv, od_qn_g, od_kn_g, od_lambda_q1, od_lambda_k1, od_lambda_q2, od_lambda_k2, od_subln_g, od_w_o, moe_wg, moe_bg, moe_we, moe_be, moe_w_gate, moe_w_up, moe_w_down):
    x_in = (x_prompt.reshape(N_CTX, D_MODEL), x_sample.reshape(N_LAT, D_MODEL))
    cond = jnp.concatenate([c_ctx[None, :], c, jnp.zeros((MOD_ROWS - 1 - DEC_BATCH, D_MODEL), F32)], axis=0)
    mod = _mod_call(cond, ada_w, ada_b).reshape(-1, 1, D_MODEL)

    o3 = 2 * CONV_CH + MLA_Q_RANK + MLA_KV_RANK
    w_in = jnp.concatenate([ev_w_in[0][:, :o3], jnp.zeros((D_MODEL, MLA_NOPE), F32), ev_w_in[0][:, o3:],
                            jnp.zeros((D_MODEL, LANES - MLA_QK), F32)], axis=1).astype(BF16)
    u, qn, ckv, kr = _inproj0_call(*x_in, mod, norm1_g[0][None], w_in, ev_q_norm_g[0][None], ev_kv_norm_g[0][None])
    conv = _conv_call(u, ev_conv_w[0], ev_conv_b[0][None], ev_conv_ln_g[0][None], ev_conv_ln_b[0][None])

    pad_qk = ((0, 0), (0, 0), (0, LANES - MLA_QK))
    wq = jnp.pad(ev_w_qb[0].reshape(MLA_Q_RANK, MLA_HEADS, MLA_QK), pad_qk).reshape(MLA_Q_RANK, -1).astype(BF16)
    wkv = ev_w_kvb[0].reshape(MLA_KV_RANK, MLA_HEADS, MLA_NOPE + MLA_V)
    wk = jnp.pad(wkv[..., :MLA_NOPE], ((0, 0), (0, 0), (0, LANES - MLA_NOPE))).reshape(MLA_KV_RANK, -1).astype(BF16)
    wv = wkv[..., MLA_NOPE:].reshape(MLA_KV_RANK, -1).astype(BF16)
    qg = jnp.pad(ev_qn_g[0] * (MLA_QK ** -0.5 * LOG2E), (0, LANES - MLA_QK))[None]
    kg = jnp.pad(ev_kn_g[0], (0, LANES - MLA_QK))[None]
    tables = _rope_tables(MLA_ROPE, (MLA_NOPE,), DEC_SEQ)
    q0, k0, v0 = _mla_prep_call(qn, ckv, kr, wq, wk, wv, qg, kg, tables, _rope_block)
    kr_cache = jnp.pad(cache_mla_krope[:, 0].reshape(N_CACHE, MLA_ROPE), ((0, 0), (MLA_NOPE, LANES - MLA_QK)))
    _, kc0, vc0 = _mla_prep_call(jnp.zeros((N_CACHE, MLA_Q_RANK), BF16), cache_mla_ckv[:, 0].reshape(N_CACHE, MLA_KV_RANK),
                                 kr_cache, wq, wk, wv, qg, kg, tables, lambda i: BLK_PER_LAT)
    attn = (_mla_attn_ctx_call(qg, kg, q0, k0, v0), _mla_attn_lat_call(qg, kg, q0, k0, v0, kc0, vc0))

    w_o = ev_w_o[0].astype(BF16)
    r_hi, r_lo, r_b = _router_weights(moe_we[0], moe_wg[0], moe_be[0], moe_bg[0])
    x, h2, cw, cnt = _outproj_call(0, [(conv,), attn], [w_o[:CONV_CH], w_o[CONV_CH:]], x_in, mod, norm2_g[0][None], r_hi, r_lo, r_b)
    (x,) = _moe_call(0, h2, cw, cnt, moe_w_gate, moe_w_up, moe_w_down, x, mod)

    lambda_init = 0.8 - 0.6 * math.exp(-0.3 * 1)
    qg1 = jnp.tile(od_qn_g[0] * (DIFF_DK ** -0.5 * LOG2E), 2)[None]
    kg1 = jnp.tile(od_kn_g[0], 2)[None]
    tables1 = _rope_tables(DIFF_DK, (0, DIFF_DK), DEC_SEQ)
    q1, k1, v1, kn1, vf1 = _inproj1_call(x, mod, norm1_g[1][None], od_w_qkv[0].astype(BF16), qg1, kg1, tables1)
    lams = [od_lambda_q1[0][None], od_lambda_k1[0][None], od_lambda_q2[0][None], od_lambda_k2[0][None]]
    sg = od_subln_g[0][None]
    hw = DIFF_HEADS * DIFF_DV
    kc1 = cache_diff_k[:, 0].reshape(N_CACHE, hw).astype(BF16)
    vc1 = cache_diff_v[:, 0].reshape(N_CACHE, hw).astype(BF16)
    attn1 = (_diff_attn_ctx_call(lambda_init, (qg1, kg1), q1, k1, v1, lams, sg),
             _diff_attn_lat_call(lambda_init, (qg1, kg1), q1, k1, v1, kc1, vc1, lams, sg))
    r_hi, r_lo, r_b = _router_weights(moe_we[1], moe_wg[1], moe_be[1], moe_bg[1])
    x, h2, cw, cnt = _outproj_call(1, [attn1], [od_w_o[0].astype(BF16)], (x,), mod, norm2_g[1][None], r_hi, r_lo, r_b)
    y_ctx, y_lat = _moe_call(1, h2, cw, cnt, moe_w_gate, moe_w_up, moe_w_down, x, mod, split_out=True)

    y_prompt = y_ctx.reshape(BATCH, SEQ, D_MODEL)
    y_sample = y_lat.reshape(DEC_BATCH, DEC_SEQ, D_MODEL)
    new_mla_ckv = ckv[:N_CTX].reshape(BATCH, 1, SEQ, MLA_KV_RANK)
    new_mla_krope = kr[:N_CTX, MLA_NOPE:MLA_QK].reshape(BATCH, 1, SEQ, MLA_ROPE)
    new_diff_k = kn1.reshape(BATCH, 1, SEQ, DIFF_HEADS, 2, DIFF_DK)
    new_diff_v = vf1.reshape(BATCH, 1, SEQ, DIFF_HEADS, DIFF_DV)
    return (y_prompt, y_sample, new_mla_ckv, new_mla_krope, new_diff_k, new_diff_v)
```

```python
import functools
import math

import jax
import jax.numpy as jnp
from jax import lax
from jax.experimental import pallas as pl
from jax.experimental.pallas import tpu as pltpu

F32 = jnp.float32
BF16 = jnp.bfloat16

D_MODEL = 1024
BATCH = 16
SEQ = 256
DEC_BATCH = 8
DEC_SEQ = 2048
PAST_LEN = 256
GRID_W = 64
ROPE_BASE = 10000.0
EPS = 1e-6

CONV_CH = 512
CONV_WIDTH = 31
CONV_PAD = 15
MLA_HEADS = 8
MLA_NOPE = 64
MLA_ROPE = 32
MLA_QK = 96
MLA_V = 64
MLA_Q_RANK = 384
MLA_KV_RANK = 256
DIFF_HEADS = 8
DIFF_DK = 64
DIFF_DV = 128
N_GROUPS = 4
EXPERTS_PER_GROUP = 8
N_EXPERTS = 32
EXPERT_FF = 256

LANES = 128
SUBLANES = 8
N_CTX = BATCH * SEQ
N_LAT = DEC_BATCH * DEC_SEQ
N_TOK = N_CTX + N_LAT
N_CACHE = DEC_BATCH * PAST_LEN
MOD_ROWS = 16
VMEM_LIMIT = 48 * 1024 * 1024

TM = 512
NB_CTX = N_CTX // TM
NB_TOK = N_TOK // TM
BLK_PER_LAT = DEC_SEQ // TM
TQ = 256
CONV_T = 256
CONV_HALO = 16
CONV_RB = 32
CONV_SHIFT_ROWS = CONV_T + 2 * CONV_HALO - SUBLANES


def _params(*sem):
    return pltpu.CompilerParams(dimension_semantics=sem, vmem_limit_bytes=VMEM_LIMIT)


def _sigmoid(x):
    return 1.0 / (1.0 + jnp.exp(-x))


def _rms(x, g):
    return x * lax.rsqrt(jnp.mean(x * x, axis=-1, keepdims=True) + EPS) * g


def _dot(a, b):
    return jnp.dot(a, b, preferred_element_type=F32)


def _dot_nt(a, b):
    return lax.dot_general(a, b, (((1,), (1,)), ((), ())), preferred_element_type=F32)


def _seq_of_block(i):
    return jnp.where(i < NB_CTX, 0, 1 + (i - NB_CTX) // BLK_PER_LAT)


def _mod_spec(layer, slot):
    return pl.BlockSpec((1, 1, D_MODEL), lambda i, *_: ((layer * MOD_ROWS + _seq_of_block(i)) * 6 + slot, 0, 0))


def _row_spec(width, tm=TM):
    return pl.BlockSpec((tm, width), lambda i, *_: (i, 0))


def _ctx_spec(width):
    return pl.BlockSpec((TM, width), lambda i, *_: (jnp.minimum(i, NB_CTX - 1), 0))


def _lat_spec(width):
    return pl.BlockSpec((TM, width), lambda i, *_: (jnp.maximum(i - NB_CTX, 0), 0))


def _split_specs(width):
    return [_ctx_spec(width), _lat_spec(width)]


def _pick(ctx_ref, lat_ref):
    return jnp.where(pl.program_id(0) < NB_CTX, ctx_ref[...], lat_ref[...])


def _ones_blocks(group_log2):
    r = lax.broadcasted_iota(jnp.int32, (LANES, LANES), 0) >> group_log2
    c = lax.broadcasted_iota(jnp.int32, (LANES, LANES), 1) >> group_log2
    return jnp.where(r == c, 1.0, 0.0).astype(BF16)


def _full_spec(shape):
    nd = len(shape)
    return pl.BlockSpec(shape, lambda *_: (0,) * nd)


def _mod_kernel(c_ref, w_ref, b_ref, o_ref):
    c = c_ref[...]
    s = (c * _sigmoid(c)).astype(BF16)
    o_ref[0] = _dot(s, w_ref[0].astype(BF16)) + b_ref[0]


def _mod_call(cond, ada_w, ada_b):
    depth = ada_w.shape[0]
    tn = 1024
    return pl.pallas_call(
        _mod_kernel,
        grid=(depth, 6 * D_MODEL // tn),
        in_specs=[
            pl.BlockSpec((MOD_ROWS, D_MODEL), lambda l, j: (0, 0)),
            pl.BlockSpec((1, D_MODEL, tn), lambda l, j: (l, 0, j)),
            pl.BlockSpec((1, 1, tn), lambda l, j: (l, 0, j)),
        ],
        out_specs=pl.BlockSpec((1, MOD_ROWS, tn), lambda l, j: (l, 0, j)),
        out_shape=jax.ShapeDtypeStruct((depth, MOD_ROWS, 6 * D_MODEL), F32),
        compiler_params=_params("parallel", "parallel"),
        name="adaln_mod",
    )(cond, ada_w, ada_b.reshape(depth, 1, 6 * D_MODEL))


EV_W = 1792


def _inproj0_kernel(xc_ref, xl_ref, g_ref, sh_ref, sc_ref, w_ref, qg_ref, kvg_ref, u_ref, qn_ref, ckv_ref, kr_ref):
    h = _rms(_pick(xc_ref, xl_ref), g_ref[...]) * (1.0 + sc_ref[0]) + sh_ref[0]
    p = _dot(h.astype(BF16), w_ref[...])
    u_ref[...] = p[:, :CONV_CH] * _sigmoid(p[:, CONV_CH:2 * CONV_CH])
    o1 = 2 * CONV_CH
    o2 = o1 + MLA_Q_RANK
    o3 = o2 + MLA_KV_RANK
    qn_ref[...] = _rms(p[:, o1:o2], qg_ref[...]).astype(BF16)
    ckv_ref[...] = _rms(p[:, o2:o3], kvg_ref[...])
    kr_ref[...] = p[:, o3:o3 + LANES]


def _inproj0_call(x_ctx, x_lat, mod, g1, w_in, qg, kvg):
    return pl.pallas_call(
        _inproj0_kernel,
        grid=(NB_TOK,),
        in_specs=_split_specs(D_MODEL) + [
            _full_spec((1, D_MODEL)), _mod_spec(0, 0), _mod_spec(0, 1),
            _full_spec((D_MODEL, EV_W)), _full_spec((1, MLA_Q_RANK)), _full_spec((1, MLA_KV_RANK)),
        ],
        out_specs=[_row_spec(CONV_CH), _row_spec(MLA_Q_RANK), _row_spec(MLA_KV_RANK), _row_spec(LANES)],
        out_shape=[
            jax.ShapeDtypeStruct((N_TOK, CONV_CH), F32),
            jax.ShapeDtypeStruct((N_TOK, MLA_Q_RANK), BF16),
            jax.ShapeDtypeStruct((N_TOK, MLA_KV_RANK), F32),
            jax.ShapeDtypeStruct((N_TOK, LANES), F32),
        ],
        compiler_params=_params("parallel"),
        name="inproj0",
    )(x_ctx, x_lat, g1, mod, mod, w_in, qg, kvg)


CONV_CHUNKS_PER_LAT = DEC_SEQ // CONV_T
CONV_NB_CTX = N_CTX // CONV_T
CONV_NB = N_TOK // CONV_T


def _conv_kernel(prev_ref, cur_ref, nxt_ref, w_ref, b_ref, lg_ref, lb_ref, o_ref, buf, shifted):
    i = pl.program_id(0)
    j = (i - CONV_NB_CTX) % CONV_CHUNKS_PER_LAT
    first = jnp.logical_or(i < CONV_NB_CTX, j == 0)
    last = jnp.logical_or(i < CONV_NB_CTX, j == CONV_CHUNKS_PER_LAT - 1)
    buf[0:CONV_HALO, :] = jnp.where(first, 0.0, prev_ref[...])
    buf[CONV_HALO:CONV_HALO + CONV_T, :] = cur_ref[...]
    buf[CONV_HALO + CONV_T:, :] = jnp.where(last, 0.0, nxt_ref[...])
    for s in range(1, SUBLANES):
        shifted[s - 1] = buf[s:s + CONV_SHIFT_ROWS, :]
    for rb in range(CONV_T // CONV_RB):
        acc = jnp.zeros((CONV_RB, CONV_CH), F32)
        for k in range(CONV_WIDTH):
            row = rb * CONV_RB + CONV_HALO - CONV_PAD + k
            s, r0 = row % SUBLANES, row - row % SUBLANES
            src = buf[r0:r0 + CONV_RB, :] if s == 0 else shifted[s - 1, r0:r0 + CONV_RB, :]
            acc = acc + w_ref[k:k + 1, :] * src
        y = acc + b_ref[...]
        mu = jnp.mean(y, axis=-1, keepdims=True)
        yc = y - mu
        var = jnp.mean(yc * yc, axis=-1, keepdims=True)
        z = yc * lax.rsqrt(var + EPS) * lg_ref[...] + lb_ref[...]
        o_ref[rb * CONV_RB:(rb + 1) * CONV_RB, :] = (z * _sigmoid(z)).astype(BF16)


def _conv_call(u, w, b, lg, lb):
    hb = CONV_T // CONV_HALO
    n_halo_blocks = N_TOK // CONV_HALO
    return pl.pallas_call(
        _conv_kernel,
        grid=(CONV_NB,),
        in_specs=[
            pl.BlockSpec((CONV_HALO, CONV_CH), lambda i: (jnp.maximum(i * hb - 1, 0), 0)),
            pl.BlockSpec((CONV_T, CONV_CH), lambda i: (i, 0)),
            pl.BlockSpec((CONV_HALO, CONV_CH), lambda i: (jnp.minimum((i + 1) * hb, n_halo_blocks - 1), 0)),
            _full_spec((CONV_WIDTH, CONV_CH)), _full_spec((1, CONV_CH)),
            _full_spec((1, CONV_CH)), _full_spec((1, CONV_CH)),
        ],
        out_specs=pl.BlockSpec((CONV_T, CONV_CH), lambda i: (i, 0)),
        out_shape=jax.ShapeDtypeStruct((N_TOK, CONV_CH), BF16),
        scratch_shapes=[pltpu.VMEM((CONV_T + 2 * CONV_HALO, CONV_CH), F32),
                        pltpu.VMEM((SUBLANES - 1, CONV_SHIFT_ROWS, CONV_CH), F32)],
        compiler_params=_params("parallel"),
        name="conformer_conv",
    )(u, u, u, w, b, lg, lb)


def _rope_tile(x, c_ref, s1_ref, s2_ref, half):
    up = pltpu.roll(x, LANES - half, 1)
    dn = pltpu.roll(x, half, 1)
    return x * c_ref[...] + up * s1_ref[...] + dn * s2_ref[...]


def _rope_tables(rot_dim, lane_starts, n_tok):
    rows = n_tok // GRID_W
    row_ids = jnp.repeat(jnp.arange(rows, dtype=F32), GRID_W)
    col_ids = jnp.tile(jnp.arange(GRID_W, dtype=F32), rows)
    quarter = rot_dim // 4
    half = rot_dim // 2
    freqs = jnp.power(ROPE_BASE, -jnp.arange(quarter, dtype=F32) / quarter)
    ang = jnp.concatenate([row_ids[:, None] * freqs, col_ids[:, None] * freqs], axis=-1)
    cos, sin = jnp.cos(ang), jnp.sin(ang)
    one = lambda n: jnp.ones((n_tok, n), F32)
    zero = lambda n: jnp.zeros((n_tok, n), F32)
    c, s1, s2, at = [], [], [], 0
    for st in lane_starts:
        c += [one(st - at), cos, cos]
        s1 += [zero(st - at), -sin, zero(half)]
        s2 += [zero(st - at), zero(half), sin]
        at = st + rot_dim
    c, s1, s2 = (jnp.concatenate(t + [fill(LANES - at)], axis=1) for t, fill in ((c, one), (s1, zero), (s2, zero)))
    ident = (jnp.ones((TM, LANES), F32), jnp.zeros((TM, LANES), F32), jnp.zeros((TM, LANES), F32))
    return tuple(jnp.concatenate([t, e], axis=0) for t, e in zip((c, s1, s2), ident))


def _rope_block(i):
    return jnp.where(i < NB_CTX, BLK_PER_LAT, (i - NB_CTX) % BLK_PER_LAT)


def _mla_prep_kernel(qn_ref, ckv_ref, kr_ref, wq_ref, wk_ref, wv_ref, qg_ref, kg_ref, c_ref, s1_ref, s2_ref,
                     q_ref, k_ref, v_ref):
    qf = _dot(qn_ref[...], wq_ref[...])
    ckv = ckv_ref[...].astype(BF16)
    kf = _dot(ckv, wk_ref[...])
    v_ref[...] = _dot(ckv, wv_ref[...]).astype(BF16)
    kr = kr_ref[...]
    half = MLA_ROPE // 2
    ones = _ones_blocks(LANES.bit_length() - 1)
    for h in range(MLA_HEADS):
        sl = slice(h * LANES, (h + 1) * LANES)
        qh = _group_rms(qf[:, sl], qg_ref, ones, MLA_QK)
        q_ref[:, sl] = _rope_tile(qh, c_ref, s1_ref, s2_ref, half).astype(BF16)
        kh = _group_rms(kf[:, sl] + kr, kg_ref, ones, MLA_QK)
        k_ref[:, sl] = _rope_tile(kh, c_ref, s1_ref, s2_ref, half).astype(BF16)


def _mla_prep_call(qn, ckv, kr, wq, wk, wv, qg, kg, tables, rope_block_fn):
    n = qn.shape[0]
    hw = MLA_HEADS * LANES
    tspec = pl.BlockSpec((TM, LANES), lambda i: (rope_block_fn(i), 0))
    return pl.pallas_call(
        _mla_prep_kernel,
        grid=(n // TM,),
        in_specs=[
            _row_spec(MLA_Q_RANK), _row_spec(MLA_KV_RANK), _row_spec(LANES),
            _full_spec((MLA_Q_RANK, hw)), _full_spec((MLA_KV_RANK, hw)), _full_spec((MLA_KV_RANK, MLA_HEADS * MLA_V)),
            _full_spec((1, LANES)), _full_spec((1, LANES)), tspec, tspec, tspec,
        ],
        out_specs=[_row_spec(hw), _row_spec(hw), _row_spec(MLA_HEADS * MLA_V)],
        out_shape=[
            jax.ShapeDtypeStruct((n, hw), BF16),
            jax.ShapeDtypeStruct((n, hw), BF16),
            jax.ShapeDtypeStruct((n, MLA_HEADS * MLA_V), BF16),
        ],
        compiler_params=_params("parallel"),
        name="mla_prep",
    )(qn, ckv, kr, wq, wk, wv, qg, kg, *tables)


LOG2E = 1.4426950408889634
EXP2_SAFE = 100.0
BF16_NORM_MARGIN = 1.02


def _max_sq_norm(x, mask=None):
    xf = x.astype(F32)
    xx = xf * xf
    if mask is not None:
        xx = jnp.where(mask, xx, 0.0)
    return jnp.max(jnp.sum(xx, axis=-1, keepdims=True), axis=0, keepdims=True)


def _gain_sq_bound(g_ref, dims):
    g = g_ref[...]
    return jnp.max(g * g, axis=-1, keepdims=True) * (dims * BF16_NORM_MARGIN)


def _exps(s_list, shifted):
    if shifted:
        m = s_list[0].max(axis=-1, keepdims=True)
        for s in s_list[1:]:
            m = jnp.maximum(m, s.max(axis=-1, keepdims=True))
        es = [jnp.exp2(s - m) for s in s_list]
    else:
        es = [jnp.exp2(s) for s in s_list]
    l = es[0].sum(axis=-1, keepdims=True)
    for e in es[1:]:
        l = l + e.sum(axis=-1, keepdims=True)
    return es, l


def _run_query_blocks(bound_sq, n_blocks, tq, units_per_block, logits, finish):
    per_trip = 4 if n_blocks % 4 == 0 else 1

    def trip(shifted, t, carry):
        units = [(pl.multiple_of((t * per_trip + b) * tq, tq), u)
                 for b in range(per_trip) for u in range(units_per_block)]
        pending = logits(*units[0])
        for k, unit in enumerate(units):
            nxt = logits(*units[k + 1]) if k + 1 < len(units) else None
            finish(shifted, *unit, pending)
            pending = nxt
        return carry

    safe = bound_sq[0, 0] <= EXP2_SAFE * EXP2_SAFE

    @pl.when(safe)
    def _():
        lax.fori_loop(0, n_blocks // per_trip, functools.partial(trip, False), 0)

    @pl.when(jnp.logical_not(safe))
    def _():
        lax.fori_loop(0, n_blocks // per_trip, functools.partial(trip, True), 0)


def _mla_attn_kernel(has_cache, tq, n_pairs, qg_ref, kg_ref, *refs):
    if has_cache:
        q_ref, k_ref, v_ref, kc_ref, vc_ref, o_ref = refs
        k_refs, v_refs = [k_ref, kc_ref], [v_ref, vc_ref]
    else:
        q_ref, k_ref, v_ref, o_ref = refs
        k_refs, v_refs = [k_ref], [v_ref]
    bound_sq = _gain_sq_bound(qg_ref, MLA_QK) * _gain_sq_bound(kg_ref, MLA_QK)

    def logits(r0, head):
        sl = slice(head * LANES, (head + 1) * LANES)
        q = q_ref[pl.ds(r0, tq), sl]
        return [_dot_nt(q, kr[:, sl]) for kr in k_refs]

    held = []

    def finish(shifted, r0, head, s_list):
        vsl = slice(head // 2 * LANES, (head // 2 + 1) * LANES)
        es, l = _exps(s_list, shifted)
        o = _dot(es[0].astype(BF16), v_refs[0][:, vsl])
        for e, vr in zip(es[1:], v_refs[1:]):
            o = o + _dot(e.astype(BF16), vr[:, vsl])
        o = o / l
        if head % 2 == 0:
            held.append(o)
        else:
            lane = lax.broadcasted_iota(jnp.int32, o.shape, 1)
            o_ref[pl.ds(r0, tq), vsl] = jnp.where(lane < MLA_V, held.pop(), o).astype(BF16)

    _run_query_blocks(bound_sq, q_ref.shape[0] // tq, tq, 2 * n_pairs, logits, finish)


def _mla_attn_ctx_call(qg, kg, q, k, v):
    return pl.pallas_call(
        functools.partial(_mla_attn_kernel, False, SEQ, MLA_HEADS // 2),
        grid=(BATCH,),
        in_specs=[
            _full_spec((1, LANES)), _full_spec((1, LANES)),
            pl.BlockSpec((SEQ, MLA_HEADS * LANES), lambda b: (b, 0)),
            pl.BlockSpec((SEQ, MLA_HEADS * LANES), lambda b: (b, 0)),
            pl.BlockSpec((SEQ, MLA_HEADS * MLA_V), lambda b: (b, 0)),
        ],
        out_specs=pl.BlockSpec((SEQ, MLA_HEADS * MLA_V), lambda b: (b, 0)),
        out_shape=jax.ShapeDtypeStruct((N_CTX, MLA_HEADS * MLA_V), BF16),
        compiler_params=_params("parallel"),
        name="mla_attn_ctx",
    )(qg, kg, q, k, v)


def _mla_attn_lat_call(qg, kg, q, k, v, kc, vc):
    k0 = N_CTX // DEC_SEQ
    return pl.pallas_call(
        functools.partial(_mla_attn_kernel, True, TQ, 1),
        grid=(DEC_BATCH, MLA_HEADS // 2),
        in_specs=[
            _full_spec((1, LANES)), _full_spec((1, LANES)),
            pl.BlockSpec((DEC_SEQ, 2 * LANES), lambda b, hp: (k0 + b, hp)),
            pl.BlockSpec((DEC_SEQ, 2 * LANES), lambda b, hp: (k0 + b, hp)),
            pl.BlockSpec((DEC_SEQ, LANES), lambda b, hp: (k0 + b, hp)),
            pl.BlockSpec((PAST_LEN, 2 * LANES), lambda b, hp: (b, hp)),
            pl.BlockSpec((PAST_LEN, LANES), lambda b, hp: (b, hp)),
        ],
        out_specs=pl.BlockSpec((DEC_SEQ, LANES), lambda b, hp: (b, hp)),
        out_shape=jax.ShapeDtypeStruct((N_LAT, MLA_HEADS * MLA_V), BF16),
        compiler_params=_params("parallel", "parallel"),
        name="mla_attn_lat",
    )(qg, kg, q, k, v, kc, vc)


ROUTE_ROWS = 40


def _route(logits_t):
    row = lax.broadcasted_iota(jnp.int32, logits_t.shape, 0)
    row_f = row.astype(F32)
    big = float(LANES)
    neg = -jnp.inf
    gmask = jnp.logical_and(row >= N_EXPERTS, row < N_EXPERTS + N_GROUPS)
    gl = jnp.where(gmask, logits_t, neg)
    ge = jnp.exp(gl - jnp.max(gl, axis=0, keepdims=True))
    gp = ge / jnp.sum(ge, axis=0, keepdims=True)
    gate = jnp.max(gp, axis=0, keepdims=True)
    gidx = jnp.min(jnp.where(jnp.logical_and(gmask, gp == gate), row_f, big), axis=0, keepdims=True) - N_EXPERTS
    lo = gidx * EXPERTS_PER_GROUP
    emask = jnp.logical_and(row_f >= lo, row_f < lo + EXPERTS_PER_GROUP)
    el = jnp.where(emask, logits_t, neg)
    ee = jnp.exp(el - jnp.max(el, axis=0, keepdims=True))
    ep = jnp.where(emask, ee / jnp.sum(ee, axis=0, keepdims=True), -1.0)
    p1 = jnp.max(ep, axis=0, keepdims=True)
    i1 = jnp.min(jnp.where(ep == p1, row_f, big), axis=0, keepdims=True)
    ep2 = jnp.where(row_f == i1, -1.0, ep)
    p2 = jnp.max(ep2, axis=0, keepdims=True)
    i2 = jnp.min(jnp.where(ep2 == p2, row_f, big), axis=0, keepdims=True)
    den = p1 + p2
    return jnp.where(row_f == i1, gate * p1 / den, 0.0) + jnp.where(row_f == i2, gate * p2 / den, 0.0)


def _window_positions(cw_t):
    n = cw_t.shape[1]
    cw = jnp.concatenate([cw_t, jnp.zeros((LANES - ROUTE_ROWS, n), F32)], axis=0)
    a = cw > 0.0
    a_bf = jnp.where(a, 1.0, 0.0).astype(BF16)
    earlier = jnp.where(lax.broadcasted_iota(jnp.int32, (n, n), 0) < lax.broadcasted_iota(jnp.int32, (n, n), 1), 1.0, 0.0)
    rank = _dot(a_bf, earlier.astype(BF16))
    cnt_col = jnp.sum(jnp.where(a, 1.0, 0.0), axis=1, keepdims=True)
    padc = jnp.floor((cnt_col + (MOE_CHUNK - 1)) * (1.0 / MOE_CHUNK)) * MOE_CHUNK
    lower = jnp.where(lax.broadcasted_iota(jnp.int32, (LANES, LANES), 1) < lax.broadcasted_iota(jnp.int32, (LANES, LANES), 0),
                      1.0, 0.0)
    loff = _dot(lower.astype(BF16), jnp.broadcast_to(padc, (LANES, LANES)).astype(BF16))
    lpos = rank + jnp.concatenate([loff] * (n // LANES), axis=1)
    pa = jnp.min(jnp.where(a, lpos, float(R_LOC)), axis=0, keepdims=True)
    pb = jnp.max(jnp.where(a, lpos, -1.0), axis=0, keepdims=True)
    wa = jnp.sum(jnp.where(jnp.logical_and(a, lpos == pa), cw, 0.0), axis=0, keepdims=True)
    wb = jnp.sum(jnp.where(jnp.logical_and(a, lpos == pb), cw, 0.0), axis=0, keepdims=True)
    pos_t = jnp.concatenate([pa, pb, wa, wb, jnp.zeros((SUBLANES - 4, n), F32)], axis=0)
    cnt = _dot_nt(jnp.ones((SUBLANES, n), BF16), a_bf)[0:1]
    return pos_t, cnt


def _outproj_kernel(arity, *refs):
    refs = list(refs)
    rows = []
    for n in arity:
        rows.append(refs[0][...] if n == 1 else _pick(refs[0], refs[1]))
        refs = refs[n:]
    n_act = len(arity) - 1
    w_refs, refs = refs[:n_act], refs[n_act:]
    g1_ref, n2_ref, sh2_ref, sc2_ref, rh_ref, rl_ref, rb_ref, xo_ref, h_ref, pos_ref, pt_ref, cnt_ref = refs
    out = _dot(rows[0], w_refs[0][...])
    for a, w_ref in zip(rows[1:n_act], w_refs[1:]):
        out = out + _dot(a, w_ref[...])
    x = rows[n_act] + g1_ref[0] * out
    xo_ref[...] = x
    h = _rms(x, n2_ref[...]) * (1.0 + sc2_ref[0]) + sh2_ref[0]
    h_hi = h.astype(BF16)
    h_ref[...] = h_hi
    h_lo = (h - h_hi.astype(F32)).astype(BF16)
    logits = _dot(h_hi, rh_ref[...]) + (_dot(h_hi, rl_ref[...]) + _dot(h_lo, rh_ref[...])) + rb_ref[...]
    pos_t, cnt = _window_positions(_route(logits.T[:ROUTE_ROWS]))
    pt_ref[0] = pos_t
    pos_ref[...] = jnp.concatenate([pos_t, jnp.zeros((LANES - SUBLANES, pos_t.shape[1]), F32)], axis=0).T
    cnt_ref[0] = cnt


def _outproj_call(layer, acts, ws, x, mod, n2, r_hi, r_lo, r_b):
    operands = list(acts) + [x]
    row_specs = []
    for op in operands:
        row_specs += [_row_spec(op[0].shape[1])] if len(op) == 1 else _split_specs(op[0].shape[1])
    flat = [a for op in operands for a in op]
    return pl.pallas_call(
        functools.partial(_outproj_kernel, tuple(len(op) for op in operands)),
        grid=(NB_TOK,),
        in_specs=(
            row_specs + [_full_spec(w.shape) for w in ws]
            + [_mod_spec(layer, 2), _full_spec((1, D_MODEL)), _mod_spec(layer, 3),
               _mod_spec(layer, 4), _full_spec((D_MODEL, LANES)), _full_spec((D_MODEL, LANES)), _full_spec((1, LANES))]
        ),
        out_specs=[_row_spec(D_MODEL), _row_spec(D_MODEL), _row_spec(LANES),
                   pl.BlockSpec((1, SUBLANES, TM), lambda i: (i, 0, 0)), pl.BlockSpec((1, 1, LANES), lambda i: (i, 0, 0))],
        out_shape=[
            jax.ShapeDtypeStruct((N_TOK, D_MODEL), F32),
            jax.ShapeDtypeStruct((N_TOK, D_MODEL), BF16),
            jax.ShapeDtypeStruct((N_TOK, LANES), F32),
            jax.ShapeDtypeStruct((NB_TOK, SUBLANES, TM), F32),
            jax.ShapeDtypeStruct((NB_TOK, 1, LANES), F32),
        ],
        compiler_params=_params("parallel"),
        name=f"outproj{layer}",
    )(*flat, *ws, mod, n2, mod, mod, r_hi, r_lo, r_b)


MOE_CHUNK = 16
MOE_TILE = 512
NW = N_TOK // TM
R_LOC = 2 * TM + N_EXPERTS * MOE_CHUNK
R_SORTED_MAX = 2 * N_TOK + NW * N_EXPERTS * (MOE_CHUNK - 1) + N_EXPERTS * (MOE_TILE - 1)
NT_MAX = -(-R_SORTED_MAX // MOE_TILE)
P_MAX = R_LOC // MOE_CHUNK
PIECE_GROUP = 8
TRASH_ROW = NT_MAX * MOE_TILE
SORTED_ROWS = TRASH_ROW + PIECE_GROUP * MOE_CHUNK


def _moe_tables(cnt):
    cnt = cnt[:, 0, :N_EXPERTS].astype(jnp.int32)
    padc = (cnt + (MOE_CHUNK - 1)) // MOE_CHUNK * MOE_CHUNK
    loff = jnp.cumsum(padc, axis=1) - padc
    tot = jnp.sum(padc, axis=0)
    seg = (tot + (MOE_TILE - 1)) // MOE_TILE * MOE_TILE
    seg_end = jnp.cumsum(seg)
    base = seg_end - seg
    goff = base[None, :] + jnp.cumsum(padc, axis=0) - padc
    tile_start = jnp.arange(NT_MAX, dtype=jnp.int32) * MOE_TILE
    texp = jnp.minimum(jnp.sum((tile_start[:, None] >= seg_end[None, :]).astype(jnp.int32), axis=1), N_EXPERTS - 1)
    nvalid = (seg_end[-1:] // MOE_TILE).astype(jnp.int32)
    p_row = jnp.arange(P_MAX, dtype=jnp.int32)[None, :, None] * MOE_CHUNK
    inside = jnp.logical_and(p_row >= loff[:, None, :], p_row < (loff + padc)[:, None, :])
    grow = jnp.sum(jnp.where(inside, goff[:, None, :] + p_row - loff[:, None, :], 0), axis=-1)
    used = jnp.any(inside, axis=-1)
    filler = TRASH_ROW + (jnp.arange(P_MAX, dtype=jnp.int32) % PIECE_GROUP)[None, :] * MOE_CHUNK
    npieces = jnp.sum(padc, axis=1) // MOE_CHUNK
    i32 = lambda a: a.reshape(-1).astype(jnp.int32)
    return dict(dst=i32(jnp.where(used, grow, filler)), src=i32(jnp.where(used, grow, 0)),
                ngroups=i32((npieces + (PIECE_GROUP - 1)) // PIECE_GROUP), tail_start=i32(base + tot),
                tail_chunks=i32((seg - tot) // MOE_CHUNK), texp=texp, nvalid=nvalid)


def _for_each_piece(w, ngroups_ref, grow_ref, fn):
    n = ngroups_ref[w]

    def group(j, carry):
        for u in range(PIECE_GROUP):
            p = j * PIECE_GROUP + u
            fn(pl.multiple_of(p * MOE_CHUNK, MOE_CHUNK), pl.multiple_of(grow_ref[w * P_MAX + p], MOE_CHUNK))
        return carry

    lax.fori_loop(0, n, group, 0)
    return n


def _dispatch_kernel(ngroups_ref, dst_ref, tstart_ref, tchunk_ref, h_ref, pt_ref, hs_ref, buf, zbuf, sems):
    w = pl.program_id(0)
    slot = w % 2
    pos_t = pt_ref[0]
    r = lax.broadcasted_iota(jnp.int32, (R_LOC, TM), 0).astype(F32)
    sel = jnp.logical_or(r == pos_t[0:1, :], r == pos_t[1:2, :])
    buf[slot] = _dot(jnp.where(sel, 1.0, 0.0).astype(BF16), h_ref[...]).astype(BF16)

    def wait_groups(window, s):
        def wait_group(c, carry):
            pltpu.make_async_copy(buf.at[s, pl.ds(0, PIECE_GROUP * MOE_CHUNK)],
                                  hs_ref.at[pl.ds(0, PIECE_GROUP * MOE_CHUNK)], sems.at[s]).wait()
            return carry

        lax.fori_loop(0, ngroups_ref[window], wait_group, 0)

    @pl.when(w > 0)
    def _():
        wait_groups(w - 1, 1 - slot)

    def piece_copy(local_row, global_row):
        return pltpu.make_async_copy(buf.at[slot, pl.ds(local_row, MOE_CHUNK)],
                                     hs_ref.at[pl.ds(global_row, MOE_CHUNK)], sems.at[slot])

    _for_each_piece(w, ngroups_ref, dst_ref, lambda lo, go: piece_copy(lo, go).start())

    @pl.when(w == NW - 1)
    def _():
        wait_groups(w, slot)
        sem = sems.at[slot]
        zbuf[...] = jnp.zeros_like(zbuf)

        def zero_copy(global_row):
            return pltpu.make_async_copy(zbuf, hs_ref.at[pl.ds(global_row, MOE_CHUNK)], sem)

        def per_expert(e, total):
            n = tchunk_ref[e]
            st = tstart_ref[e]

            def piece(c, carry):
                zero_copy(pl.multiple_of(st + c * MOE_CHUNK, MOE_CHUNK)).start()
                return carry

            lax.fori_loop(0, n, piece, 0)
            return total + n

        total_z = lax.fori_loop(0, N_EXPERTS, per_expert, 0)

        def wait_zero(c, carry):
            zero_copy(0).wait()
            return carry

        lax.fori_loop(0, total_z, wait_zero, 0)


def _dispatch_call(layer, h, pos_t, tb):
    return pl.pallas_call(
        _dispatch_kernel,
        grid_spec=pltpu.PrefetchScalarGridSpec(
            num_scalar_prefetch=4,
            grid=(NW,),
            in_specs=[_row_spec(D_MODEL), pl.BlockSpec((1, SUBLANES, TM), lambda i, *_: (i, 0, 0))],
            out_specs=pl.BlockSpec(memory_space=pl.ANY),
            scratch_shapes=[pltpu.VMEM((2, R_LOC, D_MODEL), BF16), pltpu.VMEM((MOE_CHUNK, D_MODEL), BF16),
                            pltpu.SemaphoreType.DMA((2,))],
        ),
        out_shape=jax.ShapeDtypeStruct((SORTED_ROWS, D_MODEL), BF16),
        compiler_params=_params("arbitrary"),
        name=f"moe_dispatch{layer}",
    )(tb["ngroups"], tb["dst"], tb["tail_start"], tb["tail_chunks"], h, pos_t)


def _expert_kernel(texp_ref, nvalid_ref, hs_ref, wg_ref, wu_ref, wd_ref, ys_ref, wg_b, wu_b, wd_b):
    i = pl.program_id(0)
    fresh = jnp.logical_or(i == 0, texp_ref[i] != texp_ref[jnp.maximum(i - 1, 0)])

    @pl.when(fresh)
    def _():
        wg_b[...] = wg_ref[0].astype(BF16)
        wu_b[...] = wu_ref[0].astype(BF16)
        wd_b[...] = wd_ref[0].astype(BF16)

    @pl.when(i < nvalid_ref[0])
    def _():
        h = hs_ref[...]
        a = _dot(h, wg_b[...])
        b = _dot(h, wu_b[...])
        ys_ref[...] = _dot((a * _sigmoid(a) * b).astype(BF16), wd_b[...]).astype(BF16)


def _expert_call(layer, hs, tb, w_gate, w_up, w_down):
    tile_map = lambda i, texp, nv: (jnp.minimum(i, nv[0] - 1), 0)
    w_map = lambda i, texp, nv: (layer, texp[i], 0, 0)
    return pl.pallas_call(
        _expert_kernel,
        grid_spec=pltpu.PrefetchScalarGridSpec(
            num_scalar_prefetch=2,
            grid=(NT_MAX,),
            in_specs=[
                pl.BlockSpec((MOE_TILE, D_MODEL), tile_map),
                pl.BlockSpec((None, 1, D_MODEL, EXPERT_FF), w_map),
                pl.BlockSpec((None, 1, D_MODEL, EXPERT_FF), w_map),
                pl.BlockSpec((None, 1, EXPERT_FF, D_MODEL), w_map),
            ],
            out_specs=pl.BlockSpec((MOE_TILE, D_MODEL), tile_map),
            scratch_shapes=[pltpu.VMEM((D_MODEL, EXPERT_FF), BF16), pltpu.VMEM((D_MODEL, EXPERT_FF), BF16),
                            pltpu.VMEM((EXPERT_FF, D_MODEL), BF16)],
        ),
        out_shape=jax.ShapeDtypeStruct((NT_MAX * MOE_TILE, D_MODEL), BF16),
        compiler_params=_params("arbitrary"),
        name=f"moe_experts{layer}",
    )(tb["texp"], tb["nvalid"], hs, w_gate, w_up, w_down)


def _combine_kernel(split_out, ngroups_ref, src_ref, pos_ref, x_ref, g2_ref, ys_ref, *refs):
    o_refs, (buf, sems) = refs[:-2], refs[-2:]
    w = pl.program_id(0)
    slot = w % 2

    def fetch(window, s):
        def piece_copy(local_row, global_row):
            return pltpu.make_async_copy(ys_ref.at[pl.ds(global_row, MOE_CHUNK)],
                                         buf.at[s, pl.ds(local_row, MOE_CHUNK)], sems.at[s])

        _for_each_piece(window, ngroups_ref, src_ref, lambda lo, go: piece_copy(lo, go).start())

    @pl.when(w == 0)
    def _():
        buf[...] = jnp.zeros_like(buf)
        fetch(0, 0)

    @pl.when(w + 1 < NW)
    def _():
        fetch(w + 1, 1 - slot)

    pos = pos_ref[...]
    pa, pb, wa, wb = pos[:, 0:1], pos[:, 1:2], pos[:, 2:3], pos[:, 3:4]
    r = lax.broadcasted_iota(jnp.int32, (TM, R_LOC), 1).astype(F32)
    wt = jnp.where(r == pa, wa, jnp.where(r == pb, wb, 0.0)).astype(BF16)

    def wait_group(c, carry):
        pltpu.make_async_copy(ys_ref.at[pl.ds(0, PIECE_GROUP * MOE_CHUNK)],
                              buf.at[slot, pl.ds(0, PIECE_GROUP * MOE_CHUNK)], sems.at[slot]).wait()
        return carry

    lax.fori_loop(0, ngroups_ref[w], wait_group, 0)
    y = x_ref[...] + g2_ref[0] * _dot(wt, buf[slot])
    if split_out:
        @pl.when(w < NB_CTX)
        def _():
            o_refs[0][...] = y

        @pl.when(w >= NB_CTX)
        def _():
            o_refs[1][...] = y
    else:
        o_refs[0][...] = y


def _combine_call(layer, pos, x, mod, ys, tb, split_out):
    if split_out:
        out_specs = _split_specs(D_MODEL)
        out_shape = [jax.ShapeDtypeStruct((N_CTX, D_MODEL), F32), jax.ShapeDtypeStruct((N_LAT, D_MODEL), F32)]
    else:
        out_specs = [_row_spec(D_MODEL)]
        out_shape = [jax.ShapeDtypeStruct((N_TOK, D_MODEL), F32)]
    return pl.pallas_call(
        functools.partial(_combine_kernel, split_out),
        grid_spec=pltpu.PrefetchScalarGridSpec(
            num_scalar_prefetch=2,
            grid=(NW,),
            in_specs=[_row_spec(LANES), _row_spec(D_MODEL), _mod_spec(layer, 5), pl.BlockSpec(memory_space=pl.ANY)],
            out_specs=out_specs,
            scratch_shapes=[pltpu.VMEM((2, R_LOC, D_MODEL), BF16), pltpu.SemaphoreType.DMA((2,))],
        ),
        out_shape=out_shape,
        compiler_params=_params("arbitrary"),
        name=f"moe_combine{layer}",
    )(tb["ngroups"], tb["src"], pos, x, mod, ys)


def _moe_call(layer, h, routing, w_gate, w_up, w_down, x, mod, split_out=False):
    pos, pos_t, cnt = routing
    tb = _moe_tables(cnt)
    hs = _dispatch_call(layer, h, pos_t, tb)
    ys = _expert_call(layer, hs, tb, w_gate, w_up, w_down)
    return _combine_call(layer, pos, x, mod, ys, tb, split_out)


def _group_rms(x, g_ref, ones_blocks, n):
    ss = _dot((x * x).astype(BF16), ones_blocks)
    return x * lax.rsqrt(ss * (1.0 / n) + EPS) * g_ref[...]


def _inproj1_kernel(x_ref, g_ref, sh_ref, sc_ref, w_ref, qg_ref, kg_ref, c_ref, s1_ref, s2_ref,
                    q_ref, k_ref, v_ref, kn_ref, vf_ref, kn_buf):
    i = pl.program_id(0)
    h = (_rms(x_ref[...], g_ref[...]) * (1.0 + sc_ref[0]) + sh_ref[0]).astype(BF16)
    hw = DIFF_HEADS * DIFF_DV
    half = DIFF_DK // 2
    ones_blocks = _ones_blocks(DIFF_DK.bit_length() - 1)
    n_pairs = DIFF_HEADS // 2

    def project(tp):
        return (_dot(h, w_ref[:, tp * 2 * LANES:(tp + 1) * 2 * LANES]),
                _dot(h, w_ref[:, hw + tp * 2 * LANES:hw + (tp + 1) * 2 * LANES]))

    nxt = project(0)
    for tp in range(n_pairs):
        pq, pk = nxt
        if tp + 1 < n_pairs:
            nxt = project(tp + 1)
        for u in range(2):
            sl = slice((2 * tp + u) * LANES, (2 * tp + u + 1) * LANES)
            qt = _group_rms(pq[:, u * LANES:(u + 1) * LANES], qg_ref, ones_blocks, DIFF_DK)
            q_ref[:, sl] = _rope_tile(qt, c_ref, s1_ref, s2_ref, half).astype(BF16)
            kt = _group_rms(pk[:, u * LANES:(u + 1) * LANES], kg_ref, ones_blocks, DIFF_DK)
            k_ref[:, sl] = _rope_tile(kt, c_ref, s1_ref, s2_ref, half).astype(BF16)
            kn_buf[:, sl] = kt

    v = _dot(h, w_ref[:, 2 * hw:])
    v_ref[...] = v.astype(BF16)

    @pl.when(i < NB_CTX)
    def _():
        kn_ref[...] = kn_buf[...]
        vf_ref[...] = v


def _inproj1_call(x, mod, g1, w_qkv, qg, kg, tables):
    hw = DIFF_HEADS * DIFF_DV
    tspec = pl.BlockSpec((TM, LANES), lambda i: (_rope_block(i), 0))
    ctx_spec = pl.BlockSpec((TM, hw), lambda i: (jnp.minimum(i, NB_CTX - 1), 0))
    return pl.pallas_call(
        _inproj1_kernel,
        grid=(NB_TOK,),
        in_specs=[
            _row_spec(D_MODEL), _full_spec((1, D_MODEL)), _mod_spec(1, 0), _mod_spec(1, 1),
            _full_spec((D_MODEL, 3 * hw)), _full_spec((1, LANES)), _full_spec((1, LANES)), tspec, tspec, tspec,
        ],
        out_specs=[_row_spec(hw), _row_spec(hw), _row_spec(hw), ctx_spec, ctx_spec],
        out_shape=[
            jax.ShapeDtypeStruct((N_TOK, hw), BF16),
            jax.ShapeDtypeStruct((N_TOK, hw), BF16),
            jax.ShapeDtypeStruct((N_TOK, hw), BF16),
            jax.ShapeDtypeStruct((N_CTX, hw), F32),
            jax.ShapeDtypeStruct((N_CTX, hw), F32),
        ],
        scratch_shapes=[pltpu.VMEM((TM, DIFF_HEADS * DIFF_DV), F32)],
        compiler_params=_params("arbitrary"),
        name="inproj1",
    )(x, g1, mod, mod, w_qkv, qg, kg, *tables)


def _diff_attn_kernel(has_cache, lambda_init, tq, n_heads, qg_ref, kg_ref, *refs):
    if has_cache:
        q_ref, k_ref, v_ref, kc_ref, vc_ref, lq1, lk1, lq2, lk2, sg_ref, o_ref = refs
        k_refs, v_refs = [k_ref, kc_ref], [v_ref, vc_ref]
    else:
        q_ref, k_ref, v_ref, lq1, lk1, lq2, lk2, sg_ref, o_ref = refs
        k_refs, v_refs = [k_ref], [v_ref]
    lam = (jnp.exp(jnp.sum(lq1[...] * lk1[...], axis=-1, keepdims=True))
           - jnp.exp(jnp.sum(lq2[...] * lk2[...], axis=-1, keepdims=True)) + lambda_init)
    lo = lax.broadcasted_iota(jnp.int32, (1, LANES), 1) < DIFF_DK
    k_sq = _gain_sq_bound(kg_ref, DIFF_DK)
    if has_cache:
        for mask in (lo, jnp.logical_not(lo)):
            k_sq = jnp.maximum(k_sq, _max_sq_norm(kc_ref[...], mask))
    bound_sq = _gain_sq_bound(qg_ref, DIFF_DK) * k_sq

    def logits(r0, head):
        sl = slice(head * LANES, (head + 1) * LANES)
        q = q_ref[pl.ds(r0, tq), sl]
        zero = jnp.zeros_like(q)
        return ([_dot_nt(jnp.where(lo, q, zero), kr[:, sl]) for kr in k_refs],
                [_dot_nt(jnp.where(lo, zero, q), kr[:, sl]) for kr in k_refs])

    def finish(shifted, r0, head, s):
        sl = slice(head * LANES, (head + 1) * LANES)
        e0, l0 = _exps(s[0], shifted)
        e1, l1 = _exps(s[1], shifted)
        f0 = 1.0 / l0
        f1 = lam / l1
        o = None
        for a, b, vr in zip(e0, e1, v_refs):
            t = _dot((a * f0 - b * f1).astype(BF16), vr[:, sl])
            o = t if o is None else o + t
        o = o * lax.rsqrt(jnp.mean(o * o, axis=-1, keepdims=True) + EPS) * sg_ref[...] * (1.0 - lambda_init)
        o_ref[pl.ds(r0, tq), sl] = o.astype(BF16)

    _run_query_blocks(bound_sq, q_ref.shape[0] // tq, tq, n_heads, logits, finish)


def _diff_attn_ctx_call(lambda_init, gains, q, k, v, lams, sg):
    lspec = _full_spec((1, DIFF_DK))
    gspec = _full_spec((1, LANES))
    return pl.pallas_call(
        functools.partial(_diff_attn_kernel, False, lambda_init, SEQ, DIFF_HEADS),
        grid=(BATCH,),
        in_specs=[gspec, gspec] + [pl.BlockSpec((SEQ, DIFF_HEADS * LANES), lambda b: (b, 0))] * 3 + [lspec] * 4
        + [_full_spec((1, DIFF_DV))],
        out_specs=pl.BlockSpec((SEQ, DIFF_HEADS * LANES), lambda b: (b, 0)),
        out_shape=jax.ShapeDtypeStruct((N_CTX, DIFF_HEADS * DIFF_DV), BF16),
        compiler_params=_params("parallel"),
        name="diff_attn_ctx",
    )(*gains, q, k, v, *lams, sg)


def _diff_attn_lat_call(lambda_init, gains, q, k, v, kc, vc, lams, sg):
    k0 = N_CTX // DEC_SEQ
    lspec = _full_spec((1, DIFF_DK))
    gspec = _full_spec((1, LANES))
    seq_spec = pl.BlockSpec((DEC_SEQ, LANES), lambda b, h: (k0 + b, h))
    c_spec = pl.BlockSpec((PAST_LEN, LANES), lambda b, h: (b, h))
    return pl.pallas_call(
        functools.partial(_diff_attn_kernel, True, lambda_init, TQ, 1),
        grid=(DEC_BATCH, DIFF_HEADS),
        in_specs=[gspec, gspec, seq_spec, seq_spec, seq_spec, c_spec, c_spec] + [lspec] * 4 + [_full_spec((1, DIFF_DV))],
        out_specs=pl.BlockSpec((DEC_SEQ, LANES), lambda b, h: (b, h)),
        out_shape=jax.ShapeDtypeStruct((N_LAT, DIFF_HEADS * DIFF_DV), BF16),
        compiler_params=_params("parallel", "parallel"),
        name="diff_attn_lat",
    )(*gains, q, k, v, kc, vc, *lams, sg)


def _router_weights(we, wg, be, bg):
    pad = LANES - N_EXPERTS - N_GROUPS
    w = jnp.concatenate([we, wg, jnp.zeros((D_MODEL, pad), F32)], axis=1)
    b = jnp.concatenate([be, bg, jnp.zeros((pad,), F32)])[None]
    hi = w.astype(BF16)
    lo = (w - hi.astype(F32)).astype(BF16)
    return hi, lo, b


def kernel(x_prompt, x_sample, cache_mla_ckv, cache_mla_krope, cache_diff_k, cache_diff_v, c, c_ctx, ada_w, ada_b, norm1_g, norm2_g, ev_w_in, ev_conv_w, ev_conv_b, ev_conv_ln_g, ev_conv_ln_b, ev_q_norm_g, ev_w_qb, ev_kv_norm_g, ev_w_kvb, ev_qn_g, ev_kn_g, ev_w_o, od_w_qkv, od_qn_g, od_kn_g, od_lambda_q1, od_lambda_k1, od_lambda_q2, od_lambda_k2, od_subln_g, od_w_o, moe_wg, moe_bg, moe_we, moe_be, moe_w_gate, moe_w_up, moe_w_down):
    x_in = (x_prompt.reshape(N_CTX, D_MODEL), x_sample.reshape(N_LAT, D_MODEL))
    cond = jnp.concatenate([c_ctx[None, :], c, jnp.zeros((MOD_ROWS - 1 - DEC_BATCH, D_MODEL), F32)], axis=0)
    mod = _mod_call(cond, ada_w, ada_b).reshape(-1, 1, D_MODEL)

    o3 = 2 * CONV_CH + MLA_Q_RANK + MLA_KV_RANK
    w_in = jnp.concatenate([ev_w_in[0][:, :o3], jnp.zeros((D_MODEL, MLA_NOPE), F32), ev_w_in[0][:, o3:],
                            jnp.zeros((D_MODEL, LANES - MLA_QK), F32)], axis=1).astype(BF16)
    u, qn, ckv, kr = _inproj0_call(*x_in, mod, norm1_g[0][None], w_in, ev_q_norm_g[0][None], ev_kv_norm_g[0][None])
    conv = _conv_call(u, ev_conv_w[0], ev_conv_b[0][None], ev_conv_ln_g[0][None], ev_conv_ln_b[0][None])

    pad_qk = ((0, 0), (0, 0), (0, LANES - MLA_QK))
    wq = jnp.pad(ev_w_qb[0].reshape(MLA_Q_RANK, MLA_HEADS, MLA_QK), pad_qk).reshape(MLA_Q_RANK, -1).astype(BF16)
    wkv = ev_w_kvb[0].reshape(MLA_KV_RANK, MLA_HEADS, MLA_NOPE + MLA_V)
    wk = jnp.pad(wkv[..., :MLA_NOPE], ((0, 0), (0, 0), (0, LANES - MLA_NOPE))).reshape(MLA_KV_RANK, -1).astype(BF16)
    wv = wkv[..., MLA_NOPE:].reshape(MLA_KV_RANK, -1).astype(BF16)
    qg = jnp.pad(ev_qn_g[0] * (MLA_QK ** -0.5 * LOG2E), (0, LANES - MLA_QK))[None]
    kg = jnp.pad(ev_kn_g[0], (0, LANES - MLA_QK))[None]
    tables = _rope_tables(MLA_ROPE, (MLA_NOPE,), DEC_SEQ)
    q0, k0, v0 = _mla_prep_call(qn, ckv, kr, wq, wk, wv, qg, kg, tables, _rope_block)
    kr_cache = jnp.pad(cache_mla_krope[:, 0].reshape(N_CACHE, MLA_ROPE), ((0, 0), (MLA_NOPE, LANES - MLA_QK)))
    _, kc0, vc0 = _mla_prep_call(jnp.zeros((N_CACHE, MLA_Q_RANK), BF16), cache_mla_ckv[:, 0].reshape(N_CACHE, MLA_KV_RANK),
                                 kr_cache, wq, wk, wv, qg, kg, tables, lambda i: BLK_PER_LAT)
    attn = (_mla_attn_ctx_call(qg, kg, q0, k0, v0), _mla_attn_lat_call(qg, kg, q0, k0, v0, kc0, vc0))

    w_o = ev_w_o[0].astype(BF16)
    r_hi, r_lo, r_b = _router_weights(moe_we[0], moe_wg[0], moe_be[0], moe_bg[0])
    x, h2, *routing = _outproj_call(0, [(conv,), attn], [w_o[:CONV_CH], w_o[CONV_CH:]], x_in, mod, norm2_g[0][None], r_hi, r_lo, r_b)
    (x,) = _moe_call(0, h2, routing, moe_w_gate, moe_w_up, moe_w_down, x, mod)

    lambda_init = 0.8 - 0.6 * math.exp(-0.3 * 1)
    qg1 = jnp.tile(od_qn_g[0] * (DIFF_DK ** -0.5 * LOG2E), 2)[None]
    kg1 = jnp.tile(od_kn_g[0], 2)[None]
    tables1 = _rope_tables(DIFF_DK, (0, DIFF_DK), DEC_SEQ)
    q1, k1, v1, kn1, vf1 = _inproj1_call(x, mod, norm1_g[1][None], od_w_qkv[0].astype(BF16), qg1, kg1, tables1)
    lams = [od_lambda_q1[0][None], od_lambda_k1[0][None], od_lambda_q2[0][None], od_lambda_k2[0][None]]
    sg = od_subln_g[0][None]
    hw = DIFF_HEADS * DIFF_DV
    kc1 = cache_diff_k[:, 0].reshape(N_CACHE, hw).astype(BF16)
    vc1 = cache_diff_v[:, 0].reshape(N_CACHE, hw).astype(BF16)
    attn1 = (_diff_attn_ctx_call(lambda_init, (qg1, kg1), q1, k1, v1, lams, sg),
             _diff_attn_lat_call(lambda_init, (qg1, kg1), q1, k1, v1, kc1, vc1, lams, sg))
    r_hi, r_lo, r_b = _router_weights(moe_we[1], moe_wg[1], moe_be[1], moe_bg[1])
    x, h2, *routing = _outproj_call(1, [attn1], [od_w_o[0].astype(BF16)], (x,), mod, norm2_g[1][None], r_hi, r_lo, r_b)
    y_ctx, y_lat = _moe_call(1, h2, routing, moe_w_gate, moe_w_up, moe_w_down, x, mod, split_out=True)

    y_prompt = y_ctx.reshape(BATCH, SEQ, D_MODEL)
    y_sample = y_lat.reshape(DEC_BATCH, DEC_SEQ, D_MODEL)
    new_mla_ckv = ckv[:N_CTX].reshape(BATCH, 1, SEQ, MLA_KV_RANK)
    new_mla_krope = kr[:N_CTX, MLA_NOPE:MLA_QK].reshape(BATCH, 1, SEQ, MLA_ROPE)
    new_diff_k = kn1.reshape(BATCH, 1, SEQ, DIFF_HEADS, 2, DIFF_DK)
    new_diff_v = vf1.reshape(BATCH, 1, SEQ, DIFF_HEADS, DIFF_DV)
    return (y_prompt, y_sample, new_mla_ckv, new_mla_krope, new_diff_k, new_diff_v)
```

```python
import functools
import math

import jax
import jax.numpy as jnp
from jax import lax
from jax.experimental import pallas as pl
from jax.experimental.pallas import tpu as pltpu

F32 = jnp.float32
BF16 = jnp.bfloat16

D_MODEL = 1024
BATCH = 16
SEQ = 256
DEC_BATCH = 8
DEC_SEQ = 2048
PAST_LEN = 256
GRID_W = 64
ROPE_BASE = 10000.0
EPS = 1e-6

CONV_CH = 512
CONV_WIDTH = 31
CONV_PAD = 15
MLA_HEADS = 8
MLA_NOPE = 64
MLA_ROPE = 32
MLA_QK = 96
MLA_V = 64
MLA_Q_RANK = 384
MLA_KV_RANK = 256
DIFF_HEADS = 8
DIFF_DK = 64
DIFF_DV = 128
N_GROUPS = 4
EXPERTS_PER_GROUP = 8
N_EXPERTS = 32
EXPERT_FF = 256

LANES = 128
SUBLANES = 8
N_CTX = BATCH * SEQ
N_LAT = DEC_BATCH * DEC_SEQ
N_TOK = N_CTX + N_LAT
N_CACHE = DEC_BATCH * PAST_LEN
MOD_ROWS = 16
VMEM_LIMIT = 48 * 1024 * 1024

TM = 512
NB_CTX = N_CTX // TM
NB_TOK = N_TOK // TM
BLK_PER_LAT = DEC_SEQ // TM
TQ = 256
CONV_T = 256
CONV_HALO = 16
CONV_RB = 32
CONV_SHIFT_ROWS = CONV_T + 2 * CONV_HALO - SUBLANES


def _params(*sem):
    return pltpu.CompilerParams(dimension_semantics=sem, vmem_limit_bytes=VMEM_LIMIT)


def _sigmoid(x):
    return 1.0 / (1.0 + jnp.exp(-x))


def _rms(x, g):
    return x * lax.rsqrt(jnp.mean(x * x, axis=-1, keepdims=True) + EPS) * g


def _dot(a, b):
    return jnp.dot(a, b, preferred_element_type=F32)


def _dot_nt(a, b):
    return lax.dot_general(a, b, (((1,), (1,)), ((), ())), preferred_element_type=F32)


def _seq_of_block(i):
    return jnp.where(i < NB_CTX, 0, 1 + (i - NB_CTX) // BLK_PER_LAT)


def _mod_spec(layer, slot):
    return pl.BlockSpec((1, 1, D_MODEL), lambda i, *_: ((layer * MOD_ROWS + _seq_of_block(i)) * 6 + slot, 0, 0))


def _row_spec(width, tm=TM):
    return pl.BlockSpec((tm, width), lambda i, *_: (i, 0))


def _ctx_spec(width):
    return pl.BlockSpec((TM, width), lambda i, *_: (jnp.minimum(i, NB_CTX - 1), 0))


def _lat_spec(width):
    return pl.BlockSpec((TM, width), lambda i, *_: (jnp.maximum(i - NB_CTX, 0), 0))


def _split_specs(width):
    return [_ctx_spec(width), _lat_spec(width)]


def _pick(ctx_ref, lat_ref):
    return jnp.where(pl.program_id(0) < NB_CTX, ctx_ref[...], lat_ref[...])


def _ones_blocks(group_log2):
    r = lax.broadcasted_iota(jnp.int32, (LANES, LANES), 0) >> group_log2
    c = lax.broadcasted_iota(jnp.int32, (LANES, LANES), 1) >> group_log2
    return jnp.where(r == c, 1.0, 0.0).astype(BF16)


def _full_spec(shape):
    nd = len(shape)
    return pl.BlockSpec(shape, lambda *_: (0,) * nd)


def _mod_kernel(c_ref, w_ref, b_ref, o_ref):
    c = c_ref[...]
    s = (c * _sigmoid(c)).astype(BF16)
    o_ref[0] = _dot(s, w_ref[0].astype(BF16)) + b_ref[0]


def _mod_call(cond, ada_w, ada_b):
    depth = ada_w.shape[0]
    tn = 1024
    return pl.pallas_call(
        _mod_kernel,
        grid=(depth, 6 * D_MODEL // tn),
        in_specs=[
            pl.BlockSpec((MOD_ROWS, D_MODEL), lambda l, j: (0, 0)),
            pl.BlockSpec((1, D_MODEL, tn), lambda l, j: (l, 0, j)),
            pl.BlockSpec((1, 1, tn), lambda l, j: (l, 0, j)),
        ],
        out_specs=pl.BlockSpec((1, MOD_ROWS, tn), lambda l, j: (l, 0, j)),
        out_shape=jax.ShapeDtypeStruct((depth, MOD_ROWS, 6 * D_MODEL), F32),
        compiler_params=_params("parallel", "parallel"),
        name="adaln_mod",
    )(cond, ada_w, ada_b.reshape(depth, 1, 6 * D_MODEL))


EV_W = 1792


def _inproj0_kernel(xc_ref, xl_ref, g_ref, sh_ref, sc_ref, w_ref, qg_ref, kvg_ref, u_ref, qn_ref, ckv_ref, kr_ref):
    h = _rms(_pick(xc_ref, xl_ref), g_ref[...]) * (1.0 + sc_ref[0]) + sh_ref[0]
    p = _dot(h.astype(BF16), w_ref[...])
    u_ref[...] = p[:, :CONV_CH] * _sigmoid(p[:, CONV_CH:2 * CONV_CH])
    o1 = 2 * CONV_CH
    o2 = o1 + MLA_Q_RANK
    o3 = o2 + MLA_KV_RANK
    qn_ref[...] = _rms(p[:, o1:o2], qg_ref[...]).astype(BF16)
    ckv_ref[...] = _rms(p[:, o2:o3], kvg_ref[...])
    kr_ref[...] = p[:, o3:o3 + LANES]


def _inproj0_call(x_ctx, x_lat, mod, g1, w_in, qg, kvg):
    return pl.pallas_call(
        _inproj0_kernel,
        grid=(NB_TOK,),
        in_specs=_split_specs(D_MODEL) + [
            _full_spec((1, D_MODEL)), _mod_spec(0, 0), _mod_spec(0, 1),
            _full_spec((D_MODEL, EV_W)), _full_spec((1, MLA_Q_RANK)), _full_spec((1, MLA_KV_RANK)),
        ],
        out_specs=[_row_spec(CONV_CH), _row_spec(MLA_Q_RANK), _row_spec(MLA_KV_RANK), _row_spec(LANES)],
        out_shape=[
            jax.ShapeDtypeStruct((N_TOK, CONV_CH), F32),
            jax.ShapeDtypeStruct((N_TOK, MLA_Q_RANK), BF16),
            jax.ShapeDtypeStruct((N_TOK, MLA_KV_RANK), F32),
            jax.ShapeDtypeStruct((N_TOK, LANES), F32),
        ],
        compiler_params=_params("parallel"),
        name="inproj0",
    )(x_ctx, x_lat, g1, mod, mod, w_in, qg, kvg)


CONV_CHUNKS_PER_LAT = DEC_SEQ // CONV_T
CONV_NB_CTX = N_CTX // CONV_T
CONV_NB = N_TOK // CONV_T


def _conv_kernel(prev_ref, cur_ref, nxt_ref, w_ref, b_ref, lg_ref, lb_ref, o_ref, buf, shifted):
    i = pl.program_id(0)
    j = (i - CONV_NB_CTX) % CONV_CHUNKS_PER_LAT
    first = jnp.logical_or(i < CONV_NB_CTX, j == 0)
    last = jnp.logical_or(i < CONV_NB_CTX, j == CONV_CHUNKS_PER_LAT - 1)
    buf[0:CONV_HALO, :] = jnp.where(first, 0.0, prev_ref[...])
    buf[CONV_HALO:CONV_HALO + CONV_T, :] = cur_ref[...]
    buf[CONV_HALO + CONV_T:, :] = jnp.where(last, 0.0, nxt_ref[...])
    for s in range(1, SUBLANES):
        shifted[s - 1] = buf[s:s + CONV_SHIFT_ROWS, :]
    for rb in range(CONV_T // CONV_RB):
        acc = jnp.zeros((CONV_RB, CONV_CH), F32)
        for k in range(CONV_WIDTH):
            row = rb * CONV_RB + CONV_HALO - CONV_PAD + k
            s, r0 = row % SUBLANES, row - row % SUBLANES
            src = buf[r0:r0 + CONV_RB, :] if s == 0 else shifted[s - 1, r0:r0 + CONV_RB, :]
            acc = acc + w_ref[k:k + 1, :] * src
        y = acc + b_ref[...]
        mu = jnp.mean(y, axis=-1, keepdims=True)
        yc = y - mu
        var = jnp.mean(yc * yc, axis=-1, keepdims=True)
        z = yc * lax.rsqrt(var + EPS) * lg_ref[...] + lb_ref[...]
        o_ref[rb * CONV_RB:(rb + 1) * CONV_RB, :] = (z * _sigmoid(z)).astype(BF16)


def _conv_call(u, w, b, lg, lb):
    hb = CONV_T // CONV_HALO
    n_halo_blocks = N_TOK // CONV_HALO
    return pl.pallas_call(
        _conv_kernel,
        grid=(CONV_NB,),
        in_specs=[
            pl.BlockSpec((CONV_HALO, CONV_CH), lambda i: (jnp.maximum(i * hb - 1, 0), 0)),
            pl.BlockSpec((CONV_T, CONV_CH), lambda i: (i, 0)),
            pl.BlockSpec((CONV_HALO, CONV_CH), lambda i: (jnp.minimum((i + 1) * hb, n_halo_blocks - 1), 0)),
            _full_spec((CONV_WIDTH, CONV_CH)), _full_spec((1, CONV_CH)),
            _full_spec((1, CONV_CH)), _full_spec((1, CONV_CH)),
        ],
        out_specs=pl.BlockSpec((CONV_T, CONV_CH), lambda i: (i, 0)),
        out_shape=jax.ShapeDtypeStruct((N_TOK, CONV_CH), BF16),
        scratch_shapes=[pltpu.VMEM((CONV_T + 2 * CONV_HALO, CONV_CH), F32),
                        pltpu.VMEM((SUBLANES - 1, CONV_SHIFT_ROWS, CONV_CH), F32)],
        compiler_params=_params("parallel"),
        name="conformer_conv",
    )(u, u, u, w, b, lg, lb)


def _rope_tile(x, c_ref, s1_ref, s2_ref, half):
    up = pltpu.roll(x, LANES - half, 1)
    dn = pltpu.roll(x, half, 1)
    return x * c_ref[...] + up * s1_ref[...] + dn * s2_ref[...]


def _rope_tables(rot_dim, lane_starts, n_tok):
    rows = n_tok // GRID_W
    row_ids = jnp.repeat(jnp.arange(rows, dtype=F32), GRID_W)
    col_ids = jnp.tile(jnp.arange(GRID_W, dtype=F32), rows)
    quarter = rot_dim // 4
    half = rot_dim // 2
    freqs = jnp.power(ROPE_BASE, -jnp.arange(quarter, dtype=F32) / quarter)
    ang = jnp.concatenate([row_ids[:, None] * freqs, col_ids[:, None] * freqs], axis=-1)
    cos, sin = jnp.cos(ang), jnp.sin(ang)
    one = lambda n: jnp.ones((n_tok, n), F32)
    zero = lambda n: jnp.zeros((n_tok, n), F32)
    c, s1, s2, at = [], [], [], 0
    for st in lane_starts:
        c += [one(st - at), cos, cos]
        s1 += [zero(st - at), -sin, zero(half)]
        s2 += [zero(st - at), zero(half), sin]
        at = st + rot_dim
    c, s1, s2 = (jnp.concatenate(t + [fill(LANES - at)], axis=1) for t, fill in ((c, one), (s1, zero), (s2, zero)))
    ident = (jnp.ones((TM, LANES), F32), jnp.zeros((TM, LANES), F32), jnp.zeros((TM, LANES), F32))
    return tuple(jnp.concatenate([t, e], axis=0) for t, e in zip((c, s1, s2), ident))


def _rope_block(i):
    return jnp.where(i < NB_CTX, BLK_PER_LAT, (i - NB_CTX) % BLK_PER_LAT)


def _mla_prep_kernel(qn_ref, ckv_ref, kr_ref, wq_ref, wk_ref, wv_ref, qg_ref, kg_ref, c_ref, s1_ref, s2_ref,
                     q_ref, k_ref, v_ref):
    qn = qn_ref[...]
    ckv = ckv_ref[...].astype(BF16)
    kr = kr_ref[...]
    half = MLA_ROPE // 2
    ones = _ones_blocks(LANES.bit_length() - 1)
    n_pairs = MLA_HEADS // 2

    def project(hp):
        pair = slice(hp * 2 * LANES, (hp + 1) * 2 * LANES)
        return _dot(qn, wq_ref[:, pair]), _dot(ckv, wk_ref[:, pair])

    nxt = project(0)
    for hp in range(n_pairs):
        qf, kf = nxt
        nxt = project(hp + 1) if hp + 1 < n_pairs else None
        for u in range(2):
            sl = slice((2 * hp + u) * LANES, (2 * hp + u + 1) * LANES)
            qh = _group_rms(qf[:, u * LANES:(u + 1) * LANES], qg_ref, ones, MLA_QK)
            q_ref[:, sl] = _rope_tile(qh, c_ref, s1_ref, s2_ref, half).astype(BF16)
            kh = _group_rms(kf[:, u * LANES:(u + 1) * LANES] + kr, kg_ref, ones, MLA_QK)
            k_ref[:, sl] = _rope_tile(kh, c_ref, s1_ref, s2_ref, half).astype(BF16)
    v_ref[...] = _dot(ckv, wv_ref[...]).astype(BF16)


def _mla_prep_call(qn, ckv, kr, wq, wk, wv, qg, kg, tables, rope_block_fn):
    n = qn.shape[0]
    hw = MLA_HEADS * LANES
    tspec = pl.BlockSpec((TM, LANES), lambda i: (rope_block_fn(i), 0))
    return pl.pallas_call(
        _mla_prep_kernel,
        grid=(n // TM,),
        in_specs=[
            _row_spec(MLA_Q_RANK), _row_spec(MLA_KV_RANK), _row_spec(LANES),
            _full_spec((MLA_Q_RANK, hw)), _full_spec((MLA_KV_RANK, hw)), _full_spec((MLA_KV_RANK, MLA_HEADS * MLA_V)),
            _full_spec((1, LANES)), _full_spec((1, LANES)), tspec, tspec, tspec,
        ],
        out_specs=[_row_spec(hw), _row_spec(hw), _row_spec(MLA_HEADS * MLA_V)],
        out_shape=[
            jax.ShapeDtypeStruct((n, hw), BF16),
            jax.ShapeDtypeStruct((n, hw), BF16),
            jax.ShapeDtypeStruct((n, MLA_HEADS * MLA_V), BF16),
        ],
        compiler_params=_params("parallel"),
        name="mla_prep",
    )(qn, ckv, kr, wq, wk, wv, qg, kg, *tables)


LOG2E = 1.4426950408889634
EXP2_SAFE = 100.0
BF16_NORM_MARGIN = 1.02


def _max_sq_norm(x, mask=None):
    xf = x.astype(F32)
    xx = xf * xf
    if mask is not None:
        xx = jnp.where(mask, xx, 0.0)
    return jnp.max(jnp.sum(xx, axis=-1, keepdims=True), axis=0, keepdims=True)


def _gain_sq_bound(g_ref, dims):
    g = g_ref[...]
    return jnp.max(g * g, axis=-1, keepdims=True) * (dims * BF16_NORM_MARGIN)


def _exps(s_list, shifted):
    if shifted:
        m = s_list[0].max(axis=-1, keepdims=True)
        for s in s_list[1:]:
            m = jnp.maximum(m, s.max(axis=-1, keepdims=True))
        es = [jnp.exp2(s - m) for s in s_list]
    else:
        es = [jnp.exp2(s) for s in s_list]
    l = es[0].sum(axis=-1, keepdims=True)
    for e in es[1:]:
        l = l + e.sum(axis=-1, keepdims=True)
    return es, l


def _run_query_blocks(bound_sq, n_blocks, tq, units_per_block, logits, finish):
    per_trip = 4 if n_blocks % 4 == 0 else 1

    def trip(shifted, t, carry):
        units = [(pl.multiple_of((t * per_trip + b) * tq, tq), u)
                 for b in range(per_trip) for u in range(units_per_block)]
        pending = logits(*units[0])
        for k, unit in enumerate(units):
            nxt = logits(*units[k + 1]) if k + 1 < len(units) else None
            finish(shifted, *unit, pending)
            pending = nxt
        return carry

    safe = bound_sq[0, 0] <= EXP2_SAFE * EXP2_SAFE

    @pl.when(safe)
    def _():
        lax.fori_loop(0, n_blocks // per_trip, functools.partial(trip, False), 0)

    @pl.when(jnp.logical_not(safe))
    def _():
        lax.fori_loop(0, n_blocks // per_trip, functools.partial(trip, True), 0)


def _mla_attn_kernel(has_cache, tq, n_pairs, qg_ref, kg_ref, *refs):
    if has_cache:
        q_ref, k_ref, v_ref, kc_ref, vc_ref, o_ref = refs
        k_refs, v_refs = [k_ref, kc_ref], [v_ref, vc_ref]
    else:
        q_ref, k_ref, v_ref, o_ref = refs
        k_refs, v_refs = [k_ref], [v_ref]
    bound_sq = _gain_sq_bound(qg_ref, MLA_QK) * _gain_sq_bound(kg_ref, MLA_QK)

    def logits(r0, head):
        sl = slice(head * LANES, (head + 1) * LANES)
        q = q_ref[pl.ds(r0, tq), sl]
        return [_dot_nt(q, kr[:, sl]) for kr in k_refs]

    held = []

    def finish(shifted, r0, head, s_list):
        vsl = slice(head // 2 * LANES, (head // 2 + 1) * LANES)
        es, l = _exps(s_list, shifted)
        o = _dot(es[0].astype(BF16), v_refs[0][:, vsl])
        for e, vr in zip(es[1:], v_refs[1:]):
            o = o + _dot(e.astype(BF16), vr[:, vsl])
        o = o / l
        if head % 2 == 0:
            held.append(o)
        else:
            lane = lax.broadcasted_iota(jnp.int32, o.shape, 1)
            o_ref[pl.ds(r0, tq), vsl] = jnp.where(lane < MLA_V, held.pop(), o).astype(BF16)

    _run_query_blocks(bound_sq, q_ref.shape[0] // tq, tq, 2 * n_pairs, logits, finish)


def _mla_attn_ctx_call(qg, kg, q, k, v):
    return pl.pallas_call(
        functools.partial(_mla_attn_kernel, False, SEQ, MLA_HEADS // 2),
        grid=(BATCH,),
        in_specs=[
            _full_spec((1, LANES)), _full_spec((1, LANES)),
            pl.BlockSpec((SEQ, MLA_HEADS * LANES), lambda b: (b, 0)),
            pl.BlockSpec((SEQ, MLA_HEADS * LANES), lambda b: (b, 0)),
            pl.BlockSpec((SEQ, MLA_HEADS * MLA_V), lambda b: (b, 0)),
        ],
        out_specs=pl.BlockSpec((SEQ, MLA_HEADS * MLA_V), lambda b: (b, 0)),
        out_shape=jax.ShapeDtypeStruct((N_CTX, MLA_HEADS * MLA_V), BF16),
        compiler_params=_params("parallel"),
        name="mla_attn_ctx",
    )(qg, kg, q, k, v)


def _mla_attn_lat_call(qg, kg, q, k, v, kc, vc):
    k0 = N_CTX // DEC_SEQ
    return pl.pallas_call(
        functools.partial(_mla_attn_kernel, True, TQ, 1),
        grid=(DEC_BATCH, MLA_HEADS // 2),
        in_specs=[
            _full_spec((1, LANES)), _full_spec((1, LANES)),
            pl.BlockSpec((DEC_SEQ, 2 * LANES), lambda b, hp: (k0 + b, hp)),
            pl.BlockSpec((DEC_SEQ, 2 * LANES), lambda b, hp: (k0 + b, hp)),
            pl.BlockSpec((DEC_SEQ, LANES), lambda b, hp: (k0 + b, hp)),
            pl.BlockSpec((PAST_LEN, 2 * LANES), lambda b, hp: (b, hp)),
            pl.BlockSpec((PAST_LEN, LANES), lambda b, hp: (b, hp)),
        ],
        out_specs=pl.BlockSpec((DEC_SEQ, LANES), lambda b, hp: (b, hp)),
        out_shape=jax.ShapeDtypeStruct((N_LAT, MLA_HEADS * MLA_V), BF16),
        compiler_params=_params("parallel", "parallel"),
        name="mla_attn_lat",
    )(qg, kg, q, k, v, kc, vc)


ROUTE_ROWS = 40


def _route(logits_t):
    row = lax.broadcasted_iota(jnp.int32, logits_t.shape, 0)
    row_f = row.astype(F32)
    big = float(LANES)
    neg = -jnp.inf
    gmask = jnp.logical_and(row >= N_EXPERTS, row < N_EXPERTS + N_GROUPS)
    gl = jnp.where(gmask, logits_t, neg)
    ge = jnp.exp(gl - jnp.max(gl, axis=0, keepdims=True))
    gp = ge / jnp.sum(ge, axis=0, keepdims=True)
    gate = jnp.max(gp, axis=0, keepdims=True)
    gidx = jnp.min(jnp.where(jnp.logical_and(gmask, gp == gate), row_f, big), axis=0, keepdims=True) - N_EXPERTS
    lo = gidx * EXPERTS_PER_GROUP
    emask = jnp.logical_and(row_f >= lo, row_f < lo + EXPERTS_PER_GROUP)
    el = jnp.where(emask, logits_t, neg)
    ee = jnp.exp(el - jnp.max(el, axis=0, keepdims=True))
    ep = jnp.where(emask, ee / jnp.sum(ee, axis=0, keepdims=True), -1.0)
    p1 = jnp.max(ep, axis=0, keepdims=True)
    i1 = jnp.min(jnp.where(ep == p1, row_f, big), axis=0, keepdims=True)
    ep2 = jnp.where(row_f == i1, -1.0, ep)
    p2 = jnp.max(ep2, axis=0, keepdims=True)
    i2 = jnp.min(jnp.where(ep2 == p2, row_f, big), axis=0, keepdims=True)
    den = p1 + p2
    return jnp.where(row_f == i1, gate * p1 / den, 0.0) + jnp.where(row_f == i2, gate * p2 / den, 0.0)


def _window_positions(cw_t):
    n = cw_t.shape[1]
    cw = jnp.concatenate([cw_t, jnp.zeros((LANES - ROUTE_ROWS, n), F32)], axis=0)
    a = cw > 0.0
    a_bf = jnp.where(a, 1.0, 0.0).astype(BF16)
    earlier = jnp.where(lax.broadcasted_iota(jnp.int32, (n, n), 0) < lax.broadcasted_iota(jnp.int32, (n, n), 1), 1.0, 0.0)
    rank = _dot(a_bf, earlier.astype(BF16))
    cnt_col = jnp.sum(jnp.where(a, 1.0, 0.0), axis=1, keepdims=True)
    padc = jnp.floor((cnt_col + (MOE_CHUNK - 1)) * (1.0 / MOE_CHUNK)) * MOE_CHUNK
    lower = jnp.where(lax.broadcasted_iota(jnp.int32, (LANES, LANES), 1) < lax.broadcasted_iota(jnp.int32, (LANES, LANES), 0),
                      1.0, 0.0)
    loff = _dot(lower.astype(BF16), jnp.broadcast_to(padc, (LANES, LANES)).astype(BF16))
    lpos = rank + jnp.concatenate([loff] * (n // LANES), axis=1)
    pa = jnp.min(jnp.where(a, lpos, float(R_LOC)), axis=0, keepdims=True)
    pb = jnp.max(jnp.where(a, lpos, -1.0), axis=0, keepdims=True)
    wa = jnp.sum(jnp.where(jnp.logical_and(a, lpos == pa), cw, 0.0), axis=0, keepdims=True)
    wb = jnp.sum(jnp.where(jnp.logical_and(a, lpos == pb), cw, 0.0), axis=0, keepdims=True)
    pos_t = jnp.concatenate([pa, pb, wa, wb, jnp.zeros((SUBLANES - 4, n), F32)], axis=0)
    cnt = _dot_nt(jnp.ones((SUBLANES, n), BF16), a_bf)[0:1]
    return pos_t, cnt


def _outproj_kernel(arity, *refs):
    refs = list(refs)
    rows = []
    for n in arity:
        rows.append(refs[0][...] if n == 1 else _pick(refs[0], refs[1]))
        refs = refs[n:]
    n_act = len(arity) - 1
    w_refs, refs = refs[:n_act], refs[n_act:]
    g1_ref, n2_ref, sh2_ref, sc2_ref, rh_ref, rl_ref, rb_ref, xo_ref, h_ref, pos_ref, pt_ref, cnt_ref = refs
    out = _dot(jnp.concatenate(rows[:n_act], axis=1), jnp.concatenate([w_ref[...] for w_ref in w_refs], axis=0))
    x = rows[n_act] + g1_ref[0] * out
    xo_ref[...] = x
    h = _rms(x, n2_ref[...]) * (1.0 + sc2_ref[0]) + sh2_ref[0]
    h_hi = h.astype(BF16)
    h_ref[...] = h_hi
    h_lo = (h - h_hi.astype(F32)).astype(BF16)
    logits = _dot(h_hi, rh_ref[...]) + (_dot(h_hi, rl_ref[...]) + _dot(h_lo, rh_ref[...])) + rb_ref[...]
    pos_t, cnt = _window_positions(_route(logits.T[:ROUTE_ROWS]))
    pt_ref[0] = pos_t
    pos_ref[...] = jnp.concatenate([pos_t, jnp.zeros((LANES - SUBLANES, pos_t.shape[1]), F32)], axis=0).T
    cnt_ref[0] = cnt


def _outproj_call(layer, acts, ws, x, mod, n2, r_hi, r_lo, r_b):
    operands = list(acts) + [x]
    row_specs = []
    for op in operands:
        row_specs += [_row_spec(op[0].shape[1])] if len(op) == 1 else _split_specs(op[0].shape[1])
    flat = [a for op in operands for a in op]
    return pl.pallas_call(
        functools.partial(_outproj_kernel, tuple(len(op) for op in operands)),
        grid=(NB_TOK,),
        in_specs=(
            row_specs + [_full_spec(w.shape) for w in ws]
            + [_mod_spec(layer, 2), _full_spec((1, D_MODEL)), _mod_spec(layer, 3),
               _mod_spec(layer, 4), _full_spec((D_MODEL, LANES)), _full_spec((D_MODEL, LANES)), _full_spec((1, LANES))]
        ),
        out_specs=[_row_spec(D_MODEL), _row_spec(D_MODEL), _row_spec(LANES),
                   pl.BlockSpec((1, SUBLANES, TM), lambda i: (i, 0, 0)), pl.BlockSpec((1, 1, LANES), lambda i: (i, 0, 0))],
        out_shape=[
            jax.ShapeDtypeStruct((N_TOK, D_MODEL), F32),
            jax.ShapeDtypeStruct((N_TOK, D_MODEL), BF16),
            jax.ShapeDtypeStruct((N_TOK, LANES), F32),
            jax.ShapeDtypeStruct((NB_TOK, SUBLANES, TM), F32),
            jax.ShapeDtypeStruct((NB_TOK, 1, LANES), F32),
        ],
        compiler_params=_params("parallel"),
        name=f"outproj{layer}",
    )(*flat, *ws, mod, n2, mod, mod, r_hi, r_lo, r_b)


MOE_CHUNK = 16
MOE_TILE = 512
NW = N_TOK // TM
R_LOC = 2 * TM + N_EXPERTS * MOE_CHUNK
R_SORTED_MAX = 2 * N_TOK + NW * N_EXPERTS * (MOE_CHUNK - 1) + N_EXPERTS * (MOE_TILE - 1)
NT_MAX = -(-R_SORTED_MAX // MOE_TILE)
P_MAX = R_LOC // MOE_CHUNK
PIECE_GROUP = 8
TRASH_ROW = NT_MAX * MOE_TILE
SORTED_ROWS = TRASH_ROW + PIECE_GROUP * MOE_CHUNK


def _moe_tables(cnt):
    cnt = cnt[:, 0, :N_EXPERTS].astype(jnp.int32)
    padc = (cnt + (MOE_CHUNK - 1)) // MOE_CHUNK * MOE_CHUNK
    loff = jnp.cumsum(padc, axis=1) - padc
    tot = jnp.sum(padc, axis=0)
    seg = (tot + (MOE_TILE - 1)) // MOE_TILE * MOE_TILE
    seg_end = jnp.cumsum(seg)
    base = seg_end - seg
    goff = base[None, :] + jnp.cumsum(padc, axis=0) - padc
    tile_start = jnp.arange(NT_MAX, dtype=jnp.int32) * MOE_TILE
    texp = jnp.minimum(jnp.sum((tile_start[:, None] >= seg_end[None, :]).astype(jnp.int32), axis=1), N_EXPERTS - 1)
    nvalid = (seg_end[-1:] // MOE_TILE).astype(jnp.int32)
    p_row = jnp.arange(P_MAX, dtype=jnp.int32)[None, :, None] * MOE_CHUNK
    inside = jnp.logical_and(p_row >= loff[:, None, :], p_row < (loff + padc)[:, None, :])
    grow = jnp.sum(jnp.where(inside, goff[:, None, :] + p_row - loff[:, None, :], 0), axis=-1)
    used = jnp.any(inside, axis=-1)
    filler = TRASH_ROW + (jnp.arange(P_MAX, dtype=jnp.int32) % PIECE_GROUP)[None, :] * MOE_CHUNK
    npieces = jnp.sum(padc, axis=1) // MOE_CHUNK
    i32 = lambda a: a.reshape(-1).astype(jnp.int32)
    return dict(dst=i32(jnp.where(used, grow, filler)), src=i32(jnp.where(used, grow, 0)),
                ngroups=i32((npieces + (PIECE_GROUP - 1)) // PIECE_GROUP), tail_start=i32(base + tot),
                tail_chunks=i32((seg - tot) // MOE_CHUNK), texp=texp, nvalid=nvalid)


def _for_each_piece(w, ngroups_ref, grow_ref, fn):
    n = ngroups_ref[w]

    def group(j, carry):
        for u in range(PIECE_GROUP):
            p = j * PIECE_GROUP + u
            fn(pl.multiple_of(p * MOE_CHUNK, MOE_CHUNK), pl.multiple_of(grow_ref[w * P_MAX + p], MOE_CHUNK))
        return carry

    lax.fori_loop(0, n, group, 0)
    return n


def _dispatch_kernel(ngroups_ref, dst_ref, tstart_ref, tchunk_ref, h_ref, pt_ref, hs_ref, buf, zbuf, sems):
    w = pl.program_id(0)
    slot = w % 2
    pos_t = pt_ref[0]
    r = lax.broadcasted_iota(jnp.int32, (R_LOC, TM), 0).astype(F32)
    sel = jnp.logical_or(r == pos_t[0:1, :], r == pos_t[1:2, :])
    buf[slot] = _dot(jnp.where(sel, 1.0, 0.0).astype(BF16), h_ref[...]).astype(BF16)

    def wait_groups(window, s):
        def wait_group(c, carry):
            pltpu.make_async_copy(buf.at[s, pl.ds(0, PIECE_GROUP * MOE_CHUNK)],
                                  hs_ref.at[pl.ds(0, PIECE_GROUP * MOE_CHUNK)], sems.at[s]).wait()
            return carry

        lax.fori_loop(0, ngroups_ref[window], wait_group, 0)

    @pl.when(w > 0)
    def _():
        wait_groups(w - 1, 1 - slot)

    def piece_copy(local_row, global_row):
        return pltpu.make_async_copy(buf.at[slot, pl.ds(local_row, MOE_CHUNK)],
                                     hs_ref.at[pl.ds(global_row, MOE_CHUNK)], sems.at[slot])

    _for_each_piece(w, ngroups_ref, dst_ref, lambda lo, go: piece_copy(lo, go).start())

    @pl.when(w == NW - 1)
    def _():
        wait_groups(w, slot)
        sem = sems.at[slot]
        zbuf[...] = jnp.zeros_like(zbuf)

        def zero_copy(global_row):
            return pltpu.make_async_copy(zbuf, hs_ref.at[pl.ds(global_row, MOE_CHUNK)], sem)

        def per_expert(e, total):
            n = tchunk_ref[e]
            st = tstart_ref[e]

            def piece(c, carry):
                zero_copy(pl.multiple_of(st + c * MOE_CHUNK, MOE_CHUNK)).start()
                return carry

            lax.fori_loop(0, n, piece, 0)
            return total + n

        total_z = lax.fori_loop(0, N_EXPERTS, per_expert, 0)

        def wait_zero(c, carry):
            zero_copy(0).wait()
            return carry

        lax.fori_loop(0, total_z, wait_zero, 0)


def _dispatch_call(layer, h, pos_t, tb):
    return pl.pallas_call(
        _dispatch_kernel,
        grid_spec=pltpu.PrefetchScalarGridSpec(
            num_scalar_prefetch=4,
            grid=(NW,),
            in_specs=[_row_spec(D_MODEL), pl.BlockSpec((1, SUBLANES, TM), lambda i, *_: (i, 0, 0))],
            out_specs=pl.BlockSpec(memory_space=pl.ANY),
            scratch_shapes=[pltpu.VMEM((2, R_LOC, D_MODEL), BF16), pltpu.VMEM((MOE_CHUNK, D_MODEL), BF16),
                            pltpu.SemaphoreType.DMA((2,))],
        ),
        out_shape=jax.ShapeDtypeStruct((SORTED_ROWS, D_MODEL), BF16),
        compiler_params=_params("arbitrary"),
        name=f"moe_dispatch{layer}",
    )(tb["ngroups"], tb["dst"], tb["tail_start"], tb["tail_chunks"], h, pos_t)


def _expert_kernel(texp_ref, nvalid_ref, hs_ref, wg_ref, wu_ref, wd_ref, ys_ref, wg_b, wu_b, wd_b):
    i = pl.program_id(0)
    fresh = jnp.logical_or(i == 0, texp_ref[i] != texp_ref[jnp.maximum(i - 1, 0)])

    @pl.when(fresh)
    def _():
        wg_b[...] = wg_ref[0].astype(BF16)
        wu_b[...] = wu_ref[0].astype(BF16)
        wd_b[...] = wd_ref[0].astype(BF16)

    @pl.when(i < nvalid_ref[0])
    def _():
        h = hs_ref[...]
        a = _dot(h, wg_b[...])
        b = _dot(h, wu_b[...])
        ys_ref[...] = _dot((a * _sigmoid(a) * b).astype(BF16), wd_b[...]).astype(BF16)


def _expert_call(layer, hs, tb, w_gate, w_up, w_down):
    tile_map = lambda i, texp, nv: (jnp.minimum(i, nv[0] - 1), 0)
    w_map = lambda i, texp, nv: (layer, texp[i], 0, 0)
    return pl.pallas_call(
        _expert_kernel,
        grid_spec=pltpu.PrefetchScalarGridSpec(
            num_scalar_prefetch=2,
            grid=(NT_MAX,),
            in_specs=[
                pl.BlockSpec((MOE_TILE, D_MODEL), tile_map),
                pl.BlockSpec((None, 1, D_MODEL, EXPERT_FF), w_map),
                pl.BlockSpec((None, 1, D_MODEL, EXPERT_FF), w_map),
                pl.BlockSpec((None, 1, EXPERT_FF, D_MODEL), w_map),
            ],
            out_specs=pl.BlockSpec((MOE_TILE, D_MODEL), tile_map),
            scratch_shapes=[pltpu.VMEM((D_MODEL, EXPERT_FF), BF16), pltpu.VMEM((D_MODEL, EXPERT_FF), BF16),
                            pltpu.VMEM((EXPERT_FF, D_MODEL), BF16)],
        ),
        out_shape=jax.ShapeDtypeStruct((NT_MAX * MOE_TILE, D_MODEL), BF16),
        compiler_params=_params("arbitrary"),
        name=f"moe_experts{layer}",
    )(tb["texp"], tb["nvalid"], hs, w_gate, w_up, w_down)


def _combine_kernel(split_out, ngroups_ref, src_ref, pos_ref, x_ref, g2_ref, ys_ref, *refs):
    o_refs, (buf, sems) = refs[:-2], refs[-2:]
    w = pl.program_id(0)
    slot = w % 2

    def fetch(window, s):
        def piece_copy(local_row, global_row):
            return pltpu.make_async_copy(ys_ref.at[pl.ds(global_row, MOE_CHUNK)],
                                         buf.at[s, pl.ds(local_row, MOE_CHUNK)], sems.at[s])

        _for_each_piece(window, ngroups_ref, src_ref, lambda lo, go: piece_copy(lo, go).start())

    @pl.when(w == 0)
    def _():
        buf[...] = jnp.zeros_like(buf)
        fetch(0, 0)

    @pl.when(w + 1 < NW)
    def _():
        fetch(w + 1, 1 - slot)

    pos = pos_ref[...]
    pa, pb, wa, wb = pos[:, 0:1], pos[:, 1:2], pos[:, 2:3], pos[:, 3:4]
    r = lax.broadcasted_iota(jnp.int32, (TM, R_LOC), 1).astype(F32)
    wt = jnp.where(r == pa, wa, jnp.where(r == pb, wb, 0.0)).astype(BF16)

    def wait_group(c, carry):
        pltpu.make_async_copy(ys_ref.at[pl.ds(0, PIECE_GROUP * MOE_CHUNK)],
                              buf.at[slot, pl.ds(0, PIECE_GROUP * MOE_CHUNK)], sems.at[slot]).wait()
        return carry

    lax.fori_loop(0, ngroups_ref[w], wait_group, 0)
    y = x_ref[...] + g2_ref[0] * _dot(wt, buf[slot])
    if split_out:
        @pl.when(w < NB_CTX)
        def _():
            o_refs[0][...] = y

        @pl.when(w >= NB_CTX)
        def _():
            o_refs[1][...] = y
    else:
        o_refs[0][...] = y


def _combine_call(layer, pos, x, mod, ys, tb, split_out):
    if split_out:
        out_specs = _split_specs(D_MODEL)
        out_shape = [jax.ShapeDtypeStruct((N_CTX, D_MODEL), F32), jax.ShapeDtypeStruct((N_LAT, D_MODEL), F32)]
    else:
        out_specs = [_row_spec(D_MODEL)]
        out_shape = [jax.ShapeDtypeStruct((N_TOK, D_MODEL), F32)]
    return pl.pallas_call(
        functools.partial(_combine_kernel, split_out),
        grid_spec=pltpu.PrefetchScalarGridSpec(
            num_scalar_prefetch=2,
            grid=(NW,),
            in_specs=[_row_spec(LANES), _row_spec(D_MODEL), _mod_spec(layer, 5), pl.BlockSpec(memory_space=pl.ANY)],
            out_specs=out_specs,
            scratch_shapes=[pltpu.VMEM((2, R_LOC, D_MODEL), BF16), pltpu.SemaphoreType.DMA((2,))],
        ),
        out_shape=out_shape,
        compiler_params=_params("arbitrary"),
        name=f"moe_combine{layer}",
    )(tb["ngroups"], tb["src"], pos, x, mod, ys)


def _moe_call(layer, h, routing, w_gate, w_up, w_down, x, mod, split_out=False):
    pos, pos_t, cnt = routing
    tb = _moe_tables(cnt)
    hs = _dispatch_call(layer, h, pos_t, tb)
    ys = _expert_call(layer, hs, tb, w_gate, w_up, w_down)
    return _combine_call(layer, pos, x, mod, ys, tb, split_out)


def _group_rms(x, g_ref, ones_blocks, n):
    ss = _dot((x * x).astype(BF16), ones_blocks)
    return x * lax.rsqrt(ss * (1.0 / n) + EPS) * g_ref[...]


def _inproj1_kernel(x_ref, g_ref, sh_ref, sc_ref, w_ref, qg_ref, kg_ref, c_ref, s1_ref, s2_ref,
                    q_ref, k_ref, v_ref, kn_ref, vf_ref, kn_buf):
    i = pl.program_id(0)
    h = (_rms(x_ref[...], g_ref[...]) * (1.0 + sc_ref[0]) + sh_ref[0]).astype(BF16)
    hw = DIFF_HEADS * DIFF_DV
    half = DIFF_DK // 2
    ones_blocks = _ones_blocks(DIFF_DK.bit_length() - 1)
    n_pairs = DIFF_HEADS // 2

    def project(tp):
        return (_dot(h, w_ref[:, tp * 2 * LANES:(tp + 1) * 2 * LANES]),
                _dot(h, w_ref[:, hw + tp * 2 * LANES:hw + (tp + 1) * 2 * LANES]))

    nxt = project(0)
    for tp in range(n_pairs):
        pq, pk = nxt
        if tp + 1 < n_pairs:
            nxt = project(tp + 1)
        for u in range(2):
            sl = slice((2 * tp + u) * LANES, (2 * tp + u + 1) * LANES)
            qt = _group_rms(pq[:, u * LANES:(u + 1) * LANES], qg_ref, ones_blocks, DIFF_DK)
            q_ref[:, sl] = _rope_tile(qt, c_ref, s1_ref, s2_ref, half).astype(BF16)
            kt = _group_rms(pk[:, u * LANES:(u + 1) * LANES], kg_ref, ones_blocks, DIFF_DK)
            k_ref[:, sl] = _rope_tile(kt, c_ref, s1_ref, s2_ref, half).astype(BF16)
            kn_buf[:, sl] = kt

    v = _dot(h, w_ref[:, 2 * hw:])
    v_ref[...] = v.astype(BF16)

    @pl.when(i < NB_CTX)
    def _():
        kn_ref[...] = kn_buf[...]
        vf_ref[...] = v


def _inproj1_call(x, mod, g1, w_qkv, qg, kg, tables):
    hw = DIFF_HEADS * DIFF_DV
    tspec = pl.BlockSpec((TM, LANES), lambda i: (_rope_block(i), 0))
    ctx_spec = pl.BlockSpec((TM, hw), lambda i: (jnp.minimum(i, NB_CTX - 1), 0))
    return pl.pallas_call(
        _inproj1_kernel,
        grid=(NB_TOK,),
        in_specs=[
            _row_spec(D_MODEL), _full_spec((1, D_MODEL)), _mod_spec(1, 0), _mod_spec(1, 1),
            _full_spec((D_MODEL, 3 * hw)), _full_spec((1, LANES)), _full_spec((1, LANES)), tspec, tspec, tspec,
        ],
        out_specs=[_row_spec(hw), _row_spec(hw), _row_spec(hw), ctx_spec, ctx_spec],
        out_shape=[
            jax.ShapeDtypeStruct((N_TOK, hw), BF16),
            jax.ShapeDtypeStruct((N_TOK, hw), BF16),
            jax.ShapeDtypeStruct((N_TOK, hw), BF16),
            jax.ShapeDtypeStruct((N_CTX, hw), F32),
            jax.ShapeDtypeStruct((N_CTX, hw), F32),
        ],
        scratch_shapes=[pltpu.VMEM((TM, DIFF_HEADS * DIFF_DV), F32)],
        compiler_params=_params("arbitrary"),
        name="inproj1",
    )(x, g1, mod, mod, w_qkv, qg, kg, *tables)


def _diff_attn_kernel(has_cache, lambda_init, tq, n_heads, qg_ref, kg_ref, *refs):
    if has_cache:
        q_ref, k_ref, v_ref, kc_ref, vc_ref, lq1, lk1, lq2, lk2, sg_ref, o_ref = refs
        k_refs, v_refs = [k_ref, kc_ref], [v_ref, vc_ref]
    else:
        q_ref, k_ref, v_ref, lq1, lk1, lq2, lk2, sg_ref, o_ref = refs
        k_refs, v_refs = [k_ref], [v_ref]
    lam = (jnp.exp(jnp.sum(lq1[...] * lk1[...], axis=-1, keepdims=True))
           - jnp.exp(jnp.sum(lq2[...] * lk2[...], axis=-1, keepdims=True)) + lambda_init)
    lo = lax.broadcasted_iota(jnp.int32, (1, LANES), 1) < DIFF_DK
    k_sq = _gain_sq_bound(kg_ref, DIFF_DK)
    if has_cache:
        for mask in (lo, jnp.logical_not(lo)):
            k_sq = jnp.maximum(k_sq, _max_sq_norm(kc_ref[...], mask))
    bound_sq = _gain_sq_bound(qg_ref, DIFF_DK) * k_sq

    def logits(r0, head):
        sl = slice(head * LANES, (head + 1) * LANES)
        q = q_ref[pl.ds(r0, tq), sl]
        zero = jnp.zeros_like(q)
        return ([_dot_nt(jnp.where(lo, q, zero), kr[:, sl]) for kr in k_refs],
                [_dot_nt(jnp.where(lo, zero, q), kr[:, sl]) for kr in k_refs])

    def finish(shifted, r0, head, s):
        sl = slice(head * LANES, (head + 1) * LANES)
        e0, l0 = _exps(s[0], shifted)
        e1, l1 = _exps(s[1], shifted)
        f0 = 1.0 / l0
        f1 = lam / l1
        o = None
        for a, b, vr in zip(e0, e1, v_refs):
            t = _dot((a * f0 - b * f1).astype(BF16), vr[:, sl])
            o = t if o is None else o + t
        o = o * lax.rsqrt(jnp.mean(o * o, axis=-1, keepdims=True) + EPS) * sg_ref[...] * (1.0 - lambda_init)
        o_ref[pl.ds(r0, tq), sl] = o.astype(BF16)

    _run_query_blocks(bound_sq, q_ref.shape[0] // tq, tq, n_heads, logits, finish)


def _diff_attn_ctx_call(lambda_init, gains, q, k, v, lams, sg):
    lspec = _full_spec((1, DIFF_DK))
    gspec = _full_spec((1, LANES))
    return pl.pallas_call(
        functools.partial(_diff_attn_kernel, False, lambda_init, SEQ, DIFF_HEADS),
        grid=(BATCH,),
        in_specs=[gspec, gspec] + [pl.BlockSpec((SEQ, DIFF_HEADS * LANES), lambda b: (b, 0))] * 3 + [lspec] * 4
        + [_full_spec((1, DIFF_DV))],
        out_specs=pl.BlockSpec((SEQ, DIFF_HEADS * LANES), lambda b: (b, 0)),
        out_shape=jax.ShapeDtypeStruct((N_CTX, DIFF_HEADS * DIFF_DV), BF16),
        compiler_params=_params("parallel"),
        name="diff_attn_ctx",
    )(*gains, q, k, v, *lams, sg)


def _diff_attn_lat_call(lambda_init, gains, q, k, v, kc, vc, lams, sg):
    k0 = N_CTX // DEC_SEQ
    lspec = _full_spec((1, DIFF_DK))
    gspec = _full_spec((1, LANES))
    seq_spec = pl.BlockSpec((DEC_SEQ, LANES), lambda b, h: (k0 + b, h))
    c_spec = pl.BlockSpec((PAST_LEN, LANES), lambda b, h: (b, h))
    return pl.pallas_call(
        functools.partial(_diff_attn_kernel, True, lambda_init, TQ, 1),
        grid=(DEC_BATCH, DIFF_HEADS),
        in_specs=[gspec, gspec, seq_spec, seq_spec, seq_spec, c_spec, c_spec] + [lspec] * 4 + [_full_spec((1, DIFF_DV))],
        out_specs=pl.BlockSpec((DEC_SEQ, LANES), lambda b, h: (b, h)),
        out_shape=jax.ShapeDtypeStruct((N_LAT, DIFF_HEADS * DIFF_DV), BF16),
        compiler_params=_params("parallel", "parallel"),
        name="diff_attn_lat",
    )(*gains, q, k, v, kc, vc, *lams, sg)


def _router_weights(we, wg, be, bg):
    pad = LANES - N_EXPERTS - N_GROUPS
    w = jnp.concatenate([we, wg, jnp.zeros((D_MODEL, pad), F32)], axis=1)
    b = jnp.concatenate([be, bg, jnp.zeros((pad,), F32)])[None]
    hi = w.astype(BF16)
    lo = (w - hi.astype(F32)).astype(BF16)
    return hi, lo, b


def kernel(x_prompt, x_sample, cache_mla_ckv, cache_mla_krope, cache_diff_k, cache_diff_v, c, c_ctx, ada_w, ada_b, norm1_g, norm2_g, ev_w_in, ev_conv_w, ev_conv_b, ev_conv_ln_g, ev_conv_ln_b, ev_q_norm_g, ev_w_qb, ev_kv_norm_g, ev_w_kvb, ev_qn_g, ev_kn_g, ev_w_o, od_w_qkv, od_qn_g, od_kn_g, od_lambda_q1, od_lambda_k1, od_lambda_q2, od_lambda_k2, od_subln_g, od_w_o, moe_wg, moe_bg, moe_we, moe_be, moe_w_gate, moe_w_up, moe_w_down):
    x_in = (x_prompt.reshape(N_CTX, D_MODEL), x_sample.reshape(N_LAT, D_MODEL))
    cond = jnp.concatenate([c_ctx[None, :], c, jnp.zeros((MOD_ROWS - 1 - DEC_BATCH, D_MODEL), F32)], axis=0)
    mod = _mod_call(cond, ada_w, ada_b).reshape(-1, 1, D_MODEL)

    o3 = 2 * CONV_CH + MLA_Q_RANK + MLA_KV_RANK
    w_in = jnp.concatenate([ev_w_in[0][:, :o3], jnp.zeros((D_MODEL, MLA_NOPE), F32), ev_w_in[0][:, o3:],
                            jnp.zeros((D_MODEL, LANES - MLA_QK), F32)], axis=1).astype(BF16)
    u, qn, ckv, kr = _inproj0_call(*x_in, mod, norm1_g[0][None], w_in, ev_q_norm_g[0][None], ev_kv_norm_g[0][None])
    conv = _conv_call(u, ev_conv_w[0], ev_conv_b[0][None], ev_conv_ln_g[0][None], ev_conv_ln_b[0][None])

    pad_qk = ((0, 0), (0, 0), (0, LANES - MLA_QK))
    wq = jnp.pad(ev_w_qb[0].reshape(MLA_Q_RANK, MLA_HEADS, MLA_QK), pad_qk).reshape(MLA_Q_RANK, -1).astype(BF16)
    wkv = ev_w_kvb[0].reshape(MLA_KV_RANK, MLA_HEADS, MLA_NOPE + MLA_V)
    wk = jnp.pad(wkv[..., :MLA_NOPE], ((0, 0), (0, 0), (0, LANES - MLA_NOPE))).reshape(MLA_KV_RANK, -1).astype(BF16)
    wv = wkv[..., MLA_NOPE:].reshape(MLA_KV_RANK, -1).astype(BF16)
    qg = jnp.pad(ev_qn_g[0] * (MLA_QK ** -0.5 * LOG2E), (0, LANES - MLA_QK))[None]
    kg = jnp.pad(ev_kn_g[0], (0, LANES - MLA_QK))[None]
    tables = _rope_tables(MLA_ROPE, (MLA_NOPE,), DEC_SEQ)
    q0, k0, v0 = _mla_prep_call(qn, ckv, kr, wq, wk, wv, qg, kg, tables, _rope_block)
    kr_cache = jnp.pad(cache_mla_krope[:, 0].reshape(N_CACHE, MLA_ROPE), ((0, 0), (MLA_NOPE, LANES - MLA_QK)))
    _, kc0, vc0 = _mla_prep_call(jnp.zeros((N_CACHE, MLA_Q_RANK), BF16), cache_mla_ckv[:, 0].reshape(N_CACHE, MLA_KV_RANK),
                                 kr_cache, wq, wk, wv, qg, kg, tables, lambda i: BLK_PER_LAT)
    attn = (_mla_attn_ctx_call(qg, kg, q0, k0, v0), _mla_attn_lat_call(qg, kg, q0, k0, v0, kc0, vc0))

    w_o = ev_w_o[0].astype(BF16)
    r_hi, r_lo, r_b = _router_weights(moe_we[0], moe_wg[0], moe_be[0], moe_bg[0])
    x, h2, *routing = _outproj_call(0, [(conv,), attn], [w_o[:CONV_CH], w_o[CONV_CH:]], x_in, mod, norm2_g[0][None], r_hi, r_lo, r_b)
    (x,) = _moe_call(0, h2, routing, moe_w_gate, moe_w_up, moe_w_down, x, mod)

    lambda_init = 0.8 - 0.6 * math.exp(-0.3 * 1)
    qg1 = jnp.tile(od_qn_g[0] * (DIFF_DK ** -0.5 * LOG2E), 2)[None]
    kg1 = jnp.tile(od_kn_g[0], 2)[None]
    tables1 = _rope_tables(DIFF_DK, (0, DIFF_DK), DEC_SEQ)
    q1, k1, v1, kn1, vf1 = _inproj1_call(x, mod, norm1_g[1][None], od_w_qkv[0].astype(BF16), qg1, kg1, tables1)
    lams = [od_lambda_q1[0][None], od_lambda_k1[0][None], od_lambda_q2[0][None], od_lambda_k2[0][None]]
    sg = od_subln_g[0][None]
    hw = DIFF_HEADS * DIFF_DV
    kc1 = cache_diff_k[:, 0].reshape(N_CACHE, hw).astype(BF16)
    vc1 = cache_diff_v[:, 0].reshape(N_CACHE, hw).astype(BF16)
    attn1 = (_diff_attn_ctx_call(lambda_init, (qg1, kg1), q1, k1, v1, lams, sg),
             _diff_attn_lat_call(lambda_init, (qg1, kg1), q1, k1, v1, kc1, vc1, lams, sg))
    r_hi, r_lo, r_b = _router_weights(moe_we[1], moe_wg[1], moe_be[1], moe_bg[1])
    x, h2, *routing = _outproj_call(1, [attn1], [od_w_o[0].astype(BF16)], (x,), mod, norm2_g[1][None], r_hi, r_lo, r_b)
    y_ctx, y_lat = _moe_call(1, h2, routing, moe_w_gate, moe_w_up, moe_w_down, x, mod, split_out=True)

    y_prompt = y_ctx.reshape(BATCH, SEQ, D_MODEL)
    y_sample = y_lat.reshape(DEC_BATCH, DEC_SEQ, D_MODEL)
    new_mla_ckv = ckv[:N_CTX].reshape(BATCH, 1, SEQ, MLA_KV_RANK)
    new_mla_krope = kr[:N_CTX, MLA_NOPE:MLA_QK].reshape(BATCH, 1, SEQ, MLA_ROPE)
    new_diff_k = kn1.reshape(BATCH, 1, SEQ, DIFF_HEADS, 2, DIFF_DK)
    new_diff_v = vf1.reshape(BATCH, 1, SEQ, DIFF_HEADS, DIFF_DV)
    return (y_prompt, y_sample, new_mla_ckv, new_mla_krope, new_diff_k, new_diff_v)
```

```python
import functools
import math

import jax
import jax.numpy as jnp
from jax import lax
from jax.experimental import pallas as pl
from jax.experimental.pallas import tpu as pltpu

F32 = jnp.float32
BF16 = jnp.bfloat16

D_MODEL = 1024
BATCH = 16
SEQ = 256
DEC_BATCH = 8
DEC_SEQ = 2048
PAST_LEN = 256
GRID_W = 64
ROPE_BASE = 10000.0
EPS = 1e-6

CONV_CH = 512
CONV_WIDTH = 31
CONV_PAD = 15
MLA_HEADS = 8
MLA_NOPE = 64
MLA_ROPE = 32
MLA_QK = 96
MLA_V = 64
MLA_Q_RANK = 384
MLA_KV_RANK = 256
DIFF_HEADS = 8
DIFF_DK = 64
DIFF_DV = 128
N_GROUPS = 4
EXPERTS_PER_GROUP = 8
N_EXPERTS = 32
EXPERT_FF = 256

LANES = 128
SUBLANES = 8
N_CTX = BATCH * SEQ
N_LAT = DEC_BATCH * DEC_SEQ
N_TOK = N_CTX + N_LAT
N_CACHE = DEC_BATCH * PAST_LEN
MOD_ROWS = 16
VMEM_LIMIT = 48 * 1024 * 1024

TM = 512
NB_CTX = N_CTX // TM
NB_TOK = N_TOK // TM
BLK_PER_LAT = DEC_SEQ // TM
TQ = 256
CONV_T = 256
CONV_HALO = 16
CONV_RB = 32
CONV_SHIFT_ROWS = CONV_T + 2 * CONV_HALO - SUBLANES


def _params(*sem):
    return pltpu.CompilerParams(dimension_semantics=sem, vmem_limit_bytes=VMEM_LIMIT)


def _sigmoid(x):
    return 1.0 / (1.0 + jnp.exp(-x))


def _rms(x, g):
    return x * lax.rsqrt(jnp.mean(x * x, axis=-1, keepdims=True) + EPS) * g


def _dot(a, b):
    return jnp.dot(a, b, preferred_element_type=F32)


def _dot_nt(a, b):
    return lax.dot_general(a, b, (((1,), (1,)), ((), ())), preferred_element_type=F32)


def _seq_of_block(i):
    return jnp.where(i < NB_CTX, 0, 1 + (i - NB_CTX) // BLK_PER_LAT)


def _mod_spec(layer, slot):
    return pl.BlockSpec((1, 1, D_MODEL), lambda i, *_: ((layer * MOD_ROWS + _seq_of_block(i)) * 6 + slot, 0, 0))


def _row_spec(width, tm=TM):
    return pl.BlockSpec((tm, width), lambda i, *_: (i, 0))


def _ctx_spec(width):
    return pl.BlockSpec((TM, width), lambda i, *_: (jnp.minimum(i, NB_CTX - 1), 0))


def _lat_spec(width):
    return pl.BlockSpec((TM, width), lambda i, *_: (jnp.maximum(i - NB_CTX, 0), 0))


def _split_specs(width):
    return [_ctx_spec(width), _lat_spec(width)]


def _pick(ctx_ref, lat_ref):
    return jnp.where(pl.program_id(0) < NB_CTX, ctx_ref[...], lat_ref[...])


def _ones_blocks(group_log2):
    r = lax.broadcasted_iota(jnp.int32, (LANES, LANES), 0) >> group_log2
    c = lax.broadcasted_iota(jnp.int32, (LANES, LANES), 1) >> group_log2
    return jnp.where(r == c, 1.0, 0.0).astype(BF16)


def _full_spec(shape):
    nd = len(shape)
    return pl.BlockSpec(shape, lambda *_: (0,) * nd)


def _mod_kernel(c_ref, w_ref, b_ref, o_ref):
    c = c_ref[...]
    s = (c * _sigmoid(c)).astype(BF16)
    o_ref[0] = _dot(s, w_ref[0].astype(BF16)) + b_ref[0]


def _mod_call(cond, ada_w, ada_b):
    depth = ada_w.shape[0]
    tn = 1024
    return pl.pallas_call(
        _mod_kernel,
        grid=(depth, 6 * D_MODEL // tn),
        in_specs=[
            pl.BlockSpec((MOD_ROWS, D_MODEL), lambda l, j: (0, 0)),
            pl.BlockSpec((1, D_MODEL, tn), lambda l, j: (l, 0, j)),
            pl.BlockSpec((1, 1, tn), lambda l, j: (l, 0, j)),
        ],
        out_specs=pl.BlockSpec((1, MOD_ROWS, tn), lambda l, j: (l, 0, j)),
        out_shape=jax.ShapeDtypeStruct((depth, MOD_ROWS, 6 * D_MODEL), F32),
        compiler_params=_params("parallel", "parallel"),
        name="adaln_mod",
    )(cond, ada_w, ada_b.reshape(depth, 1, 6 * D_MODEL))


EV_W = 1792


def _inproj0_kernel(xc_ref, xl_ref, g_ref, sh_ref, sc_ref, w_ref, qg_ref, kvg_ref, u_ref, qn_ref, ckv_ref, kr_ref):
    h = _rms(_pick(xc_ref, xl_ref), g_ref[...]) * (1.0 + sc_ref[0]) + sh_ref[0]
    p = _dot(h.astype(BF16), w_ref[...])
    u_ref[...] = p[:, :CONV_CH] * _sigmoid(p[:, CONV_CH:2 * CONV_CH])
    o1 = 2 * CONV_CH
    o2 = o1 + MLA_Q_RANK
    o3 = o2 + MLA_KV_RANK
    qn_ref[...] = _rms(p[:, o1:o2], qg_ref[...]).astype(BF16)
    ckv_ref[...] = _rms(p[:, o2:o3], kvg_ref[...])
    kr_ref[...] = p[:, o3:o3 + LANES]


def _inproj0_call(x_ctx, x_lat, mod, g1, w_in, qg, kvg):
    return pl.pallas_call(
        _inproj0_kernel,
        grid=(NB_TOK,),
        in_specs=_split_specs(D_MODEL) + [
            _full_spec((1, D_MODEL)), _mod_spec(0, 0), _mod_spec(0, 1),
            _full_spec((D_MODEL, EV_W)), _full_spec((1, MLA_Q_RANK)), _full_spec((1, MLA_KV_RANK)),
        ],
        out_specs=[_row_spec(CONV_CH), _row_spec(MLA_Q_RANK), _row_spec(MLA_KV_RANK), _row_spec(LANES)],
        out_shape=[
            jax.ShapeDtypeStruct((N_TOK, CONV_CH), F32),
            jax.ShapeDtypeStruct((N_TOK, MLA_Q_RANK), BF16),
            jax.ShapeDtypeStruct((N_TOK, MLA_KV_RANK), F32),
            jax.ShapeDtypeStruct((N_TOK, LANES), F32),
        ],
        compiler_params=_params("parallel"),
        name="inproj0",
    )(x_ctx, x_lat, g1, mod, mod, w_in, qg, kvg)


CONV_CHUNKS_PER_LAT = DEC_SEQ // CONV_T
CONV_NB_CTX = N_CTX // CONV_T
CONV_NB = N_TOK // CONV_T


def _conv_kernel(prev_ref, cur_ref, nxt_ref, w_ref, b_ref, lg_ref, lb_ref, o_ref, buf, shifted):
    i = pl.program_id(0)
    j = (i - CONV_NB_CTX) % CONV_CHUNKS_PER_LAT
    first = jnp.logical_or(i < CONV_NB_CTX, j == 0)
    last = jnp.logical_or(i < CONV_NB_CTX, j == CONV_CHUNKS_PER_LAT - 1)
    buf[0:CONV_HALO, :] = jnp.where(first, 0.0, prev_ref[...])
    buf[CONV_HALO:CONV_HALO + CONV_T, :] = cur_ref[...]
    buf[CONV_HALO + CONV_T:, :] = jnp.where(last, 0.0, nxt_ref[...])
    for s in range(1, SUBLANES):
        shifted[s - 1] = buf[s:s + CONV_SHIFT_ROWS, :]
    for rb in range(CONV_T // CONV_RB):
        acc = jnp.zeros((CONV_RB, CONV_CH), F32)
        for k in range(CONV_WIDTH):
            row = rb * CONV_RB + CONV_HALO - CONV_PAD + k
            s, r0 = row % SUBLANES, row - row % SUBLANES
            src = buf[r0:r0 + CONV_RB, :] if s == 0 else shifted[s - 1, r0:r0 + CONV_RB, :]
            acc = acc + w_ref[k:k + 1, :] * src
        y = acc + b_ref[...]
        mu = jnp.mean(y, axis=-1, keepdims=True)
        yc = y - mu
        var = jnp.mean(yc * yc, axis=-1, keepdims=True)
        z = yc * lax.rsqrt(var + EPS) * lg_ref[...] + lb_ref[...]
        o_ref[rb * CONV_RB:(rb + 1) * CONV_RB, :] = (z * _sigmoid(z)).astype(BF16)


def _conv_call(u, w, b, lg, lb):
    hb = CONV_T // CONV_HALO
    n_halo_blocks = N_TOK // CONV_HALO
    return pl.pallas_call(
        _conv_kernel,
        grid=(CONV_NB,),
        in_specs=[
            pl.BlockSpec((CONV_HALO, CONV_CH), lambda i: (jnp.maximum(i * hb - 1, 0), 0)),
            pl.BlockSpec((CONV_T, CONV_CH), lambda i: (i, 0)),
            pl.BlockSpec((CONV_HALO, CONV_CH), lambda i: (jnp.minimum((i + 1) * hb, n_halo_blocks - 1), 0)),
            _full_spec((CONV_WIDTH, CONV_CH)), _full_spec((1, CONV_CH)),
            _full_spec((1, CONV_CH)), _full_spec((1, CONV_CH)),
        ],
        out_specs=pl.BlockSpec((CONV_T, CONV_CH), lambda i: (i, 0)),
        out_shape=jax.ShapeDtypeStruct((N_TOK, CONV_CH), BF16),
        scratch_shapes=[pltpu.VMEM((CONV_T + 2 * CONV_HALO, CONV_CH), F32),
                        pltpu.VMEM((SUBLANES - 1, CONV_SHIFT_ROWS, CONV_CH), F32)],
        compiler_params=_params("parallel"),
        name="conformer_conv",
    )(u, u, u, w, b, lg, lb)


def _rope_tile(x, c_ref, s1_ref, s2_ref, half):
    up = pltpu.roll(x, LANES - half, 1)
    dn = pltpu.roll(x, half, 1)
    return x * c_ref[...] + up * s1_ref[...] + dn * s2_ref[...]


def _rope_tables(rot_dim, lane_starts, n_tok):
    rows = n_tok // GRID_W
    row_ids = jnp.repeat(jnp.arange(rows, dtype=F32), GRID_W)
    col_ids = jnp.tile(jnp.arange(GRID_W, dtype=F32), rows)
    quarter = rot_dim // 4
    half = rot_dim // 2
    freqs = jnp.power(ROPE_BASE, -jnp.arange(quarter, dtype=F32) / quarter)
    ang = jnp.concatenate([row_ids[:, None] * freqs, col_ids[:, None] * freqs], axis=-1)
    cos, sin = jnp.cos(ang), jnp.sin(ang)
    one = lambda n: jnp.ones((n_tok, n), F32)
    zero = lambda n: jnp.zeros((n_tok, n), F32)
    c, s1, s2, at = [], [], [], 0
    for st in lane_starts:
        c += [one(st - at), cos, cos]
        s1 += [zero(st - at), -sin, zero(half)]
        s2 += [zero(st - at), zero(half), sin]
        at = st + rot_dim
    c, s1, s2 = (jnp.concatenate(t + [fill(LANES - at)], axis=1) for t, fill in ((c, one), (s1, zero), (s2, zero)))
    ident = (jnp.ones((TM, LANES), F32), jnp.zeros((TM, LANES), F32), jnp.zeros((TM, LANES), F32))
    return tuple(jnp.concatenate([t, e], axis=0) for t, e in zip((c, s1, s2), ident))


def _rope_block(i):
    return jnp.where(i < NB_CTX, BLK_PER_LAT, (i - NB_CTX) % BLK_PER_LAT)


def _mla_prep_kernel(qn_ref, ckv_ref, kr_ref, wq_ref, wk_ref, wv_ref, qg_ref, kg_ref, c_ref, s1_ref, s2_ref,
                     q_ref, k_ref, v_ref):
    qn = qn_ref[...]
    ckv = ckv_ref[...].astype(BF16)
    kr = kr_ref[...]
    half = MLA_ROPE // 2
    ones = _ones_blocks(LANES.bit_length() - 1)
    n_pairs = MLA_HEADS // 2

    def project(hp):
        pair = slice(hp * 2 * LANES, (hp + 1) * 2 * LANES)
        return _dot(qn, wq_ref[:, pair]), _dot(ckv, wk_ref[:, pair])

    nxt = project(0)
    for hp in range(n_pairs):
        qf, kf = nxt
        nxt = project(hp + 1) if hp + 1 < n_pairs else None
        for u in range(2):
            sl = slice((2 * hp + u) * LANES, (2 * hp + u + 1) * LANES)
            qh = _group_rms(qf[:, u * LANES:(u + 1) * LANES], qg_ref, ones, MLA_QK)
            q_ref[:, sl] = _rope_tile(qh, c_ref, s1_ref, s2_ref, half).astype(BF16)
            kh = _group_rms(kf[:, u * LANES:(u + 1) * LANES] + kr, kg_ref, ones, MLA_QK)
            k_ref[:, sl] = _rope_tile(kh, c_ref, s1_ref, s2_ref, half).astype(BF16)
    v_ref[...] = _dot(ckv, wv_ref[...]).astype(BF16)


def _mla_prep_call(qn, ckv, kr, wq, wk, wv, qg, kg, tables, rope_block_fn):
    n = qn.shape[0]
    hw = MLA_HEADS * LANES
    tspec = pl.BlockSpec((TM, LANES), lambda i: (rope_block_fn(i), 0))
    return pl.pallas_call(
        _mla_prep_kernel,
        grid=(n // TM,),
        in_specs=[
            _row_spec(MLA_Q_RANK), _row_spec(MLA_KV_RANK), _row_spec(LANES),
            _full_spec((MLA_Q_RANK, hw)), _full_spec((MLA_KV_RANK, hw)), _full_spec((MLA_KV_RANK, MLA_HEADS * MLA_V)),
            _full_spec((1, LANES)), _full_spec((1, LANES)), tspec, tspec, tspec,
        ],
        out_specs=[_row_spec(hw), _row_spec(hw), _row_spec(MLA_HEADS * MLA_V)],
        out_shape=[
            jax.ShapeDtypeStruct((n, hw), BF16),
            jax.ShapeDtypeStruct((n, hw), BF16),
            jax.ShapeDtypeStruct((n, MLA_HEADS * MLA_V), BF16),
        ],
        compiler_params=_params("parallel"),
        name="mla_prep",
    )(qn, ckv, kr, wq, wk, wv, qg, kg, *tables)


LOG2E = 1.4426950408889634
EXP2_SAFE = 100.0
BF16_NORM_MARGIN = 1.02


def _max_sq_norm(x, mask=None):
    xf = x.astype(F32)
    xx = xf * xf
    if mask is not None:
        xx = jnp.where(mask, xx, 0.0)
    return jnp.max(jnp.sum(xx, axis=-1, keepdims=True), axis=0, keepdims=True)


def _gain_sq_bound(g_ref, dims):
    g = g_ref[...]
    return jnp.max(g * g, axis=-1, keepdims=True) * (dims * BF16_NORM_MARGIN)


def _exps(s_list, shifted):
    if shifted:
        m = s_list[0].max(axis=-1, keepdims=True)
        for s in s_list[1:]:
            m = jnp.maximum(m, s.max(axis=-1, keepdims=True))
        es = [jnp.exp2(s - m) for s in s_list]
    else:
        es = [jnp.exp2(s) for s in s_list]
    l = es[0].sum(axis=-1, keepdims=True)
    for e in es[1:]:
        l = l + e.sum(axis=-1, keepdims=True)
    return es, l


def _run_query_blocks(bound_sq, n_blocks, tq, units_per_block, logits, finish):
    per_trip = 4 if n_blocks % 4 == 0 else 1

    def trip(shifted, t, carry):
        units = [(pl.multiple_of((t * per_trip + b) * tq, tq), u)
                 for b in range(per_trip) for u in range(units_per_block)]
        pending = logits(*units[0])
        for k, unit in enumerate(units):
            nxt = logits(*units[k + 1]) if k + 1 < len(units) else None
            finish(shifted, *unit, pending)
            pending = nxt
        return carry

    safe = bound_sq[0, 0] <= EXP2_SAFE * EXP2_SAFE

    @pl.when(safe)
    def _():
        lax.fori_loop(0, n_blocks // per_trip, functools.partial(trip, False), 0)

    @pl.when(jnp.logical_not(safe))
    def _():
        lax.fori_loop(0, n_blocks // per_trip, functools.partial(trip, True), 0)


def _mla_attn_kernel(has_cache, tq, n_pairs, qg_ref, kg_ref, *refs):
    if has_cache:
        q_ref, k_ref, v_ref, kc_ref, vc_ref, o_ref = refs
        k_refs, v_refs = [k_ref, kc_ref], [v_ref, vc_ref]
    else:
        q_ref, k_ref, v_ref, o_ref = refs
        k_refs, v_refs = [k_ref], [v_ref]
    bound_sq = _gain_sq_bound(qg_ref, MLA_QK) * _gain_sq_bound(kg_ref, MLA_QK)

    def logits(r0, head):
        sl = slice(head * LANES, (head + 1) * LANES)
        q = q_ref[pl.ds(r0, tq), sl]
        return [_dot_nt(q, kr[:, sl]) for kr in k_refs]

    held = []

    def finish(shifted, r0, head, s_list):
        vsl = slice(head // 2 * LANES, (head // 2 + 1) * LANES)
        es, l = _exps(s_list, shifted)
        o = _dot(es[0].astype(BF16), v_refs[0][:, vsl])
        for e, vr in zip(es[1:], v_refs[1:]):
            o = o + _dot(e.astype(BF16), vr[:, vsl])
        o = o / l
        if head % 2 == 0:
            held.append(o)
        else:
            lane = lax.broadcasted_iota(jnp.int32, o.shape, 1)
            o_ref[pl.ds(r0, tq), vsl] = jnp.where(lane < MLA_V, held.pop(), o).astype(BF16)

    _run_query_blocks(bound_sq, q_ref.shape[0] // tq, tq, 2 * n_pairs, logits, finish)


def _mla_attn_ctx_call(qg, kg, q, k, v):
    return pl.pallas_call(
        functools.partial(_mla_attn_kernel, False, SEQ, MLA_HEADS // 2),
        grid=(BATCH,),
        in_specs=[
            _full_spec((1, LANES)), _full_spec((1, LANES)),
            pl.BlockSpec((SEQ, MLA_HEADS * LANES), lambda b: (b, 0)),
            pl.BlockSpec((SEQ, MLA_HEADS * LANES), lambda b: (b, 0)),
            pl.BlockSpec((SEQ, MLA_HEADS * MLA_V), lambda b: (b, 0)),
        ],
        out_specs=pl.BlockSpec((SEQ, MLA_HEADS * MLA_V), lambda b: (b, 0)),
        out_shape=jax.ShapeDtypeStruct((N_CTX, MLA_HEADS * MLA_V), BF16),
        compiler_params=_params("parallel"),
        name="mla_attn_ctx",
    )(qg, kg, q, k, v)


def _mla_attn_lat_call(qg, kg, q, k, v, kc, vc):
    k0 = N_CTX // DEC_SEQ
    return pl.pallas_call(
        functools.partial(_mla_attn_kernel, True, TQ, 1),
        grid=(DEC_BATCH, MLA_HEADS // 2),
        in_specs=[
            _full_spec((1, LANES)), _full_spec((1, LANES)),
            pl.BlockSpec((DEC_SEQ, 2 * LANES), lambda b, hp: (k0 + b, hp)),
            pl.BlockSpec((DEC_SEQ, 2 * LANES), lambda b, hp: (k0 + b, hp)),
            pl.BlockSpec((DEC_SEQ, LANES), lambda b, hp: (k0 + b, hp)),
            pl.BlockSpec((PAST_LEN, 2 * LANES), lambda b, hp: (b, hp)),
            pl.BlockSpec((PAST_LEN, LANES), lambda b, hp: (b, hp)),
        ],
        out_specs=pl.BlockSpec((DEC_SEQ, LANES), lambda b, hp: (b, hp)),
        out_shape=jax.ShapeDtypeStruct((N_LAT, MLA_HEADS * MLA_V), BF16),
        compiler_params=_params("parallel", "parallel"),
        name="mla_attn_lat",
    )(qg, kg, q, k, v, kc, vc)


ROUTE_ROWS = 40


def _route(logits_t):
    row = lax.broadcasted_iota(jnp.int32, logits_t.shape, 0)
    row_f = row.astype(F32)
    big = float(LANES)
    neg = -jnp.inf
    gmask = jnp.logical_and(row >= N_EXPERTS, row < N_EXPERTS + N_GROUPS)
    gl = jnp.where(gmask, logits_t, neg)
    ge = jnp.exp(gl - jnp.max(gl, axis=0, keepdims=True))
    gp = ge / jnp.sum(ge, axis=0, keepdims=True)
    gate = jnp.max(gp, axis=0, keepdims=True)
    gidx = jnp.min(jnp.where(jnp.logical_and(gmask, gp == gate), row_f, big), axis=0, keepdims=True) - N_EXPERTS
    lo = gidx * EXPERTS_PER_GROUP
    emask = jnp.logical_and(row_f >= lo, row_f < lo + EXPERTS_PER_GROUP)
    el = jnp.where(emask, logits_t, neg)
    ee = jnp.exp(el - jnp.max(el, axis=0, keepdims=True))
    ep = jnp.where(emask, ee / jnp.sum(ee, axis=0, keepdims=True), -1.0)
    p1 = jnp.max(ep, axis=0, keepdims=True)
    i1 = jnp.min(jnp.where(ep == p1, row_f, big), axis=0, keepdims=True)
    ep2 = jnp.where(row_f == i1, -1.0, ep)
    p2 = jnp.max(ep2, axis=0, keepdims=True)
    i2 = jnp.min(jnp.where(ep2 == p2, row_f, big), axis=0, keepdims=True)
    den = p1 + p2
    return jnp.where(row_f == i1, gate * p1 / den, 0.0) + jnp.where(row_f == i2, gate * p2 / den, 0.0)


def _window_positions(cw_t):
    n = cw_t.shape[1]
    cw = jnp.concatenate([cw_t, jnp.zeros((LANES - ROUTE_ROWS, n), F32)], axis=0)
    a = cw > 0.0
    a_bf = jnp.where(a, 1.0, 0.0).astype(BF16)
    earlier = jnp.where(lax.broadcasted_iota(jnp.int32, (n, n), 0) < lax.broadcasted_iota(jnp.int32, (n, n), 1), 1.0, 0.0)
    rank = _dot(a_bf, earlier.astype(BF16))
    cnt_col = jnp.sum(jnp.where(a, 1.0, 0.0), axis=1, keepdims=True)
    padc = jnp.floor((cnt_col + (MOE_CHUNK - 1)) * (1.0 / MOE_CHUNK)) * MOE_CHUNK
    lower = jnp.where(lax.broadcasted_iota(jnp.int32, (LANES, LANES), 1) < lax.broadcasted_iota(jnp.int32, (LANES, LANES), 0),
                      1.0, 0.0)
    loff = _dot(lower.astype(BF16), jnp.broadcast_to(padc, (LANES, LANES)).astype(BF16))
    lpos = rank + jnp.concatenate([loff] * (n // LANES), axis=1)
    pa = jnp.min(jnp.where(a, lpos, float(R_LOC)), axis=0, keepdims=True)
    pb = jnp.max(jnp.where(a, lpos, -1.0), axis=0, keepdims=True)
    wa = jnp.sum(jnp.where(jnp.logical_and(a, lpos == pa), cw, 0.0), axis=0, keepdims=True)
    wb = jnp.sum(jnp.where(jnp.logical_and(a, lpos == pb), cw, 0.0), axis=0, keepdims=True)
    pos_t = jnp.concatenate([pa, pb, wa, wb, jnp.zeros((SUBLANES - 4, n), F32)], axis=0)
    cnt = _dot_nt(jnp.ones((SUBLANES, n), BF16), a_bf)[0:1]
    return pos_t, cnt


def _outproj_kernel(arity, *refs):
    refs = list(refs)
    rows = []
    for n in arity:
        rows.append(refs[0][...] if n == 1 else _pick(refs[0], refs[1]))
        refs = refs[n:]
    n_act = len(arity) - 1
    w_refs, refs = refs[:n_act], refs[n_act:]
    g1_ref, n2_ref, sh2_ref, sc2_ref, rh_ref, rl_ref, rb_ref, xo_ref, h_ref, pos_ref, pt_ref, cnt_ref = refs
    out = _dot(jnp.concatenate(rows[:n_act], axis=1), jnp.concatenate([w_ref[...] for w_ref in w_refs], axis=0))
    x = rows[n_act] + g1_ref[0] * out
    xo_ref[...] = x
    h = _rms(x, n2_ref[...]) * (1.0 + sc2_ref[0]) + sh2_ref[0]
    h_hi = h.astype(BF16)
    h_ref[...] = h_hi
    h_lo = (h - h_hi.astype(F32)).astype(BF16)
    logits = _dot(h_hi, rh_ref[...]) + (_dot(h_hi, rl_ref[...]) + _dot(h_lo, rh_ref[...])) + rb_ref[...]
    pos_t, cnt = _window_positions(_route(logits.T[:ROUTE_ROWS]))
    pt_ref[0] = pos_t
    pos_ref[...] = jnp.concatenate([pos_t, jnp.zeros((LANES - SUBLANES, pos_t.shape[1]), F32)], axis=0).T
    cnt_ref[0] = cnt


def _outproj_call(layer, acts, ws, x, mod, n2, r_hi, r_lo, r_b):
    operands = list(acts) + [x]
    row_specs = []
    for op in operands:
        row_specs += [_row_spec(op[0].shape[1])] if len(op) == 1 else _split_specs(op[0].shape[1])
    flat = [a for op in operands for a in op]
    return pl.pallas_call(
        functools.partial(_outproj_kernel, tuple(len(op) for op in operands)),
        grid=(NB_TOK,),
        in_specs=(
            row_specs + [_full_spec(w.shape) for w in ws]
            + [_mod_spec(layer, 2), _full_spec((1, D_MODEL)), _mod_spec(layer, 3),
               _mod_spec(layer, 4), _full_spec((D_MODEL, LANES)), _full_spec((D_MODEL, LANES)), _full_spec((1, LANES))]
        ),
        out_specs=[_row_spec(D_MODEL), _row_spec(D_MODEL), _row_spec(LANES),
                   pl.BlockSpec((1, SUBLANES, TM), lambda i: (i, 0, 0)), pl.BlockSpec((1, 1, LANES), lambda i: (i, 0, 0))],
        out_shape=[
            jax.ShapeDtypeStruct((N_TOK, D_MODEL), F32),
            jax.ShapeDtypeStruct((N_TOK, D_MODEL), BF16),
            jax.ShapeDtypeStruct((N_TOK, LANES), F32),
            jax.ShapeDtypeStruct((NB_TOK, SUBLANES, TM), F32),
            jax.ShapeDtypeStruct((NB_TOK, 1, LANES), F32),
        ],
        compiler_params=_params("parallel"),
        name=f"outproj{layer}",
    )(*flat, *ws, mod, n2, mod, mod, r_hi, r_lo, r_b)


MOE_CHUNK = 16
MOE_TILE = 512
NW = N_TOK // TM
R_LOC = 2 * TM + N_EXPERTS * MOE_CHUNK
R_SORTED_MAX = 2 * N_TOK + NW * N_EXPERTS * (MOE_CHUNK - 1) + N_EXPERTS * (MOE_TILE - 1)
NT_MAX = -(-R_SORTED_MAX // MOE_TILE)
P_MAX = R_LOC // MOE_CHUNK
PIECE_GROUP = 8
TRASH_ROW = NT_MAX * MOE_TILE
SORTED_ROWS = TRASH_ROW + PIECE_GROUP * MOE_CHUNK


def _moe_tables(cnt):
    cnt = cnt[:, 0, :N_EXPERTS].astype(jnp.int32)
    padc = (cnt + (MOE_CHUNK - 1)) // MOE_CHUNK * MOE_CHUNK
    loff = jnp.cumsum(padc, axis=1) - padc
    tot = jnp.sum(padc, axis=0)
    seg = (tot + (MOE_TILE - 1)) // MOE_TILE * MOE_TILE
    seg_end = jnp.cumsum(seg)
    base = seg_end - seg
    goff = base[None, :] + jnp.cumsum(padc, axis=0) - padc
    tile_start = jnp.arange(NT_MAX, dtype=jnp.int32) * MOE_TILE
    texp = jnp.minimum(jnp.sum((tile_start[:, None] >= seg_end[None, :]).astype(jnp.int32), axis=1), N_EXPERTS - 1)
    nvalid = (seg_end[-1:] // MOE_TILE).astype(jnp.int32)
    p_row = jnp.arange(P_MAX, dtype=jnp.int32)[None, :, None] * MOE_CHUNK
    inside = jnp.logical_and(p_row >= loff[:, None, :], p_row < (loff + padc)[:, None, :])
    grow = jnp.sum(jnp.where(inside, goff[:, None, :] + p_row - loff[:, None, :], 0), axis=-1)
    used = jnp.any(inside, axis=-1)
    filler = TRASH_ROW + (jnp.arange(P_MAX, dtype=jnp.int32) % PIECE_GROUP)[None, :] * MOE_CHUNK
    npieces = jnp.sum(padc, axis=1) // MOE_CHUNK
    i32 = lambda a: a.reshape(-1).astype(jnp.int32)
    return dict(dst=i32(jnp.where(used, grow, filler)), src=i32(jnp.where(used, grow, 0)),
                ngroups=i32((npieces + (PIECE_GROUP - 1)) // PIECE_GROUP), tail_start=i32(base + tot),
                tail_chunks=i32((seg - tot) // MOE_CHUNK), texp=texp, nvalid=nvalid)


def _for_each_piece(w, ngroups_ref, grow_ref, fn):
    n = ngroups_ref[w]

    def group(j, carry):
        for u in range(PIECE_GROUP):
            p = j * PIECE_GROUP + u
            fn(pl.multiple_of(p * MOE_CHUNK, MOE_CHUNK), pl.multiple_of(grow_ref[w * P_MAX + p], MOE_CHUNK))
        return carry

    lax.fori_loop(0, n, group, 0)
    return n


def _dispatch_kernel(ngroups_ref, dst_ref, tstart_ref, tchunk_ref, h_ref, pt_ref, hs_ref, buf, zbuf, sems):
    w = pl.program_id(0)
    slot = w % 2
    pos_t = pt_ref[0]
    r = lax.broadcasted_iota(jnp.int32, (R_LOC, TM), 0).astype(F32)
    sel = jnp.logical_or(r == pos_t[0:1, :], r == pos_t[1:2, :])
    buf[slot] = _dot(jnp.where(sel, 1.0, 0.0).astype(BF16), h_ref[...]).astype(BF16)

    def wait_groups(window, s):
        def wait_group(c, carry):
            pltpu.make_async_copy(buf.at[s, pl.ds(0, PIECE_GROUP * MOE_CHUNK)],
                                  hs_ref.at[pl.ds(0, PIECE_GROUP * MOE_CHUNK)], sems.at[s]).wait()
            return carry

        lax.fori_loop(0, ngroups_ref[window], wait_group, 0)

    @pl.when(w > 0)
    def _():
        wait_groups(w - 1, 1 - slot)

    def piece_copy(local_row, global_row):
        return pltpu.make_async_copy(buf.at[slot, pl.ds(local_row, MOE_CHUNK)],
                                     hs_ref.at[pl.ds(global_row, MOE_CHUNK)], sems.at[slot])

    _for_each_piece(w, ngroups_ref, dst_ref, lambda lo, go: piece_copy(lo, go).start())

    @pl.when(w == NW - 1)
    def _():
        wait_groups(w, slot)
        sem = sems.at[slot]
        zbuf[...] = jnp.zeros_like(zbuf)

        def zero_copy(global_row):
            return pltpu.make_async_copy(zbuf, hs_ref.at[pl.ds(global_row, MOE_CHUNK)], sem)

        def per_expert(e, total):
            n = tchunk_ref[e]
            st = tstart_ref[e]

            def piece(c, carry):
                zero_copy(pl.multiple_of(st + c * MOE_CHUNK, MOE_CHUNK)).start()
                return carry

            lax.fori_loop(0, n, piece, 0)
            return total + n

        total_z = lax.fori_loop(0, N_EXPERTS, per_expert, 0)

        def wait_zero(c, carry):
            zero_copy(0).wait()
            return carry

        lax.fori_loop(0, total_z, wait_zero, 0)


def _dispatch_call(layer, h, pos_t, tb):
    return pl.pallas_call(
        _dispatch_kernel,
        grid_spec=pltpu.PrefetchScalarGridSpec(
            num_scalar_prefetch=4,
            grid=(NW,),
            in_specs=[_row_spec(D_MODEL), pl.BlockSpec((1, SUBLANES, TM), lambda i, *_: (i, 0, 0))],
            out_specs=pl.BlockSpec(memory_space=pl.ANY),
            scratch_shapes=[pltpu.VMEM((2, R_LOC, D_MODEL), BF16), pltpu.VMEM((MOE_CHUNK, D_MODEL), BF16),
                            pltpu.SemaphoreType.DMA((2,))],
        ),
        out_shape=jax.ShapeDtypeStruct((SORTED_ROWS, D_MODEL), BF16),
        compiler_params=_params("arbitrary"),
        name=f"moe_dispatch{layer}",
    )(tb["ngroups"], tb["dst"], tb["tail_start"], tb["tail_chunks"], h, pos_t)


def _expert_kernel(texp_ref, nvalid_ref, hs_ref, wg_ref, wu_ref, wd_ref, ys_ref, wg_b, wu_b, wd_b):
    i = pl.program_id(0)
    fresh = jnp.logical_or(i == 0, texp_ref[i] != texp_ref[jnp.maximum(i - 1, 0)])

    @pl.when(fresh)
    def _():
        wg_b[...] = wg_ref[0].astype(BF16)
        wu_b[...] = wu_ref[0].astype(BF16)
        wd_b[...] = wd_ref[0].astype(BF16)

    @pl.when(i < nvalid_ref[0])
    def _():
        h = hs_ref[...]
        a = _dot(h, wg_b[...])
        b = _dot(h, wu_b[...])
        ys_ref[...] = _dot((a * _sigmoid(a) * b).astype(BF16), wd_b[...]).astype(BF16)


def _expert_call(layer, hs, tb, w_gate, w_up, w_down):
    tile_map = lambda i, texp, nv: (jnp.minimum(i, nv[0] - 1), 0)
    w_map = lambda i, texp, nv: (layer, texp[i], 0, 0)
    return pl.pallas_call(
        _expert_kernel,
        grid_spec=pltpu.PrefetchScalarGridSpec(
            num_scalar_prefetch=2,
            grid=(tb["nvalid"][0],),
            in_specs=[
                pl.BlockSpec((MOE_TILE, D_MODEL), tile_map),
                pl.BlockSpec((None, 1, D_MODEL, EXPERT_FF), w_map),
                pl.BlockSpec((None, 1, D_MODEL, EXPERT_FF), w_map),
                pl.BlockSpec((None, 1, EXPERT_FF, D_MODEL), w_map),
            ],
            out_specs=pl.BlockSpec((MOE_TILE, D_MODEL), tile_map),
            scratch_shapes=[pltpu.VMEM((D_MODEL, EXPERT_FF), BF16), pltpu.VMEM((D_MODEL, EXPERT_FF), BF16),
                            pltpu.VMEM((EXPERT_FF, D_MODEL), BF16)],
        ),
        out_shape=jax.ShapeDtypeStruct((NT_MAX * MOE_TILE, D_MODEL), BF16),
        compiler_params=_params("arbitrary"),
        name=f"moe_experts{layer}",
    )(tb["texp"], tb["nvalid"], hs, w_gate, w_up, w_down)


def _combine_kernel(split_out, ngroups_ref, src_ref, pos_ref, x_ref, g2_ref, ys_ref, *refs):
    o_refs, (buf, sems) = refs[:-2], refs[-2:]
    w = pl.program_id(0)
    slot = w % 2

    def fetch(window, s):
        def piece_copy(local_row, global_row):
            return pltpu.make_async_copy(ys_ref.at[pl.ds(global_row, MOE_CHUNK)],
                                         buf.at[s, pl.ds(local_row, MOE_CHUNK)], sems.at[s])

        _for_each_piece(window, ngroups_ref, src_ref, lambda lo, go: piece_copy(lo, go).start())

    @pl.when(w == 0)
    def _():
        buf[...] = jnp.zeros_like(buf)
        fetch(0, 0)

    @pl.when(w + 1 < NW)
    def _():
        fetch(w + 1, 1 - slot)

    pos = pos_ref[...]
    pa, pb, wa, wb = pos[:, 0:1], pos[:, 1:2], pos[:, 2:3], pos[:, 3:4]
    r = lax.broadcasted_iota(jnp.int32, (TM, R_LOC), 1).astype(F32)
    wt = jnp.where(r == pa, wa, jnp.where(r == pb, wb, 0.0)).astype(BF16)

    def wait_group(c, carry):
        pltpu.make_async_copy(ys_ref.at[pl.ds(0, PIECE_GROUP * MOE_CHUNK)],
                              buf.at[slot, pl.ds(0, PIECE_GROUP * MOE_CHUNK)], sems.at[slot]).wait()
        return carry

    lax.fori_loop(0, ngroups_ref[w], wait_group, 0)
    y = x_ref[...] + g2_ref[0] * _dot(wt, buf[slot])
    if split_out:
        @pl.when(w < NB_CTX)
        def _():
            o_refs[0][...] = y

        @pl.when(w >= NB_CTX)
        def _():
            o_refs[1][...] = y
    else:
        o_refs[0][...] = y


def _combine_call(layer, pos, x, mod, ys, tb, split_out):
    if split_out:
        out_specs = _split_specs(D_MODEL)
        out_shape = [jax.ShapeDtypeStruct((N_CTX, D_MODEL), F32), jax.ShapeDtypeStruct((N_LAT, D_MODEL), F32)]
    else:
        out_specs = [_row_spec(D_MODEL)]
        out_shape = [jax.ShapeDtypeStruct((N_TOK, D_MODEL), F32)]
    return pl.pallas_call(
        functools.partial(_combine_kernel, split_out),
        grid_spec=pltpu.PrefetchScalarGridSpec(
            num_scalar_prefetch=2,
            grid=(NW,),
            in_specs=[_row_spec(LANES), _row_spec(D_MODEL), _mod_spec(layer, 5), pl.BlockSpec(memory_space=pl.ANY)],
            out_specs=out_specs,
            scratch_shapes=[pltpu.VMEM((2, R_LOC, D_MODEL), BF16), pltpu.SemaphoreType.DMA((2,))],
        ),
        out_shape=out_shape,
        compiler_params=_params("arbitrary"),
        name=f"moe_combine{layer}",
    )(tb["ngroups"], tb["src"], pos, x, mod, ys)


def _moe_call(layer, h, routing, w_gate, w_up, w_down, x, mod, split_out=False):
    pos, pos_t, cnt = routing
    tb = _moe_tables(cnt)
    hs = _dispatch_call(layer, h, pos_t, tb)
    ys = _expert_call(layer, hs, tb, w_gate, w_up, w_down)
    return _combine_call(layer, pos, x, mod, ys, tb, split_out)


def _group_rms(x, g_ref, ones_blocks, n):
    ss = _dot((x * x).astype(BF16), ones_blocks)
    return x * lax.rsqrt(ss * (1.0 / n) + EPS) * g_ref[...]


def _inproj1_kernel(x_ref, g_ref, sh_ref, sc_ref, w_ref, qg_ref, kg_ref, c_ref, s1_ref, s2_ref,
                    q_ref, k_ref, v_ref, kn_ref, vf_ref, kn_buf):
    i = pl.program_id(0)
    h = (_rms(x_ref[...], g_ref[...]) * (1.0 + sc_ref[0]) + sh_ref[0]).astype(BF16)
    hw = DIFF_HEADS * DIFF_DV
    half = DIFF_DK // 2
    ones_blocks = _ones_blocks(DIFF_DK.bit_length() - 1)
    n_pairs = DIFF_HEADS // 2

    def project(tp):
        return (_dot(h, w_ref[:, tp * 2 * LANES:(tp + 1) * 2 * LANES]),
                _dot(h, w_ref[:, hw + tp * 2 * LANES:hw + (tp + 1) * 2 * LANES]))

    nxt = project(0)
    for tp in range(n_pairs):
        pq, pk = nxt
        if tp + 1 < n_pairs:
            nxt = project(tp + 1)
        for u in range(2):
            sl = slice((2 * tp + u) * LANES, (2 * tp + u + 1) * LANES)
            qt = _group_rms(pq[:, u * LANES:(u + 1) * LANES], qg_ref, ones_blocks, DIFF_DK)
            q_ref[:, sl] = _rope_tile(qt, c_ref, s1_ref, s2_ref, half).astype(BF16)
            kt = _group_rms(pk[:, u * LANES:(u + 1) * LANES], kg_ref, ones_blocks, DIFF_DK)
            k_ref[:, sl] = _rope_tile(kt, c_ref, s1_ref, s2_ref, half).astype(BF16)
            kn_buf[:, sl] = kt

    v = _dot(h, w_ref[:, 2 * hw:])
    v_ref[...] = v.astype(BF16)

    @pl.when(i < NB_CTX)
    def _():
        kn_ref[...] = kn_buf[...]
        vf_ref[...] = v


def _inproj1_call(x, mod, g1, w_qkv, qg, kg, tables):
    hw = DIFF_HEADS * DIFF_DV
    tspec = pl.BlockSpec((TM, LANES), lambda i: (_rope_block(i), 0))
    ctx_spec = pl.BlockSpec((TM, hw), lambda i: (jnp.minimum(i, NB_CTX - 1), 0))
    return pl.pallas_call(
        _inproj1_kernel,
        grid=(NB_TOK,),
        in_specs=[
            _row_spec(D_MODEL), _full_spec((1, D_MODEL)), _mod_spec(1, 0), _mod_spec(1, 1),
            _full_spec((D_MODEL, 3 * hw)), _full_spec((1, LANES)), _full_spec((1, LANES)), tspec, tspec, tspec,
        ],
        out_specs=[_row_spec(hw), _row_spec(hw), _row_spec(hw), ctx_spec, ctx_spec],
        out_shape=[
            jax.ShapeDtypeStruct((N_TOK, hw), BF16),
            jax.ShapeDtypeStruct((N_TOK, hw), BF16),
            jax.ShapeDtypeStruct((N_TOK, hw), BF16),
            jax.ShapeDtypeStruct((N_CTX, hw), F32),
            jax.ShapeDtypeStruct((N_CTX, hw), F32),
        ],
        scratch_shapes=[pltpu.VMEM((TM, DIFF_HEADS * DIFF_DV), F32)],
        compiler_params=_params("arbitrary"),
        name="inproj1",
    )(x, g1, mod, mod, w_qkv, qg, kg, *tables)


def _diff_attn_kernel(has_cache, lambda_init, tq, n_heads, qg_ref, kg_ref, *refs):
    if has_cache:
        q_ref, k_ref, v_ref, kc_ref, vc_ref, lq1, lk1, lq2, lk2, sg_ref, o_ref = refs
        k_refs, v_refs = [k_ref, kc_ref], [v_ref, vc_ref]
    else:
        q_ref, k_ref, v_ref, lq1, lk1, lq2, lk2, sg_ref, o_ref = refs
        k_refs, v_refs = [k_ref], [v_ref]
    lam = (jnp.exp(jnp.sum(lq1[...] * lk1[...], axis=-1, keepdims=True))
           - jnp.exp(jnp.sum(lq2[...] * lk2[...], axis=-1, keepdims=True)) + lambda_init)
    lo = lax.broadcasted_iota(jnp.int32, (1, LANES), 1) < DIFF_DK
    k_sq = _gain_sq_bound(kg_ref, DIFF_DK)
    if has_cache:
        for mask in (lo, jnp.logical_not(lo)):
            k_sq = jnp.maximum(k_sq, _max_sq_norm(kc_ref[...], mask))
    bound_sq = _gain_sq_bound(qg_ref, DIFF_DK) * k_sq

    def logits(r0, head):
        sl = slice(head * LANES, (head + 1) * LANES)
        q = q_ref[pl.ds(r0, tq), sl]
        zero = jnp.zeros_like(q)
        return ([_dot_nt(jnp.where(lo, q, zero), kr[:, sl]) for kr in k_refs],
                [_dot_nt(jnp.where(lo, zero, q), kr[:, sl]) for kr in k_refs])

    def finish(shifted, r0, head, s):
        sl = slice(head * LANES, (head + 1) * LANES)
        e0, l0 = _exps(s[0], shifted)
        e1, l1 = _exps(s[1], shifted)
        f0 = 1.0 / l0
        f1 = lam / l1
        o = None
        for a, b, vr in zip(e0, e1, v_refs):
            t = _dot((a * f0 - b * f1).astype(BF16), vr[:, sl])
            o = t if o is None else o + t
        o = o * lax.rsqrt(jnp.mean(o * o, axis=-1, keepdims=True) + EPS) * sg_ref[...] * (1.0 - lambda_init)
        o_ref[pl.ds(r0, tq), sl] = o.astype(BF16)

    _run_query_blocks(bound_sq, q_ref.shape[0] // tq, tq, n_heads, logits, finish)


def _diff_attn_ctx_call(lambda_init, gains, q, k, v, lams, sg):
    lspec = _full_spec((1, DIFF_DK))
    gspec = _full_spec((1, LANES))
    return pl.pallas_call(
        functools.partial(_diff_attn_kernel, False, lambda_init, SEQ, DIFF_HEADS),
        grid=(BATCH,),
        in_specs=[gspec, gspec] + [pl.BlockSpec((SEQ, DIFF_HEADS * LANES), lambda b: (b, 0))] * 3 + [lspec] * 4
        + [_full_spec((1, DIFF_DV))],
        out_specs=pl.BlockSpec((SEQ, DIFF_HEADS * LANES), lambda b: (b, 0)),
        out_shape=jax.ShapeDtypeStruct((N_CTX, DIFF_HEADS * DIFF_DV), BF16),
        compiler_params=_params("parallel"),
        name="diff_attn_ctx",
    )(*gains, q, k, v, *lams, sg)


def _diff_attn_lat_call(lambda_init, gains, q, k, v, kc, vc, lams, sg):
    k0 = N_CTX // DEC_SEQ
    lspec = _full_spec((1, DIFF_DK))
    gspec = _full_spec((1, LANES))
    seq_spec = pl.BlockSpec((DEC_SEQ, LANES), lambda b, h: (k0 + b, h))
    c_spec = pl.BlockSpec((PAST_LEN, LANES), lambda b, h: (b, h))
    return pl.pallas_call(
        functools.partial(_diff_attn_kernel, True, lambda_init, TQ, 1),
        grid=(DEC_BATCH, DIFF_HEADS),
        in_specs=[gspec, gspec, seq_spec, seq_spec, seq_spec, c_spec, c_spec] + [lspec] * 4 + [_full_spec((1, DIFF_DV))],
        out_specs=pl.BlockSpec((DEC_SEQ, LANES), lambda b, h: (b, h)),
        out_shape=jax.ShapeDtypeStruct((N_LAT, DIFF_HEADS * DIFF_DV), BF16),
        compiler_params=_params("parallel", "parallel"),
        name="diff_attn_lat",
    )(*gains, q, k, v, kc, vc, *lams, sg)


def _router_weights(we, wg, be, bg):
    pad = LANES - N_EXPERTS - N_GROUPS
    w = jnp.concatenate([we, wg, jnp.zeros((D_MODEL, pad), F32)], axis=1)
    b = jnp.concatenate([be, bg, jnp.zeros((pad,), F32)])[None]
    hi = w.astype(BF16)
    lo = (w - hi.astype(F32)).astype(BF16)
    return hi, lo, b


def kernel(x_prompt, x_sample, cache_mla_ckv, cache_mla_krope, cache_diff_k, cache_diff_v, c, c_ctx, ada_w, ada_b, norm1_g, norm2_g, ev_w_in, ev_conv_w, ev_conv_b, ev_conv_ln_g, ev_conv_ln_b, ev_q_norm_g, ev_w_qb, ev_kv_norm_g, ev_w_kvb, ev_qn_g, ev_kn_g, ev_w_o, od_w_qkv, od_qn_g, od_kn_g, od_lambda_q1, od_lambda_k1, od_lambda_q2, od_lambda_k2, od_subln_g, od_w_o, moe_wg, moe_bg, moe_we, moe_be, moe_w_gate, moe_w_up, moe_w_down):
    x_in = (x_prompt.reshape(N_CTX, D_MODEL), x_sample.reshape(N_LAT, D_MODEL))
    cond = jnp.concatenate([c_ctx[None, :], c, jnp.zeros((MOD_ROWS - 1 - DEC_BATCH, D_MODEL), F32)], axis=0)
    mod = _mod_call(cond, ada_w, ada_b).reshape(-1, 1, D_MODEL)

    o3 = 2 * CONV_CH + MLA_Q_RANK + MLA_KV_RANK
    w_in = jnp.concatenate([ev_w_in[0][:, :o3], jnp.zeros((D_MODEL, MLA_NOPE), F32), ev_w_in[0][:, o3:],
                            jnp.zeros((D_MODEL, LANES - MLA_QK), F32)], axis=1).astype(BF16)
    u, qn, ckv, kr = _inproj0_call(*x_in, mod, norm1_g[0][None], w_in, ev_q_norm_g[0][None], ev_kv_norm_g[0][None])
    conv = _conv_call(u, ev_conv_w[0], ev_conv_b[0][None], ev_conv_ln_g[0][None], ev_conv_ln_b[0][None])

    pad_qk = ((0, 0), (0, 0), (0, LANES - MLA_QK))
    wq = jnp.pad(ev_w_qb[0].reshape(MLA_Q_RANK, MLA_HEADS, MLA_QK), pad_qk).reshape(MLA_Q_RANK, -1).astype(BF16)
    wkv = ev_w_kvb[0].reshape(MLA_KV_RANK, MLA_HEADS, MLA_NOPE + MLA_V)
    wk = jnp.pad(wkv[..., :MLA_NOPE], ((0, 0), (0, 0), (0, LANES - MLA_NOPE))).reshape(MLA_KV_RANK, -1).astype(BF16)
    wv = wkv[..., MLA_NOPE:].reshape(MLA_KV_RANK, -1).astype(BF16)
    qg = jnp.pad(ev_qn_g[0] * (MLA_QK ** -0.5 * LOG2E), (0, LANES - MLA_QK))[None]
    kg = jnp.pad(ev_kn_g[0], (0, LANES - MLA_QK))[None]
    tables = _rope_tables(MLA_ROPE, (MLA_NOPE,), DEC_SEQ)
    q0, k0, v0 = _mla_prep_call(qn, ckv, kr, wq, wk, wv, qg, kg, tables, _rope_block)
    kr_cache = jnp.pad(cache_mla_krope[:, 0].reshape(N_CACHE, MLA_ROPE), ((0, 0), (MLA_NOPE, LANES - MLA_QK)))
    _, kc0, vc0 = _mla_prep_call(jnp.zeros((N_CACHE, MLA_Q_RANK), BF16), cache_mla_ckv[:, 0].reshape(N_CACHE, MLA_KV_RANK),
                                 kr_cache, wq, wk, wv, qg, kg, tables, lambda i: BLK_PER_LAT)
    attn = (_mla_attn_ctx_call(qg, kg, q0, k0, v0), _mla_attn_lat_call(qg, kg, q0, k0, v0, kc0, vc0))

    w_o = ev_w_o[0].astype(BF16)
    r_hi, r_lo, r_b = _router_weights(moe_we[0], moe_wg[0], moe_be[0], moe_bg[0])
    x, h2, *routing = _outproj_call(0, [(conv,), attn], [w_o[:CONV_CH], w_o[CONV_CH:]], x_in, mod, norm2_g[0][None], r_hi, r_lo, r_b)
    (x,) = _moe_call(0, h2, routing, moe_w_gate, moe_w_up, moe_w_down, x, mod)

    lambda_init = 0.8 - 0.6 * math.exp(-0.3 * 1)
    qg1 = jnp.tile(od_qn_g[0] * (DIFF_DK ** -0.5 * LOG2E), 2)[None]
    kg1 = jnp.tile(od_kn_g[0], 2)[None]
    tables1 = _rope_tables(DIFF_DK, (0, DIFF_DK), DEC_SEQ)
    q1, k1, v1, kn1, vf1 = _inproj1_call(x, mod, norm1_g[1][None], od_w_qkv[0].astype(BF16), qg1, kg1, tables1)
    lams = [od_lambda_q1[0][None], od_lambda_k1[0][None], od_lambda_q2[0][None], od_lambda_k2[0][None]]
    sg = od_subln_g[0][None]
    hw = DIFF_HEADS * DIFF_DV
    kc1 = cache_diff_k[:, 0].reshape(N_CACHE, hw).astype(BF16)
    vc1 = cache_diff_v[:, 0].reshape(N_CACHE, hw).astype(BF16)
    attn1 = (_diff_attn_ctx_call(lambda_init, (qg1, kg1), q1, k1, v1, lams, sg),
             _diff_attn_lat_call(lambda_init, (qg1, kg1), q1, k1, v1, kc1, vc1, lams, sg))
    r_hi, r_lo, r_b = _router_weights(moe_we[1], moe_wg[1], moe_be[1], moe_bg[1])
    x, h2, *routing = _outproj_call(1, [attn1], [od_w_o[0].astype(BF16)], (x,), mod, norm2_g[1][None], r_hi, r_lo, r_b)
    y_ctx, y_lat = _moe_call(1, h2, routing, moe_w_gate, moe_w_up, moe_w_down, x, mod, split_out=True)

    y_prompt = y_ctx.reshape(BATCH, SEQ, D_MODEL)
    y_sample = y_lat.reshape(DEC_BATCH, DEC_SEQ, D_MODEL)
    new_mla_ckv = ckv[:N_CTX].reshape(BATCH, 1, SEQ, MLA_KV_RANK)
    new_mla_krope = kr[:N_CTX, MLA_NOPE:MLA_QK].reshape(BATCH, 1, SEQ, MLA_ROPE)
    new_diff_k = kn1.reshape(BATCH, 1, SEQ, DIFF_HEADS, 2, DIFF_DK)
    new_diff_v = vf1.reshape(BATCH, 1, SEQ, DIFF_HEADS, DIFF_DV)
    return (y_prompt, y_sample, new_mla_ckv, new_mla_krope, new_diff_k, new_diff_v)
```

```python
import functools
import math

import jax
import jax.numpy as jnp
from jax import lax
from jax.experimental import pallas as pl
from jax.experimental.pallas import tpu as pltpu

F32 = jnp.float32
BF16 = jnp.bfloat16

D_MODEL = 1024
BATCH = 16
SEQ = 256
DEC_BATCH = 8
DEC_SEQ = 2048
PAST_LEN = 256
GRID_W = 64
ROPE_BASE = 10000.0
EPS = 1e-6

CONV_CH = 512
CONV_WIDTH = 31
CONV_PAD = 15
MLA_HEADS = 8
MLA_NOPE = 64
MLA_ROPE = 32
MLA_QK = 96
MLA_V = 64
MLA_Q_RANK = 384
MLA_KV_RANK = 256
DIFF_HEADS = 8
DIFF_DK = 64
DIFF_DV = 128
N_GROUPS = 4
EXPERTS_PER_GROUP = 8
N_EXPERTS = 32
EXPERT_FF = 256

LANES = 128
SUBLANES = 8
N_CTX = BATCH * SEQ
N_LAT = DEC_BATCH * DEC_SEQ
N_TOK = N_CTX + N_LAT
N_CACHE = DEC_BATCH * PAST_LEN
MOD_ROWS = 16
VMEM_LIMIT = 48 * 1024 * 1024

TM = 512
NB_CTX = N_CTX // TM
NB_TOK = N_TOK // TM
BLK_PER_LAT = DEC_SEQ // TM
TQ = 256
CONV_T = 256
CONV_HALO = 16
CONV_RB = 32
CONV_SHIFT_ROWS = CONV_T + 2 * CONV_HALO - SUBLANES


def _params(*sem):
    return pltpu.CompilerParams(dimension_semantics=sem, vmem_limit_bytes=VMEM_LIMIT)


def _sigmoid(x):
    return 1.0 / (1.0 + jnp.exp(-x))


def _rms(x, g):
    return x * lax.rsqrt(jnp.mean(x * x, axis=-1, keepdims=True) + EPS) * g


def _dot(a, b):
    return jnp.dot(a, b, preferred_element_type=F32)


def _dot_nt(a, b):
    return lax.dot_general(a, b, (((1,), (1,)), ((), ())), preferred_element_type=F32)


def _seq_of_block(i):
    return jnp.where(i < NB_CTX, 0, 1 + (i - NB_CTX) // BLK_PER_LAT)


def _mod_spec(layer, slot):
    return pl.BlockSpec((1, 1, D_MODEL), lambda i, *_: ((layer * MOD_ROWS + _seq_of_block(i)) * 6 + slot, 0, 0))


def _row_spec(width, tm=TM):
    return pl.BlockSpec((tm, width), lambda i, *_: (i, 0))


def _ctx_spec(width):
    return pl.BlockSpec((TM, width), lambda i, *_: (jnp.minimum(i, NB_CTX - 1), 0))


def _lat_spec(width):
    return pl.BlockSpec((TM, width), lambda i, *_: (jnp.maximum(i - NB_CTX, 0), 0))


def _split_specs(width):
    return [_ctx_spec(width), _lat_spec(width)]


def _pick(ctx_ref, lat_ref):
    return jnp.where(pl.program_id(0) < NB_CTX, ctx_ref[...], lat_ref[...])


def _ones_blocks(group_log2):
    r = lax.broadcasted_iota(jnp.int32, (LANES, LANES), 0) >> group_log2
    c = lax.broadcasted_iota(jnp.int32, (LANES, LANES), 1) >> group_log2
    return jnp.where(r == c, 1.0, 0.0).astype(BF16)


def _full_spec(shape):
    nd = len(shape)
    return pl.BlockSpec(shape, lambda *_: (0,) * nd)


def _mod_kernel(c_ref, w_ref, b_ref, o_ref):
    c = c_ref[...]
    s = (c * _sigmoid(c)).astype(BF16)
    o_ref[0] = _dot(s, w_ref[0].astype(BF16)) + b_ref[0]


def _mod_call(cond, ada_w, ada_b):
    depth = ada_w.shape[0]
    tn = 1024
    return pl.pallas_call(
        _mod_kernel,
        grid=(depth, 6 * D_MODEL // tn),
        in_specs=[
            pl.BlockSpec((MOD_ROWS, D_MODEL), lambda l, j: (0, 0)),
            pl.BlockSpec((1, D_MODEL, tn), lambda l, j: (l, 0, j)),
            pl.BlockSpec((1, 1, tn), lambda l, j: (l, 0, j)),
        ],
        out_specs=pl.BlockSpec((1, MOD_ROWS, tn), lambda l, j: (l, 0, j)),
        out_shape=jax.ShapeDtypeStruct((depth, MOD_ROWS, 6 * D_MODEL), F32),
        compiler_params=_params("parallel", "parallel"),
        name="adaln_mod",
    )(cond, ada_w, ada_b.reshape(depth, 1, 6 * D_MODEL))


EV_W = 1792


def _inproj0_kernel(xc_ref, xl_ref, g_ref, sh_ref, sc_ref, w_ref, qg_ref, kvg_ref, u_ref, qn_ref, ckv_ref, kr_ref):
    h = _rms(_pick(xc_ref, xl_ref), g_ref[...]) * (1.0 + sc_ref[0]) + sh_ref[0]
    p = _dot(h.astype(BF16), w_ref[...])
    u_ref[...] = p[:, :CONV_CH] * _sigmoid(p[:, CONV_CH:2 * CONV_CH])
    o1 = 2 * CONV_CH
    o2 = o1 + MLA_Q_RANK
    o3 = o2 + MLA_KV_RANK
    qn_ref[...] = _rms(p[:, o1:o2], qg_ref[...]).astype(BF16)
    ckv_ref[...] = _rms(p[:, o2:o3], kvg_ref[...])
    kr_ref[...] = p[:, o3:o3 + LANES]


def _inproj0_call(x_ctx, x_lat, mod, g1, w_in, qg, kvg):
    return pl.pallas_call(
        _inproj0_kernel,
        grid=(NB_TOK,),
        in_specs=_split_specs(D_MODEL) + [
            _full_spec((1, D_MODEL)), _mod_spec(0, 0), _mod_spec(0, 1),
            _full_spec((D_MODEL, EV_W)), _full_spec((1, MLA_Q_RANK)), _full_spec((1, MLA_KV_RANK)),
        ],
        out_specs=[_row_spec(CONV_CH), _row_spec(MLA_Q_RANK), _row_spec(MLA_KV_RANK), _row_spec(LANES)],
        out_shape=[
            jax.ShapeDtypeStruct((N_TOK, CONV_CH), F32),
            jax.ShapeDtypeStruct((N_TOK, MLA_Q_RANK), BF16),
            jax.ShapeDtypeStruct((N_TOK, MLA_KV_RANK), F32),
            jax.ShapeDtypeStruct((N_TOK, LANES), F32),
        ],
        compiler_params=_params("parallel"),
        name="inproj0",
    )(x_ctx, x_lat, g1, mod, mod, w_in, qg, kvg)


CONV_CHUNKS_PER_LAT = DEC_SEQ // CONV_T
CONV_NB_CTX = N_CTX // CONV_T
CONV_NB = N_TOK // CONV_T


def _conv_kernel(prev_ref, cur_ref, nxt_ref, w_ref, b_ref, lg_ref, lb_ref, o_ref, buf, shifted):
    i = pl.program_id(0)
    j = (i - CONV_NB_CTX) % CONV_CHUNKS_PER_LAT
    first = jnp.logical_or(i < CONV_NB_CTX, j == 0)
    last = jnp.logical_or(i < CONV_NB_CTX, j == CONV_CHUNKS_PER_LAT - 1)
    buf[0:CONV_HALO, :] = jnp.where(first, 0.0, prev_ref[...])
    buf[CONV_HALO:CONV_HALO + CONV_T, :] = cur_ref[...]
    buf[CONV_HALO + CONV_T:, :] = jnp.where(last, 0.0, nxt_ref[...])
    for s in range(1, SUBLANES):
        shifted[s - 1] = buf[s:s + CONV_SHIFT_ROWS, :]
    for rb in range(CONV_T // CONV_RB):
        acc = jnp.zeros((CONV_RB, CONV_CH), F32)
        for k in range(CONV_WIDTH):
            row = rb * CONV_RB + CONV_HALO - CONV_PAD + k
            s, r0 = row % SUBLANES, row - row % SUBLANES
            src = buf[r0:r0 + CONV_RB, :] if s == 0 else shifted[s - 1, r0:r0 + CONV_RB, :]
            acc = acc + w_ref[k:k + 1, :] * src
        y = acc + b_ref[...]
        mu = jnp.mean(y, axis=-1, keepdims=True)
        yc = y - mu
        var = jnp.mean(yc * yc, axis=-1, keepdims=True)
        z = yc * lax.rsqrt(var + EPS) * lg_ref[...] + lb_ref[...]
        o_ref[rb * CONV_RB:(rb + 1) * CONV_RB, :] = (z * _sigmoid(z)).astype(BF16)


def _conv_call(u, w, b, lg, lb):
    hb = CONV_T // CONV_HALO
    n_halo_blocks = N_TOK // CONV_HALO
    return pl.pallas_call(
        _conv_kernel,
        grid=(CONV_NB,),
        in_specs=[
            pl.BlockSpec((CONV_HALO, CONV_CH), lambda i: (jnp.maximum(i * hb - 1, 0), 0)),
            pl.BlockSpec((CONV_T, CONV_CH), lambda i: (i, 0)),
            pl.BlockSpec((CONV_HALO, CONV_CH), lambda i: (jnp.minimum((i + 1) * hb, n_halo_blocks - 1), 0)),
            _full_spec((CONV_WIDTH, CONV_CH)), _full_spec((1, CONV_CH)),
            _full_spec((1, CONV_CH)), _full_spec((1, CONV_CH)),
        ],
        out_specs=pl.BlockSpec((CONV_T, CONV_CH), lambda i: (i, 0)),
        out_shape=jax.ShapeDtypeStruct((N_TOK, CONV_CH), BF16),
        scratch_shapes=[pltpu.VMEM((CONV_T + 2 * CONV_HALO, CONV_CH), F32),
                        pltpu.VMEM((SUBLANES - 1, CONV_SHIFT_ROWS, CONV_CH), F32)],
        compiler_params=_params("parallel"),
        name="conformer_conv",
    )(u, u, u, w, b, lg, lb)


def _rope_tile(x, c_ref, s1_ref, s2_ref, half):
    up = pltpu.roll(x, LANES - half, 1)
    dn = pltpu.roll(x, half, 1)
    return x * c_ref[...] + up * s1_ref[...] + dn * s2_ref[...]


def _rope_tables(rot_dim, lane_starts, n_tok):
    rows = n_tok // GRID_W
    row_ids = jnp.repeat(jnp.arange(rows, dtype=F32), GRID_W)
    col_ids = jnp.tile(jnp.arange(GRID_W, dtype=F32), rows)
    quarter = rot_dim // 4
    half = rot_dim // 2
    freqs = jnp.power(ROPE_BASE, -jnp.arange(quarter, dtype=F32) / quarter)
    ang = jnp.concatenate([row_ids[:, None] * freqs, col_ids[:, None] * freqs], axis=-1)
    cos, sin = jnp.cos(ang), jnp.sin(ang)
    one = lambda n: jnp.ones((n_tok, n), F32)
    zero = lambda n: jnp.zeros((n_tok, n), F32)
    c, s1, s2, at = [], [], [], 0
    for st in lane_starts:
        c += [one(st - at), cos, cos]
        s1 += [zero(st - at), -sin, zero(half)]
        s2 += [zero(st - at), zero(half), sin]
        at = st + rot_dim
    c, s1, s2 = (jnp.concatenate(t + [fill(LANES - at)], axis=1) for t, fill in ((c, one), (s1, zero), (s2, zero)))
    ident = (jnp.ones((TM, LANES), F32), jnp.zeros((TM, LANES), F32), jnp.zeros((TM, LANES), F32))
    return tuple(jnp.concatenate([t, e], axis=0) for t, e in zip((c, s1, s2), ident))


def _rope_block(i):
    return jnp.where(i < NB_CTX, BLK_PER_LAT, (i - NB_CTX) % BLK_PER_LAT)


def _mla_prep_kernel(qn_ref, ckv_ref, kr_ref, wq_ref, wk_ref, wv_ref, qg_ref, kg_ref, c_ref, s1_ref, s2_ref,
                     q_ref, k_ref, v_ref):
    qn = qn_ref[...]
    ckv = ckv_ref[...].astype(BF16)
    kr = kr_ref[...]
    half = MLA_ROPE // 2
    ones = _ones_blocks(LANES.bit_length() - 1)
    n_pairs = MLA_HEADS // 2

    def project(hp):
        pair = slice(hp * 2 * LANES, (hp + 1) * 2 * LANES)
        return _dot(qn, wq_ref[:, pair]), _dot(ckv, wk_ref[:, pair])

    nxt = project(0)
    for hp in range(n_pairs):
        qf, kf = nxt
        nxt = project(hp + 1) if hp + 1 < n_pairs else None
        for u in range(2):
            sl = slice((2 * hp + u) * LANES, (2 * hp + u + 1) * LANES)
            qh = _group_rms(qf[:, u * LANES:(u + 1) * LANES], qg_ref, ones, MLA_QK)
            q_ref[:, sl] = _rope_tile(qh, c_ref, s1_ref, s2_ref, half).astype(BF16)
            kh = _group_rms(kf[:, u * LANES:(u + 1) * LANES] + kr, kg_ref, ones, MLA_QK)
            k_ref[:, sl] = _rope_tile(kh, c_ref, s1_ref, s2_ref, half).astype(BF16)
    v_ref[...] = _dot(ckv, wv_ref[...]).astype(BF16)


def _mla_prep_call(qn, ckv, kr, wq, wk, wv, qg, kg, tables, rope_block_fn):
    n = qn.shape[0]
    hw = MLA_HEADS * LANES
    tspec = pl.BlockSpec((TM, LANES), lambda i: (rope_block_fn(i), 0))
    return pl.pallas_call(
        _mla_prep_kernel,
        grid=(n // TM,),
        in_specs=[
            _row_spec(MLA_Q_RANK), _row_spec(MLA_KV_RANK), _row_spec(LANES),
            _full_spec((MLA_Q_RANK, hw)), _full_spec((MLA_KV_RANK, hw)), _full_spec((MLA_KV_RANK, MLA_HEADS * MLA_V)),
            _full_spec((1, LANES)), _full_spec((1, LANES)), tspec, tspec, tspec,
        ],
        out_specs=[_row_spec(hw), _row_spec(hw), _row_spec(MLA_HEADS * MLA_V)],
        out_shape=[
            jax.ShapeDtypeStruct((n, hw), BF16),
            jax.ShapeDtypeStruct((n, hw), BF16),
            jax.ShapeDtypeStruct((n, MLA_HEADS * MLA_V), BF16),
        ],
        compiler_params=_params("parallel"),
        name="mla_prep",
    )(qn, ckv, kr, wq, wk, wv, qg, kg, *tables)


LOG2E = 1.4426950408889634
EXP2_SAFE = 100.0
BF16_NORM_MARGIN = 1.02


def _max_sq_norm(x, mask=None):
    xf = x.astype(F32)
    xx = xf * xf
    if mask is not None:
        xx = jnp.where(mask, xx, 0.0)
    return jnp.max(jnp.sum(xx, axis=-1, keepdims=True), axis=0, keepdims=True)


def _gain_sq_bound(g_ref, dims):
    g = g_ref[...]
    return jnp.max(g * g, axis=-1, keepdims=True) * (dims * BF16_NORM_MARGIN)


def _exps(s_list, shifted):
    if shifted:
        m = s_list[0].max(axis=-1, keepdims=True)
        for s in s_list[1:]:
            m = jnp.maximum(m, s.max(axis=-1, keepdims=True))
        es = [jnp.exp2(s - m) for s in s_list]
    else:
        es = [jnp.exp2(s) for s in s_list]
    l = es[0].sum(axis=-1, keepdims=True)
    for e in es[1:]:
        l = l + e.sum(axis=-1, keepdims=True)
    return es, l


def _run_query_blocks(bound_sq, n_blocks, tq, units_per_block, logits, finish):
    per_trip = 4 if n_blocks % 4 == 0 else 1

    def trip(shifted, t, carry):
        units = [(pl.multiple_of((t * per_trip + b) * tq, tq), u)
                 for b in range(per_trip) for u in range(units_per_block)]
        pending = logits(*units[0])
        for k, unit in enumerate(units):
            nxt = logits(*units[k + 1]) if k + 1 < len(units) else None
            finish(shifted, *unit, pending)
            pending = nxt
        return carry

    safe = bound_sq[0, 0] <= EXP2_SAFE * EXP2_SAFE

    @pl.when(safe)
    def _():
        lax.fori_loop(0, n_blocks // per_trip, functools.partial(trip, False), 0)

    @pl.when(jnp.logical_not(safe))
    def _():
        lax.fori_loop(0, n_blocks // per_trip, functools.partial(trip, True), 0)


def _mla_attn_kernel(has_cache, tq, n_pairs, qg_ref, kg_ref, *refs):
    if has_cache:
        q_ref, k_ref, v_ref, kc_ref, vc_ref, o_ref = refs
        k_refs, v_refs = [k_ref, kc_ref], [v_ref, vc_ref]
    else:
        q_ref, k_ref, v_ref, o_ref = refs
        k_refs, v_refs = [k_ref], [v_ref]
    bound_sq = _gain_sq_bound(qg_ref, MLA_QK) * _gain_sq_bound(kg_ref, MLA_QK)

    def logits(r0, head):
        sl = slice(head * LANES, (head + 1) * LANES)
        q = q_ref[pl.ds(r0, tq), sl]
        return [_dot_nt(q, kr[:, sl]) for kr in k_refs]

    held = []

    def finish(shifted, r0, head, s_list):
        vsl = slice(head // 2 * LANES, (head // 2 + 1) * LANES)
        es, l = _exps(s_list, shifted)
        o = _dot(es[0].astype(BF16), v_refs[0][:, vsl])
        for e, vr in zip(es[1:], v_refs[1:]):
            o = o + _dot(e.astype(BF16), vr[:, vsl])
        o = o / l
        if head % 2 == 0:
            held.append(o)
        else:
            lane = lax.broadcasted_iota(jnp.int32, o.shape, 1)
            o_ref[pl.ds(r0, tq), vsl] = jnp.where(lane < MLA_V, held.pop(), o).astype(BF16)

    _run_query_blocks(bound_sq, q_ref.shape[0] // tq, tq, 2 * n_pairs, logits, finish)


def _mla_attn_ctx_call(qg, kg, q, k, v):
    return pl.pallas_call(
        functools.partial(_mla_attn_kernel, False, SEQ, MLA_HEADS // 2),
        grid=(BATCH,),
        in_specs=[
            _full_spec((1, LANES)), _full_spec((1, LANES)),
            pl.BlockSpec((SEQ, MLA_HEADS * LANES), lambda b: (b, 0)),
            pl.BlockSpec((SEQ, MLA_HEADS * LANES), lambda b: (b, 0)),
            pl.BlockSpec((SEQ, MLA_HEADS * MLA_V), lambda b: (b, 0)),
        ],
        out_specs=pl.BlockSpec((SEQ, MLA_HEADS * MLA_V), lambda b: (b, 0)),
        out_shape=jax.ShapeDtypeStruct((N_CTX, MLA_HEADS * MLA_V), BF16),
        compiler_params=_params("parallel"),
        name="mla_attn_ctx",
    )(qg, kg, q, k, v)


def _mla_attn_lat_call(qg, kg, q, k, v, kc, vc):
    k0 = N_CTX // DEC_SEQ
    return pl.pallas_call(
        functools.partial(_mla_attn_kernel, True, TQ, 1),
        grid=(DEC_BATCH, MLA_HEADS // 2),
        in_specs=[
            _full_spec((1, LANES)), _full_spec((1, LANES)),
            pl.BlockSpec((DEC_SEQ, 2 * LANES), lambda b, hp: (k0 + b, hp)),
            pl.BlockSpec((DEC_SEQ, 2 * LANES), lambda b, hp: (k0 + b, hp)),
            pl.BlockSpec((DEC_SEQ, LANES), lambda b, hp: (k0 + b, hp)),
            pl.BlockSpec((PAST_LEN, 2 * LANES), lambda b, hp: (b, hp)),
            pl.BlockSpec((PAST_LEN, LANES), lambda b, hp: (b, hp)),
        ],
        out_specs=pl.BlockSpec((DEC_SEQ, LANES), lambda b, hp: (b, hp)),
        out_shape=jax.ShapeDtypeStruct((N_LAT, MLA_HEADS * MLA_V), BF16),
        compiler_params=_params("parallel", "parallel"),
        name="mla_attn_lat",
    )(qg, kg, q, k, v, kc, vc)


ROUTE_ROWS = 40


def _route(logits_t):
    row = lax.broadcasted_iota(jnp.int32, logits_t.shape, 0)
    row_f = row.astype(F32)
    big = float(LANES)
    neg = -jnp.inf
    gmask = jnp.logical_and(row >= N_EXPERTS, row < N_EXPERTS + N_GROUPS)
    gl = jnp.where(gmask, logits_t, neg)
    ge = jnp.exp(gl - jnp.max(gl, axis=0, keepdims=True))
    gp = ge / jnp.sum(ge, axis=0, keepdims=True)
    gate = jnp.max(gp, axis=0, keepdims=True)
    gidx = jnp.min(jnp.where(jnp.logical_and(gmask, gp == gate), row_f, big), axis=0, keepdims=True) - N_EXPERTS
    lo = gidx * EXPERTS_PER_GROUP
    emask = jnp.logical_and(row_f >= lo, row_f < lo + EXPERTS_PER_GROUP)
    el = jnp.where(emask, logits_t, neg)
    ee = jnp.exp(el - jnp.max(el, axis=0, keepdims=True))
    ep = jnp.where(emask, ee / jnp.sum(ee, axis=0, keepdims=True), -1.0)
    p1 = jnp.max(ep, axis=0, keepdims=True)
    i1 = jnp.min(jnp.where(ep == p1, row_f, big), axis=0, keepdims=True)
    ep2 = jnp.where(row_f == i1, -1.0, ep)
    p2 = jnp.max(ep2, axis=0, keepdims=True)
    i2 = jnp.min(jnp.where(ep2 == p2, row_f, big), axis=0, keepdims=True)
    den = p1 + p2
    return jnp.where(row_f == i1, gate * p1 / den, 0.0) + jnp.where(row_f == i2, gate * p2 / den, 0.0)


def _window_positions(cw_t):
    n = cw_t.shape[1]
    cw = jnp.concatenate([cw_t, jnp.zeros((LANES - ROUTE_ROWS, n), F32)], axis=0)
    a = cw > 0.0
    a_bf = jnp.where(a, 1.0, 0.0).astype(BF16)
    earlier = jnp.where(lax.broadcasted_iota(jnp.int32, (n, n), 0) < lax.broadcasted_iota(jnp.int32, (n, n), 1), 1.0, 0.0)
    rank = _dot(a_bf, earlier.astype(BF16))
    cnt_col = jnp.sum(jnp.where(a, 1.0, 0.0), axis=1, keepdims=True)
    padc = jnp.floor((cnt_col + (MOE_CHUNK - 1)) * (1.0 / MOE_CHUNK)) * MOE_CHUNK
    lower = jnp.where(lax.broadcasted_iota(jnp.int32, (LANES, LANES), 1) < lax.broadcasted_iota(jnp.int32, (LANES, LANES), 0),
                      1.0, 0.0)
    loff = _dot(lower.astype(BF16), jnp.broadcast_to(padc, (LANES, LANES)).astype(BF16))
    lpos = rank + jnp.concatenate([loff] * (n // LANES), axis=1)
    pa = jnp.min(jnp.where(a, lpos, float(R_LOC)), axis=0, keepdims=True)
    pb = jnp.max(jnp.where(a, lpos, -1.0), axis=0, keepdims=True)
    wa = jnp.sum(jnp.where(jnp.logical_and(a, lpos == pa), cw, 0.0), axis=0, keepdims=True)
    wb = jnp.sum(jnp.where(jnp.logical_and(a, lpos == pb), cw, 0.0), axis=0, keepdims=True)
    pos_t = jnp.concatenate([pa, pb, wa, wb, jnp.zeros((SUBLANES - 4, n), F32)], axis=0)
    cnt = _dot_nt(jnp.ones((SUBLANES, n), BF16), a_bf)[0:1]
    return pos_t, cnt


def _outproj_kernel(arity, *refs):
    refs = list(refs)
    rows = []
    for n in arity:
        rows.append(refs[0][...] if n == 1 else _pick(refs[0], refs[1]))
        refs = refs[n:]
    n_act = len(arity) - 1
    w_refs, refs = refs[:n_act], refs[n_act:]
    g1_ref, n2_ref, sh2_ref, sc2_ref, rh_ref, rl_ref, rb_ref, xo_ref, h_ref, pos_ref, pt_ref, cnt_ref = refs
    out = _dot(jnp.concatenate(rows[:n_act], axis=1), jnp.concatenate([w_ref[...] for w_ref in w_refs], axis=0))
    x = rows[n_act] + g1_ref[0] * out
    xo_ref[...] = x
    h = _rms(x, n2_ref[...]) * (1.0 + sc2_ref[0]) + sh2_ref[0]
    h_hi = h.astype(BF16)
    h_ref[...] = h_hi
    h_lo = (h - h_hi.astype(F32)).astype(BF16)
    logits = _dot(h_hi, rh_ref[...]) + (_dot(h_hi, rl_ref[...]) + _dot(h_lo, rh_ref[...])) + rb_ref[...]
    pos_t, cnt = _window_positions(_route(logits.T[:ROUTE_ROWS]))
    pt_ref[0] = pos_t
    pos_ref[...] = jnp.concatenate([pos_t, jnp.zeros((LANES - SUBLANES, pos_t.shape[1]), F32)], axis=0).T
    cnt_ref[0] = cnt


def _outproj_call(layer, acts, ws, x, mod, n2, r_hi, r_lo, r_b):
    operands = list(acts) + [x]
    row_specs = []
    for op in operands:
        row_specs += [_row_spec(op[0].shape[1])] if len(op) == 1 else _split_specs(op[0].shape[1])
    flat = [a for op in operands for a in op]
    return pl.pallas_call(
        functools.partial(_outproj_kernel, tuple(len(op) for op in operands)),
        grid=(NB_TOK,),
        in_specs=(
            row_specs + [_full_spec(w.shape) for w in ws]
            + [_mod_spec(layer, 2), _full_spec((1, D_MODEL)), _mod_spec(layer, 3),
               _mod_spec(layer, 4), _full_spec((D_MODEL, LANES)), _full_spec((D_MODEL, LANES)), _full_spec((1, LANES))]
        ),
        out_specs=[_row_spec(D_MODEL), _row_spec(D_MODEL), _row_spec(LANES),
                   pl.BlockSpec((1, SUBLANES, TM), lambda i: (i, 0, 0)), pl.BlockSpec((1, 1, LANES), lambda i: (i, 0, 0))],
        out_shape=[
            jax.ShapeDtypeStruct((N_TOK, D_MODEL), F32),
            jax.ShapeDtypeStruct((N_TOK, D_MODEL), BF16),
            jax.ShapeDtypeStruct((N_TOK, LANES), F32),
            jax.ShapeDtypeStruct((NB_TOK, SUBLANES, TM), F32),
            jax.ShapeDtypeStruct((NB_TOK, 1, LANES), F32),
        ],
        compiler_params=_params("parallel"),
        name=f"outproj{layer}",
    )(*flat, *ws, mod, n2, mod, mod, r_hi, r_lo, r_b)


MOE_CHUNK = 16
MOE_TILE = 512
NW = N_TOK // TM
R_LOC = 2 * TM + N_EXPERTS * MOE_CHUNK
R_SORTED_MAX = 2 * N_TOK + NW * N_EXPERTS * (MOE_CHUNK - 1) + N_EXPERTS * (MOE_TILE - 1)
NT_MAX = -(-R_SORTED_MAX // MOE_TILE)
P_MAX = R_LOC // MOE_CHUNK
PIECE_GROUP = 8
TRASH_ROW = NT_MAX * MOE_TILE
SORTED_ROWS = TRASH_ROW + PIECE_GROUP * MOE_CHUNK


def _moe_tables(cnt):
    cnt = cnt[:, 0, :N_EXPERTS].astype(jnp.int32)
    padc = (cnt + (MOE_CHUNK - 1)) // MOE_CHUNK * MOE_CHUNK
    loff = jnp.cumsum(padc, axis=1) - padc
    tot = jnp.sum(padc, axis=0)
    seg = (tot + (MOE_TILE - 1)) // MOE_TILE * MOE_TILE
    seg_end = jnp.cumsum(seg)
    base = seg_end - seg
    goff = base[None, :] + jnp.cumsum(padc, axis=0) - padc
    tile_start = jnp.arange(NT_MAX, dtype=jnp.int32) * MOE_TILE
    texp = jnp.minimum(jnp.sum((tile_start[:, None] >= seg_end[None, :]).astype(jnp.int32), axis=1), N_EXPERTS - 1)
    nvalid = (seg_end[-1:] // MOE_TILE).astype(jnp.int32)
    p_row = jnp.arange(P_MAX, dtype=jnp.int32)[None, :, None] * MOE_CHUNK
    inside = jnp.logical_and(p_row >= loff[:, None, :], p_row < (loff + padc)[:, None, :])
    grow = jnp.sum(jnp.where(inside, goff[:, None, :] + p_row - loff[:, None, :], 0), axis=-1)
    used = jnp.any(inside, axis=-1)
    filler = TRASH_ROW + (jnp.arange(P_MAX, dtype=jnp.int32) % PIECE_GROUP)[None, :] * MOE_CHUNK
    npieces = jnp.sum(padc, axis=1) // MOE_CHUNK
    i32 = lambda a: a.reshape(-1).astype(jnp.int32)
    return dict(dst=i32(jnp.where(used, grow, filler)), src=i32(jnp.where(used, grow, 0)),
                ngroups=i32((npieces + (PIECE_GROUP - 1)) // PIECE_GROUP), tail_start=i32(base + tot),
                tail_chunks=i32((seg - tot) // MOE_CHUNK), texp=texp, nvalid=nvalid)


def _for_each_piece(w, ngroups_ref, grow_ref, fn):
    n = ngroups_ref[w]

    def group(j, carry):
        for u in range(PIECE_GROUP):
            p = j * PIECE_GROUP + u
            fn(pl.multiple_of(p * MOE_CHUNK, MOE_CHUNK), pl.multiple_of(grow_ref[w * P_MAX + p], MOE_CHUNK), u % 2)
        return carry

    lax.fori_loop(0, n, group, 0)
    return n


def _dispatch_kernel(ngroups_ref, dst_ref, tstart_ref, tchunk_ref, h_ref, pt_ref, hs_ref, buf, zbuf, sems):
    w = pl.program_id(0)
    slot = w % 2
    pos_t = pt_ref[0]
    r = lax.broadcasted_iota(jnp.int32, (R_LOC, TM), 0).astype(F32)
    sel = jnp.logical_or(r == pos_t[0:1, :], r == pos_t[1:2, :])
    buf[slot] = _dot(jnp.where(sel, 1.0, 0.0).astype(BF16), h_ref[...]).astype(BF16)

    def wait_groups(window, s):
        def wait_group(c, carry):
            pltpu.make_async_copy(buf.at[s, pl.ds(0, PIECE_GROUP * MOE_CHUNK)],
                                  hs_ref.at[pl.ds(0, PIECE_GROUP * MOE_CHUNK)], sems.at[s]).wait()
            return carry

        lax.fori_loop(0, ngroups_ref[window], wait_group, 0)

    @pl.when(w > 0)
    def _():
        wait_groups(w - 1, 1 - slot)

    def piece_copy(local_row, global_row):
        return pltpu.make_async_copy(buf.at[slot, pl.ds(local_row, MOE_CHUNK)],
                                     hs_ref.at[pl.ds(global_row, MOE_CHUNK)], sems.at[slot])

    _for_each_piece(w, ngroups_ref, dst_ref, lambda lo, go, prio: piece_copy(lo, go).start(priority=prio))

    @pl.when(w == NW - 1)
    def _():
        wait_groups(w, slot)
        sem = sems.at[slot]
        zbuf[...] = jnp.zeros_like(zbuf)

        def zero_copy(global_row):
            return pltpu.make_async_copy(zbuf, hs_ref.at[pl.ds(global_row, MOE_CHUNK)], sem)

        def per_expert(e, total):
            n = tchunk_ref[e]
            st = tstart_ref[e]

            def piece(c, carry):
                zero_copy(pl.multiple_of(st + c * MOE_CHUNK, MOE_CHUNK)).start()
                return carry

            lax.fori_loop(0, n, piece, 0)
            return total + n

        total_z = lax.fori_loop(0, N_EXPERTS, per_expert, 0)

        def wait_zero(c, carry):
            zero_copy(0).wait()
            return carry

        lax.fori_loop(0, total_z, wait_zero, 0)


def _dispatch_call(layer, h, pos_t, tb):
    return pl.pallas_call(
        _dispatch_kernel,
        grid_spec=pltpu.PrefetchScalarGridSpec(
            num_scalar_prefetch=4,
            grid=(NW,),
            in_specs=[_row_spec(D_MODEL), pl.BlockSpec((1, SUBLANES, TM), lambda i, *_: (i, 0, 0))],
            out_specs=pl.BlockSpec(memory_space=pl.ANY),
            scratch_shapes=[pltpu.VMEM((2, R_LOC, D_MODEL), BF16), pltpu.VMEM((MOE_CHUNK, D_MODEL), BF16),
                            pltpu.SemaphoreType.DMA((2,))],
        ),
        out_shape=jax.ShapeDtypeStruct((SORTED_ROWS, D_MODEL), BF16),
        compiler_params=_params("arbitrary"),
        name=f"moe_dispatch{layer}",
    )(tb["ngroups"], tb["dst"], tb["tail_start"], tb["tail_chunks"], h, pos_t)


def _expert_kernel(texp_ref, nvalid_ref, hs_ref, wg_ref, wu_ref, wd_ref, ys_ref, wg_b, wu_b, wd_b):
    i = pl.program_id(0)
    fresh = jnp.logical_or(i == 0, texp_ref[i] != texp_ref[jnp.maximum(i - 1, 0)])

    @pl.when(fresh)
    def _():
        wg_b[...] = wg_ref[0].astype(BF16)
        wu_b[...] = wu_ref[0].astype(BF16)
        wd_b[...] = wd_ref[0].astype(BF16)

    @pl.when(i < nvalid_ref[0])
    def _():
        h = hs_ref[...]
        a = _dot(h, wg_b[...])
        b = _dot(h, wu_b[...])
        ys_ref[...] = _dot((a * _sigmoid(a) * b).astype(BF16), wd_b[...]).astype(BF16)


def _expert_call(layer, hs, tb, w_gate, w_up, w_down):
    tile_map = lambda i, texp, nv: (jnp.minimum(i, nv[0] - 1), 0)
    w_map = lambda i, texp, nv: (layer, texp[i], 0, 0)
    return pl.pallas_call(
        _expert_kernel,
        grid_spec=pltpu.PrefetchScalarGridSpec(
            num_scalar_prefetch=2,
            grid=(NT_MAX,),
            in_specs=[
                pl.BlockSpec((MOE_TILE, D_MODEL), tile_map),
                pl.BlockSpec((None, 1, D_MODEL, EXPERT_FF), w_map),
                pl.BlockSpec((None, 1, D_MODEL, EXPERT_FF), w_map),
                pl.BlockSpec((None, 1, EXPERT_FF, D_MODEL), w_map),
            ],
            out_specs=pl.BlockSpec((MOE_TILE, D_MODEL), tile_map),
            scratch_shapes=[pltpu.VMEM((D_MODEL, EXPERT_FF), BF16), pltpu.VMEM((D_MODEL, EXPERT_FF), BF16),
                            pltpu.VMEM((EXPERT_FF, D_MODEL), BF16)],
        ),
        out_shape=jax.ShapeDtypeStruct((NT_MAX * MOE_TILE, D_MODEL), BF16),
        compiler_params=_params("arbitrary"),
        name=f"moe_experts{layer}",
    )(tb["texp"], tb["nvalid"], hs, w_gate, w_up, w_down)


def _combine_kernel(split_out, ngroups_ref, src_ref, pos_ref, x_ref, g2_ref, ys_ref, *refs):
    o_refs, (buf, sems) = refs[:-2], refs[-2:]
    w = pl.program_id(0)
    slot = w % 2

    def fetch(window, s):
        def piece_copy(local_row, global_row):
            return pltpu.make_async_copy(ys_ref.at[pl.ds(global_row, MOE_CHUNK)],
                                         buf.at[s, pl.ds(local_row, MOE_CHUNK)], sems.at[s])

        _for_each_piece(window, ngroups_ref, src_ref, lambda lo, go, prio: piece_copy(lo, go).start(priority=prio))

    @pl.when(w == 0)
    def _():
        buf[...] = jnp.zeros_like(buf)
        fetch(0, 0)

    @pl.when(w + 1 < NW)
    def _():
        fetch(w + 1, 1 - slot)

    pos = pos_ref[...]
    pa, pb, wa, wb = pos[:, 0:1], pos[:, 1:2], pos[:, 2:3], pos[:, 3:4]
    r = lax.broadcasted_iota(jnp.int32, (TM, R_LOC), 1).astype(F32)
    wt = jnp.where(r == pa, wa, jnp.where(r == pb, wb, 0.0)).astype(BF16)

    def wait_group(c, carry):
        pltpu.make_async_copy(ys_ref.at[pl.ds(0, PIECE_GROUP * MOE_CHUNK)],
                              buf.at[slot, pl.ds(0, PIECE_GROUP * MOE_CHUNK)], sems.at[slot]).wait()
        return carry

    lax.fori_loop(0, ngroups_ref[w], wait_group, 0)
    y = x_ref[...] + g2_ref[0] * _dot(wt, buf[slot])
    if split_out:
        @pl.when(w < NB_CTX)
        def _():
            o_refs[0][...] = y

        @pl.when(w >= NB_CTX)
        def _():
            o_refs[1][...] = y
    else:
        o_refs[0][...] = y


def _combine_call(layer, pos, x, mod, ys, tb, split_out):
    if split_out:
        out_specs = _split_specs(D_MODEL)
        out_shape = [jax.ShapeDtypeStruct((N_CTX, D_MODEL), F32), jax.ShapeDtypeStruct((N_LAT, D_MODEL), F32)]
    else:
        out_specs = [_row_spec(D_MODEL)]
        out_shape = [jax.ShapeDtypeStruct((N_TOK, D_MODEL), F32)]
    return pl.pallas_call(
        functools.partial(_combine_kernel, split_out),
        grid_spec=pltpu.PrefetchScalarGridSpec(
            num_scalar_prefetch=2,
            grid=(NW,),
            in_specs=[_row_spec(LANES), _row_spec(D_MODEL), _mod_spec(layer, 5), pl.BlockSpec(memory_space=pl.ANY)],
            out_specs=out_specs,
            scratch_shapes=[pltpu.VMEM((2, R_LOC, D_MODEL), BF16), pltpu.SemaphoreType.DMA((2,))],
        ),
        out_shape=out_shape,
        compiler_params=_params("arbitrary"),
        name=f"moe_combine{layer}",
    )(tb["ngroups"], tb["src"], pos, x, mod, ys)


def _moe_call(layer, h, routing, w_gate, w_up, w_down, x, mod, split_out=False):
    pos, pos_t, cnt = routing
    tb = _moe_tables(cnt)
    hs = _dispatch_call(layer, h, pos_t, tb)
    ys = _expert_call(layer, hs, tb, w_gate, w_up, w_down)
    return _combine_call(layer, pos, x, mod, ys, tb, split_out)


def _group_rms(x, g_ref, ones_blocks, n):
    ss = _dot((x * x).astype(BF16), ones_blocks)
    return x * lax.rsqrt(ss * (1.0 / n) + EPS) * g_ref[...]


def _inproj1_kernel(x_ref, g_ref, sh_ref, sc_ref, w_ref, qg_ref, kg_ref, c_ref, s1_ref, s2_ref,
                    q_ref, k_ref, v_ref, kn_ref, vf_ref, kn_buf):
    i = pl.program_id(0)
    h = (_rms(x_ref[...], g_ref[...]) * (1.0 + sc_ref[0]) + sh_ref[0]).astype(BF16)
    hw = DIFF_HEADS * DIFF_DV
    half = DIFF_DK // 2
    ones_blocks = _ones_blocks(DIFF_DK.bit_length() - 1)
    n_pairs = DIFF_HEADS // 2

    def project(tp):
        return (_dot(h, w_ref[:, tp * 2 * LANES:(tp + 1) * 2 * LANES]),
                _dot(h, w_ref[:, hw + tp * 2 * LANES:hw + (tp + 1) * 2 * LANES]))

    nxt = project(0)
    for tp in range(n_pairs):
        pq, pk = nxt
        if tp + 1 < n_pairs:
            nxt = project(tp + 1)
        for u in range(2):
            sl = slice((2 * tp + u) * LANES, (2 * tp + u + 1) * LANES)
            qt = _group_rms(pq[:, u * LANES:(u + 1) * LANES], qg_ref, ones_blocks, DIFF_DK)
            q_ref[:, sl] = _rope_tile(qt, c_ref, s1_ref, s2_ref, half).astype(BF16)
            kt = _group_rms(pk[:, u * LANES:(u + 1) * LANES], kg_ref, ones_blocks, DIFF_DK)
            k_ref[:, sl] = _rope_tile(kt, c_ref, s1_ref, s2_ref, half).astype(BF16)
            kn_buf[:, sl] = kt

    v = _dot(h, w_ref[:, 2 * hw:])
    v_ref[...] = v.astype(BF16)

    @pl.when(i < NB_CTX)
    def _():
        kn_ref[...] = kn_buf[...]
        vf_ref[...] = v


def _inproj1_call(x, mod, g1, w_qkv, qg, kg, tables):
    hw = DIFF_HEADS * DIFF_DV
    tspec = pl.BlockSpec((TM, LANES), lambda i: (_rope_block(i), 0))
    ctx_spec = pl.BlockSpec((TM, hw), lambda i: (jnp.minimum(i, NB_CTX - 1), 0))
    return pl.pallas_call(
        _inproj1_kernel,
        grid=(NB_TOK,),
        in_specs=[
            _row_spec(D_MODEL), _full_spec((1, D_MODEL)), _mod_spec(1, 0), _mod_spec(1, 1),
            _full_spec((D_MODEL, 3 * hw)), _full_spec((1, LANES)), _full_spec((1, LANES)), tspec, tspec, tspec,
        ],
        out_specs=[_row_spec(hw), _row_spec(hw), _row_spec(hw), ctx_spec, ctx_spec],
        out_shape=[
            jax.ShapeDtypeStruct((N_TOK, hw), BF16),
            jax.ShapeDtypeStruct((N_TOK, hw), BF16),
            jax.ShapeDtypeStruct((N_TOK, hw), BF16),
            jax.ShapeDtypeStruct((N_CTX, hw), F32),
            jax.ShapeDtypeStruct((N_CTX, hw), F32),
        ],
        scratch_shapes=[pltpu.VMEM((TM, DIFF_HEADS * DIFF_DV), F32)],
        compiler_params=_params("arbitrary"),
        name="inproj1",
    )(x, g1, mod, mod, w_qkv, qg, kg, *tables)


def _diff_attn_kernel(has_cache, lambda_init, tq, n_heads, qg_ref, kg_ref, *refs):
    if has_cache:
        q_ref, k_ref, v_ref, kc_ref, vc_ref, lq1, lk1, lq2, lk2, sg_ref, o_ref = refs
        k_refs, v_refs = [k_ref, kc_ref], [v_ref, vc_ref]
    else:
        q_ref, k_ref, v_ref, lq1, lk1, lq2, lk2, sg_ref, o_ref = refs
        k_refs, v_refs = [k_ref], [v_ref]
    lam = (jnp.exp(jnp.sum(lq1[...] * lk1[...], axis=-1, keepdims=True))
           - jnp.exp(jnp.sum(lq2[...] * lk2[...], axis=-1, keepdims=True)) + lambda_init)
    lo = lax.broadcasted_iota(jnp.int32, (1, LANES), 1) < DIFF_DK
    k_sq = _gain_sq_bound(kg_ref, DIFF_DK)
    if has_cache:
        for mask in (lo, jnp.logical_not(lo)):
            k_sq = jnp.maximum(k_sq, _max_sq_norm(kc_ref[...], mask))
    bound_sq = _gain_sq_bound(qg_ref, DIFF_DK) * k_sq

    def logits(r0, head):
        sl = slice(head * LANES, (head + 1) * LANES)
        q = q_ref[pl.ds(r0, tq), sl]
        zero = jnp.zeros_like(q)
        return ([_dot_nt(jnp.where(lo, q, zero), kr[:, sl]) for kr in k_refs],
                [_dot_nt(jnp.where(lo, zero, q), kr[:, sl]) for kr in k_refs])

    def finish(shifted, r0, head, s):
        sl = slice(head * LANES, (head + 1) * LANES)
        e0, l0 = _exps(s[0], shifted)
        e1, l1 = _exps(s[1], shifted)
        f0 = 1.0 / l0
        f1 = lam / l1
        o = None
        for a, b, vr in zip(e0, e1, v_refs):
            t = _dot((a * f0 - b * f1).astype(BF16), vr[:, sl])
            o = t if o is None else o + t
        o = o * lax.rsqrt(jnp.mean(o * o, axis=-1, keepdims=True) + EPS) * sg_ref[...] * (1.0 - lambda_init)
        o_ref[pl.ds(r0, tq), sl] = o.astype(BF16)

    _run_query_blocks(bound_sq, q_ref.shape[0] // tq, tq, n_heads, logits, finish)


def _diff_attn_ctx_call(lambda_init, gains, q, k, v, lams, sg):
    lspec = _full_spec((1, DIFF_DK))
    gspec = _full_spec((1, LANES))
    return pl.pallas_call(
        functools.partial(_diff_attn_kernel, False, lambda_init, SEQ, DIFF_HEADS),
        grid=(BATCH,),
        in_specs=[gspec, gspec] + [pl.BlockSpec((SEQ, DIFF_HEADS * LANES), lambda b: (b, 0))] * 3 + [lspec] * 4
        + [_full_spec((1, DIFF_DV))],
        out_specs=pl.BlockSpec((SEQ, DIFF_HEADS * LANES), lambda b: (b, 0)),
        out_shape=jax.ShapeDtypeStruct((N_CTX, DIFF_HEADS * DIFF_DV), BF16),
        compiler_params=_params("parallel"),
        name="diff_attn_ctx",
    )(*gains, q, k, v, *lams, sg)


def _diff_attn_lat_call(lambda_init, gains, q, k, v, kc, vc, lams, sg):
    k0 = N_CTX // DEC_SEQ
    lspec = _full_spec((1, DIFF_DK))
    gspec = _full_spec((1, LANES))
    seq_spec = pl.BlockSpec((DEC_SEQ, LANES), lambda b, h: (k0 + b, h))
    c_spec = pl.BlockSpec((PAST_LEN, LANES), lambda b, h: (b, h))
    return pl.pallas_call(
        functools.partial(_diff_attn_kernel, True, lambda_init, TQ, 1),
        grid=(DEC_BATCH, DIFF_HEADS),
        in_specs=[gspec, gspec, seq_spec, seq_spec, seq_spec, c_spec, c_spec] + [lspec] * 4 + [_full_spec((1, DIFF_DV))],
        out_specs=pl.BlockSpec((DEC_SEQ, LANES), lambda b, h: (b, h)),
        out_shape=jax.ShapeDtypeStruct((N_LAT, DIFF_HEADS * DIFF_DV), BF16),
        compiler_params=_params("parallel", "parallel"),
        name="diff_attn_lat",
    )(*gains, q, k, v, kc, vc, *lams, sg)


def _router_weights(we, wg, be, bg):
    pad = LANES - N_EXPERTS - N_GROUPS
    w = jnp.concatenate([we, wg, jnp.zeros((D_MODEL, pad), F32)], axis=1)
    b = jnp.concatenate([be, bg, jnp.zeros((pad,), F32)])[None]
    hi = w.astype(BF16)
    lo = (w - hi.astype(F32)).astype(BF16)
    return hi, lo, b


def kernel(x_prompt, x_sample, cache_mla_ckv, cache_mla_krope, cache_diff_k, cache_diff_v, c, c_ctx, ada_w, ada_b, norm1_g, norm2_g, ev_w_in, ev_conv_w, ev_conv_b, ev_conv_ln_g, ev_conv_ln_b, ev_q_norm_g, ev_w_qb, ev_kv_norm_g, ev_w_kvb, ev_qn_g, ev_kn_g, ev_w_o, od_w_qkv, od_qn_g, od_kn_g, od_lambda_q1, od_lambda_k1, od_lambda_q2, od_lambda_k2, od_subln_g, od_w_o, moe_wg, moe_bg, moe_we, moe_be, moe_w_gate, moe_w_up, moe_w_down):
    x_in = (x_prompt.reshape(N_CTX, D_MODEL), x_sample.reshape(N_LAT, D_MODEL))
    cond = jnp.concatenate([c_ctx[None, :], c, jnp.zeros((MOD_ROWS - 1 - DEC_BATCH, D_MODEL), F32)], axis=0)
    mod = _mod_call(cond, ada_w, ada_b).reshape(-1, 1, D_MODEL)

    o3 = 2 * CONV_CH + MLA_Q_RANK + MLA_KV_RANK
    w_in = jnp.concatenate([ev_w_in[0][:, :o3], jnp.zeros((D_MODEL, MLA_NOPE), F32), ev_w_in[0][:, o3:],
                            jnp.zeros((D_MODEL, LANES - MLA_QK), F32)], axis=1).astype(BF16)
    u, qn, ckv, kr = _inproj0_call(*x_in, mod, norm1_g[0][None], w_in, ev_q_norm_g[0][None], ev_kv_norm_g[0][None])
    conv = _conv_call(u, ev_conv_w[0], ev_conv_b[0][None], ev_conv_ln_g[0][None], ev_conv_ln_b[0][None])

    pad_qk = ((0, 0), (0, 0), (0, LANES - MLA_QK))
    wq = jnp.pad(ev_w_qb[0].reshape(MLA_Q_RANK, MLA_HEADS, MLA_QK), pad_qk).reshape(MLA_Q_RANK, -1).astype(BF16)
    wkv = ev_w_kvb[0].reshape(MLA_KV_RANK, MLA_HEADS, MLA_NOPE + MLA_V)
    wk = jnp.pad(wkv[..., :MLA_NOPE], ((0, 0), (0, 0), (0, LANES - MLA_NOPE))).reshape(MLA_KV_RANK, -1).astype(BF16)
    wv = wkv[..., MLA_NOPE:].reshape(MLA_KV_RANK, -1).astype(BF16)
    qg = jnp.pad(ev_qn_g[0] * (MLA_QK ** -0.5 * LOG2E), (0, LANES - MLA_QK))[None]
    kg = jnp.pad(ev_kn_g[0], (0, LANES - MLA_QK))[None]
    tables = _rope_tables(MLA_ROPE, (MLA_NOPE,), DEC_SEQ)
    q0, k0, v0 = _mla_prep_call(qn, ckv, kr, wq, wk, wv, qg, kg, tables, _rope_block)
    kr_cache = jnp.pad(cache_mla_krope[:, 0].reshape(N_CACHE, MLA_ROPE), ((0, 0), (MLA_NOPE, LANES - MLA_QK)))
    _, kc0, vc0 = _mla_prep_call(jnp.zeros((N_CACHE, MLA_Q_RANK), BF16), cache_mla_ckv[:, 0].reshape(N_CACHE, MLA_KV_RANK),
                                 kr_cache, wq, wk, wv, qg, kg, tables, lambda i: BLK_PER_LAT)
    attn = (_mla_attn_ctx_call(qg, kg, q0, k0, v0), _mla_attn_lat_call(qg, kg, q0, k0, v0, kc0, vc0))

    w_o = ev_w_o[0].astype(BF16)
    r_hi, r_lo, r_b = _router_weights(moe_we[0], moe_wg[0], moe_be[0], moe_bg[0])
    x, h2, *routing = _outproj_call(0, [(conv,), attn], [w_o[:CONV_CH], w_o[CONV_CH:]], x_in, mod, norm2_g[0][None], r_hi, r_lo, r_b)
    (x,) = _moe_call(0, h2, routing, moe_w_gate, moe_w_up, moe_w_down, x, mod)

    lambda_init = 0.8 - 0.6 * math.exp(-0.3 * 1)
    qg1 = jnp.tile(od_qn_g[0] * (DIFF_DK ** -0.5 * LOG2E), 2)[None]
    kg1 = jnp.tile(od_kn_g[0], 2)[None]
    tables1 = _rope_tables(DIFF_DK, (0, DIFF_DK), DEC_SEQ)
    q1, k1, v1, kn1, vf1 = _inproj1_call(x, mod, norm1_g[1][None], od_w_qkv[0].astype(BF16), qg1, kg1, tables1)
    lams = [od_lambda_q1[0][None], od_lambda_k1[0][None], od_lambda_q2[0][None], od_lambda_k2[0][None]]
    sg = od_subln_g[0][None]
    hw = DIFF_HEADS * DIFF_DV
    kc1 = cache_diff_k[:, 0].reshape(N_CACHE, hw).astype(BF16)
    vc1 = cache_diff_v[:, 0].reshape(N_CACHE, hw).astype(BF16)
    attn1 = (_diff_attn_ctx_call(lambda_init, (qg1, kg1), q1, k1, v1, lams, sg),
             _diff_attn_lat_call(lambda_init, (qg1, kg1), q1, k1, v1, kc1, vc1, lams, sg))
    r_hi, r_lo, r_b = _router_weights(moe_we[1], moe_wg[1], moe_be[1], moe_bg[1])
    x, h2, *routing = _outproj_call(1, [attn1], [od_w_o[0].astype(BF16)], (x,), mod, norm2_g[1][None], r_hi, r_lo, r_b)
    y_ctx, y_lat = _moe_call(1, h2, routing, moe_w_gate, moe_w_up, moe_w_down, x, mod, split_out=True)

    y_prompt = y_ctx.reshape(BATCH, SEQ, D_MODEL)
    y_sample = y_lat.reshape(DEC_BATCH, DEC_SEQ, D_MODEL)
    new_mla_ckv = ckv[:N_CTX].reshape(BATCH, 1, SEQ, MLA_KV_RANK)
    new_mla_krope = kr[:N_CTX, MLA_NOPE:MLA_QK].reshape(BATCH, 1, SEQ, MLA_ROPE)
    new_diff_k = kn1.reshape(BATCH, 1, SEQ, DIFF_HEADS, 2, DIFF_DK)
    new_diff_v = vf1.reshape(BATCH, 1, SEQ, DIFF_HEADS, DIFF_DV)
    return (y_prompt, y_sample, new_mla_ckv, new_mla_krope, new_diff_k, new_diff_v)
```
